```python
import math
import jax, jax.numpy as jnp
from jax import lax
import numpy as np

D_MODEL = 2048
BATCH = 2
SEQ = 4096
DEPTH = 2
DEC_BATCH = 128
DEC_SEQ = 1
PAST_LEN = 2048
PAGE_SIZE = 128

EPS = 1e-6
DVA = 128
DKA = 64
HA = D_MODEL // (2 * DVA)
LAMBDA_INIT = 0.8 - 0.6 * math.exp(-0.3 * 0)
Q_BLOCK = 128
DVB = 128
DKB = 64
HB = D_MODEL // (2 * DVB)
RET_CHUNK = 128
ROPE_BASE = 10000.0
W_IN0 = 2 * HA * 2 * DKA + HA * DVA + 2 * HB * DKB + 2 * HB * DVB
NC = 64
HC = D_MODEL // NC
D_DECAY_LORA = 96
D_AAA_LORA = 96
D_GATE_LORA = 256
GN_EPS = 64e-5
D_FF = -(-8 * D_MODEL // (3 * 256)) * 256

kernel_name = 'diffattn_retnet_rwkv7_hybrid_step'


def rmsnorm(x, gain=None, eps=EPS):
    xf = x.astype(jnp.float32)
    y = xf * lax.rsqrt(jnp.mean(xf * xf, axis=-1, keepdims=True) + eps)
    if gain is not None:
        y = y * gain.astype(jnp.float32)
    return y.astype(x.dtype)


def swiglu(h, w_gate, w_up, w_down):
    return (jax.nn.silu(h @ w_gate) * (h @ w_up)) @ w_down


def alibi_slopes(n_heads):
    return 2.0 ** (-8.0 * jnp.arange(1, n_heads + 1, dtype=jnp.float32) / n_heads)


def retention_rotate(x, pos):
    d = x.shape[-1]
    theta = 1.0 / (ROPE_BASE ** jnp.linspace(0.0, 1.0, d // 2, dtype=jnp.float32))
    ang = pos.astype(jnp.float32)[:, None] * theta[None, :]
    cos = jnp.cos(ang)[None, :, None, :]
    sin = jnp.sin(ang)[None, :, None, :]
    xf = x.astype(jnp.float32).reshape(x.shape[:-1] + (d // 2, 2))
    x1, x2 = xf[..., 0], xf[..., 1]
    return jnp.stack([x1 * cos - x2 * sin, x1 * sin + x2 * cos], axis=-1).reshape(x.shape)


def diff_attn_block(q, k, v, q_pos, k_pos, lam, slopes):
    rel = (q_pos[:, None] - k_pos[None, :]).astype(jnp.float32)
    bias = jnp.where(rel[None] >= 0, -slopes[:, None, None] * rel[None], -1e30)
    scale = DKA ** -0.5

    def smap(qh, kh):
        s = jnp.einsum('bqhd,bshd->bhqs', qh, kh).astype(jnp.float32) * scale + bias[None]
        return jax.nn.softmax(s, axis=-1)

    w = smap(q[..., :DKA], k[..., :DKA]) - lam * smap(q[..., DKA:], k[..., DKA:])
    return jnp.einsum('bhqs,bshe->bqhe', w.astype(v.dtype), v)


def diff_attention_prompt(q, k, v, pos, lam, slopes):
    B, T = q.shape[:2]
    nb = T // Q_BLOCK
    qb = jnp.moveaxis(q.reshape((B, nb, Q_BLOCK) + q.shape[2:]), 1, 0)
    pb = pos.reshape(nb, Q_BLOCK)
    o = lax.map(lambda a: diff_attn_block(a[0], k, v, a[1], pos, lam, slopes), (qb, pb))
    return jnp.moveaxis(o, 0, 1).reshape((B, T) + o.shape[3:])


def retention_chunk(S, qkv, log_gamma):
    q, k, v = qkv
    C = q.shape[1]
    idx = jnp.arange(C, dtype=jnp.float32)
    diff = idx[:, None] - idx[None, :]
    decay = jnp.where(diff[None] >= 0,
                      jnp.exp(log_gamma[:, None, None] * jnp.maximum(diff, 0.0)[None]), 0.0)
    scores = jnp.einsum('bihd,bjhd->bhij', q, k) * decay[None]
    o = jnp.einsum('bhij,bjhe->bihe', scores, v)
    q_decay = jnp.exp(log_gamma[None, :] * (idx[:, None] + 1.0))
    o = o + jnp.einsum('bihd,bhde->bihe', q, S) * q_decay[None, :, :, None]
    k_decay = jnp.exp(log_gamma[None, :] * (C - 1.0 - idx[:, None]))
    S_new = (jnp.exp(log_gamma * C)[None, :, None, None] * S
             + jnp.einsum('bjhd,bjhe->bhde', k * k_decay[None, :, :, None], v))
    return S_new, o


def retention(q, k, v, S0, log_gamma):
    B, T = q.shape[:2]
    C = math.gcd(T, RET_CHUNK)
    n = T // C

    def to_chunks(t):
        return jnp.moveaxis(t.reshape((B, n, C) + t.shape[2:]), 1, 0)

    S, o = lax.scan(lambda s, c: retention_chunk(s, c, log_gamma), S0,
                    (to_chunks(q), to_chunks(k), to_chunks(v)))
    return jnp.moveaxis(o, 0, 1).reshape((B, T) + o.shape[3:]), S


def diff_retention_mixer(h, pos, kv_past, s_ret, P):
    B, T, _ = h.shape
    f32 = jnp.float32
    sizes = (HA * 2 * DKA, HA * 2 * DKA, HA * DVA, HB * DKB, HB * DKB, HB * DVB, HB * DVB)
    z = h @ P['w_in']
    qa, ka, va, qb, kb, vb, gb = jnp.split(z, [int(s) for s in np.cumsum(sizes)[:-1]], axis=-1)
    qa = qa.reshape(B, T, HA, 2 * DKA)
    ka = ka.reshape(B, T, HA, 2 * DKA)
    va = va.reshape(B, T, HA, DVA)
    lam = (jnp.exp(jnp.sum(P['lambda_q1'].astype(f32) * P['lambda_k1'].astype(f32)))
           - jnp.exp(jnp.sum(P['lambda_q2'].astype(f32) * P['lambda_k2'].astype(f32)))
           + LAMBDA_INIT)
    slopes = alibi_slopes(HA)
    if kv_past is None:
        o_a = diff_attention_prompt(qa, ka, va, pos, lam, slopes)
    else:
        k_past, v_past = kv_past
        past = k_past.shape[1]
        k_all = jnp.concatenate([k_past.astype(ka.dtype), ka], axis=1)
        v_all = jnp.concatenate([v_past.astype(va.dtype), va], axis=1)
        k_pos = jnp.concatenate([jnp.arange(past, dtype=jnp.int32), pos])
        o_a = diff_attn_block(qa, k_all, v_all, pos, k_pos, lam, slopes)
    o_a = rmsnorm(o_a, P['subln_gain']) * (1.0 - LAMBDA_INIT)
    log_gamma = jnp.log(1.0 - 2.0 ** (-5.0 - jnp.arange(HB, dtype=f32)))
    qr = retention_rotate(qb.reshape(B, T, HB, DKB), pos)
    kr = retention_rotate(kb.reshape(B, T, HB, DKB), pos) * (DKB ** -0.5)
    vr = vb.reshape(B, T, HB, DVB).astype(f32)
    o_b, s_new = retention(qr, kr, vr, s_ret.astype(f32), log_gamma)
    o_b = rmsnorm(o_b).reshape(B, T, HB * DVB) * jax.nn.silu(gb.astype(f32))
    out = jnp.concatenate([o_a.reshape(B, T, HA * DVA), o_b.astype(h.dtype)], axis=-1) @ P['w_out']
    return out, ka, va, s_new


def rwkv7_mixer(h, x_prev, S0, P):
    B, T, D = h.shape
    f32 = jnp.float32
    shifted = jnp.concatenate([x_prev[:, None, :].astype(h.dtype), h[:, :-1]], axis=1)
    xx = shifted - h
    mu = P['mu']
    xr, xw, xk, xv, xa, xg = (h + xx * mu[i] for i in range(6))
    r = xr @ P['w_r']
    k = xk @ P['w_k']
    v = xv @ P['w_v']
    w = -jax.nn.softplus(-(P['w0'] + jnp.tanh(xw @ P['w1']) @ P['w2']).astype(f32)) - 0.5
    decay = jnp.exp(-jnp.exp(w))
    a = jax.nn.sigmoid((P['a0'] + (xa @ P['a1']) @ P['a2']).astype(f32))
    g = jax.nn.sigmoid(xg @ P['g1']) @ P['g2']

    def heads(t):
        return t.astype(f32).reshape(B, T, HC, NC)

    kf = k.astype(f32)
    kk = heads(kf * P['k_k'].astype(f32))
    kk = kk / jnp.maximum(jnp.sqrt(jnp.sum(kk * kk, axis=-1, keepdims=True)), 1e-12)
    kf = kf * (1.0 + (a - 1.0) * P['k_a'].astype(f32))
    rh, wh, kh, vh, ah = heads(r), heads(decay), heads(kf), heads(v), heads(a)

    def step(S, inp):
        r_t, w_t, k_t, v_t, kk_t, a_t = inp
        sk = jnp.einsum('bhij,bhj->bhi', S, kk_t)
        S = (S * w_t[:, :, None, :] - sk[..., None] * (kk_t * a_t)[:, :, None, :]
             + v_t[..., None] * k_t[:, :, None, :])
        return S, jnp.einsum('bhij,bhj->bhi', S, r_t)

    S, o = lax.scan(step, S0.astype(f32),
                    tuple(jnp.moveaxis(t, 1, 0) for t in (rh, wh, kh, vh, kk, ah)))
    o = jnp.moveaxis(o, 0, 1)
    mean = jnp.mean(o, axis=-1, keepdims=True)
    var = jnp.mean(jnp.square(o - mean), axis=-1, keepdims=True)
    o = ((o - mean) * lax.rsqrt(var + GN_EPS)).reshape(B, T, D) * P['lnx_gain'].astype(f32) + P['lnx_bias'].astype(f32)
    bonus = jnp.sum(rh * kh * P['r_k'].astype(f32), axis=-1, keepdims=True) * vh
    o = o + bonus.reshape(B, T, D)
    out = (o * g.astype(f32)).astype(h.dtype) @ P['w_out']
    return out, h[:, -1], S


def trunk(x, pos, kv_past, s_ret, s_wkv, s_shift, P0, P1, final_gain):
    h = x
    for layer in range(DEPTH):
        if layer % 2 == 0:
            mix, k_rows, v_rows, ret_new = diff_retention_mixer(rmsnorm(h, P0['norm_mix']), pos, kv_past, s_ret, P0)
            P = P0
        else:
            mix, shift_new, wkv_new = rwkv7_mixer(rmsnorm(h, P1['norm_mix']), s_shift, s_wkv, P1)
            P = P1
        h = h + mix
        h = h + swiglu(rmsnorm(h, P['norm_ffn']), P['w_gate'], P['w_up'], P['w_down'])
    return rmsnorm(h, final_gain), k_rows, v_rows, ret_new, wkv_new, shift_new


def setup_inputs(seed: int = 0) -> dict:
    key = jax.random.key(seed)
    keys = jax.random.split(key, 64)
    counter = [0]

    def nxt():
        k = keys[counter[0]]
        counter[0] += 1
        return k

    def nrm(shape, scale):
        return jax.random.normal(nxt(), shape, jnp.float32) * scale

    def gain(n):
        return 1.0 + nrm((n,), 0.01)

    D = D_MODEL
    n_pages = PAST_LEN // PAGE_SIZE
    n_used = DEC_BATCH * n_pages
    n_pool = n_used + n_used // 4
    page_table = jax.random.permutation(keys[63], n_pool)[:n_used].reshape(DEC_BATCH, n_pages).astype(jnp.int32)
    w_mix_out = HA * DVA + HB * DVB
    return {
        'x_prompt': nrm((BATCH, SEQ, D), 1.0),
        'x_sample': nrm((DEC_BATCH, DEC_SEQ, D), 1.0),
        'cache_k': nrm((n_pool, PAGE_SIZE, HA, 2 * DKA), 1.0),
        'cache_v': nrm((n_pool, PAGE_SIZE, HA, DVA), 1.0),
        'page_table': page_table,
        'state_ret': nrm((DEC_BATCH, HB, DKB, DVB), 0.1),
        'state_wkv': nrm((DEC_BATCH, HC, NC, NC), 0.1),
        'state_shift': nrm((DEC_BATCH, D), 1.0),
        'norm0_mix': gain(D),
        'w_in0': nrm((D, W_IN0), D ** -0.5),
        'lambda_q1': nrm((DKA,), 0.1),
        'lambda_k1': nrm((DKA,), 0.1),
        'lambda_q2': nrm((DKA,), 0.1),
        'lambda_k2': nrm((DKA,), 0.1),
        'subln_gain': gain(DVA),
        'w_out0': nrm((w_mix_out, D), w_mix_out ** -0.5),
        'norm0_ffn': gain(D),
        'w_gate0': nrm((D, D_FF), D ** -0.5),
        'w_up0': nrm((D, D_FF), D ** -0.5),
        'w_down0': nrm((D_FF, D), D_FF ** -0.5),
        'norm1_mix': gain(D),
        'mu1': jax.random.uniform(nxt(), (6, D), jnp.float32),
        'w_r1': nrm((D, D), D ** -0.5),
        'w_k1': nrm((D, D), D ** -0.5),
        'w_v1': nrm((D, D), D ** -0.5),
        'decay_w0': jnp.linspace(-6.0, -1.0, D, dtype=jnp.float32) + nrm((D,), 0.1),
        'decay_w1': nrm((D, D_DECAY_LORA), D ** -0.5),
        'decay_w2': nrm((D_DECAY_LORA, D), 0.1 * D_DECAY_LORA ** -0.5),
        'aaa_a0': nrm((D,), 0.1),
        'aaa_a1': nrm((D, D_AAA_LORA), D ** -0.5),
        'aaa_a2': nrm((D_AAA_LORA, D), D_AAA_LORA ** -0.5),
        'gate_g1': nrm((D, D_GATE_LORA), D ** -0.5),
        'gate_g2': nrm((D_GATE_LORA, D), D_GATE_LORA ** -0.5),
        'k_k': 0.85 + nrm((D,), 0.02),
        'k_a': 1.0 + nrm((D,), 0.02),
        'r_k': nrm((HC, NC), 0.1),
        'lnx_gain': gain(D),
        'lnx_bias': nrm((D,), 0.01),
        'w_out1': nrm((D, D), D ** -0.5),
        'norm1_ffn': gain(D),
        'w_gate1': nrm((D, D_FF), D ** -0.5),
        'w_up1': nrm((D, D_FF), D ** -0.5),
        'w_down1': nrm((D_FF, D), D_FF ** -0.5),
        'norm_final': gain(D),
    }


def reference(x_prompt, x_sample, cache_k, cache_v, page_table, state_ret, state_wkv, state_shift,
              norm0_mix, w_in0, lambda_q1, lambda_k1, lambda_q2, lambda_k2, subln_gain, w_out0,
              norm0_ffn, w_gate0, w_up0, w_down0,
              norm1_mix, mu1, w_r1, w_k1, w_v1, decay_w0, decay_w1, decay_w2, aaa_a0, aaa_a1, aaa_a2,
              gate_g1, gate_g2, k_k, k_a, r_k, lnx_gain, lnx_bias, w_out1,
              norm1_ffn, w_gate1, w_up1, w_down1, norm_final):
    f32 = jnp.float32
    P0 = dict(norm_mix=norm0_mix, w_in=w_in0, lambda_q1=lambda_q1, lambda_k1=lambda_k1,
              lambda_q2=lambda_q2, lambda_k2=lambda_k2, subln_gain=subln_gain, w_out=w_out0,
              norm_ffn=norm0_ffn, w_gate=w_gate0, w_up=w_up0, w_down=w_down0)
    P1 = dict(norm_mix=norm1_mix, mu=mu1, w_r=w_r1, w_k=w_k1, w_v=w_v1, w0=decay_w0, w1=decay_w1,
              w2=decay_w2, a0=aaa_a0, a1=aaa_a1, a2=aaa_a2, g1=gate_g1, g2=gate_g2, k_k=k_k, k_a=k_a,
              r_k=r_k, lnx_gain=lnx_gain, lnx_bias=lnx_bias, w_out=w_out1,
              norm_ffn=norm1_ffn, w_gate=w_gate1, w_up=w_up1, w_down=w_down1)
    B, T = x_prompt.shape[:2]
    pos_p = jnp.arange(T, dtype=jnp.int32)
    y_prompt, k_p, v_p, ret_p, wkv_p, shift_p = trunk(
        x_prompt, pos_p, None,
        jnp.zeros((B, HB, DKB, DVB), f32), jnp.zeros((B, HC, NC, NC), f32),
        jnp.zeros((B, D_MODEL), x_prompt.dtype), P0, P1, norm_final)
    nb, n_pages = page_table.shape
    past = n_pages * cache_k.shape[1]
    k_past = cache_k[page_table].reshape(nb, past, HA, 2 * DKA)
    v_past = cache_v[page_table].reshape(nb, past, HA, DVA)
    pos_s = past + jnp.arange(x_sample.shape[1], dtype=jnp.int32)
    y_sample, k_s, v_s, ret_s, wkv_s, shift_s = trunk(
        x_sample, pos_s, (k_past, v_past), state_ret, state_wkv, state_shift, P0, P1, norm_final)
    return (y_prompt, y_sample, k_p, v_p, k_s, v_s, ret_p, ret_s, wkv_p, wkv_s, shift_p, shift_s)
```

```python
import functools
import math

import jax
import jax.numpy as jnp
from jax import lax
from jax.experimental import pallas as pl
from jax.experimental.pallas import tpu as pltpu

F32 = jnp.float32
BF16 = jnp.bfloat16

D_MODEL = 2048
EPS = 1e-6
DKA = 64
DVA = 128
HA = 8
LAMBDA_INIT = 0.8 - 0.6 * math.exp(-0.3 * 0)
DKB = 64
DVB = 128
HB = 8
ROPE_BASE = 10000.0
NC = 64
HC = D_MODEL // NC
GN_EPS = 64e-5
WKV_CHUNK = 64
V7X_VMEM_BYTES = 64 * 1024 * 1024
VMEM_LIMIT = 48 * 1024 * 1024


def _cparams(sem):
    return pltpu.CompilerParams(dimension_semantics=sem, vmem_limit_bytes=VMEM_LIMIT)


def _rms(x, eps=EPS):
    return x * lax.rsqrt(jnp.mean(x * x, axis=-1, keepdims=True) + eps)


def _sigmoid(x):
    return 1.0 / (1.0 + jnp.exp(-x))


def _dot(a, b):
    return jnp.dot(a, b, preferred_element_type=F32)


def _dot_nt(a, b):
    return lax.dot_general(a, b, (((1,), (1,)), ((), ())), preferred_element_type=F32)


def _dot_tn(a, b):
    return lax.dot_general(a, b, (((0,), (0,)), ((), ())), preferred_element_type=F32)


def _split2(x):
    hi = x.astype(BF16)
    lo = (x - hi.astype(F32)).astype(BF16)
    return hi, lo


def _dot_f32(a, b, dot=_dot):
    ah, al = _split2(a)
    bh, bl = _split2(b)
    return dot(ah, bh) + (dot(ah, bl) + dot(al, bh))


def _inproj0_kernel(x_ref, g_ref, w_ref, ka_ref, va_ref, z_ref, xn_ref):
    j = pl.program_id(1)

    @pl.when(j == 0)
    def _():
        xn_ref[...] = (_rms(x_ref[...]) * g_ref[...]).astype(BF16)

    acc = _dot(xn_ref[...], w_ref[...])

    @pl.when(j == 1)
    def _():
        ka_ref[...] = acc

    @pl.when(j == 2)
    def _():
        va_ref[...] = acc

    @pl.when(jnp.logical_and(j != 1, j != 2))
    def _():
        z_ref[...] = acc


def _inproj0(x, gain, w_bf16, tm):
    m = x.shape[0]
    tn = 1024
    nj = w_bf16.shape[1] // tn
    return pl.pallas_call(
        _inproj0_kernel,
        grid=(m // tm, nj),
        in_specs=[
            pl.BlockSpec((tm, D_MODEL), lambda i, j: (i, 0)),
            pl.BlockSpec((1, D_MODEL), lambda i, j: (0, 0)),
            pl.BlockSpec((D_MODEL, tn), lambda i, j: (0, j)),
        ],
        out_specs=[
            pl.BlockSpec((tm, tn), lambda i, j: (i, 0)),
            pl.BlockSpec((tm, tn), lambda i, j: (i, 0)),
            pl.BlockSpec((tm, tn), lambda i, j: (i, jnp.maximum(j - 2, 0))),
        ],
        out_shape=[
            jax.ShapeDtypeStruct((m, tn), F32),
            jax.ShapeDtypeStruct((m, tn), F32),
            jax.ShapeDtypeStruct((m, 4 * tn), F32),
        ],
        scratch_shapes=[pltpu.VMEM((tm, D_MODEL), BF16)],
        compiler_params=_cparams(("arbitrary", "arbitrary")),
        name="inproj0",
    )(x, gain.reshape(1, D_MODEL), w_bf16)


def _lambda_full(lq1, lk1, lq2, lk2):
    s1 = jnp.sum(lq1 * lk1, axis=-1, keepdims=True)
    s2 = jnp.sum(lq2 * lk2, axis=-1, keepdims=True)
    return jnp.exp(s1) - jnp.exp(s2) + LAMBDA_INIT


def _alibi_slope(h_vec):
    slope = jnp.zeros(h_vec.shape, F32)
    for h in range(HA):
        slope = jnp.where(h_vec == h, 2.0 ** (-(h + 1)), slope)
    return slope


def _subln(o, gain):
    return _rms(o) * gain * (1.0 - LAMBDA_INIT)


def _dattn_prompt_kernel(ii_ref, jj_ref, q_ref, k_ref, v_ref, lq1, lk1, lq2, lk2, gain_ref, o_ref,
                         m1, l1, a1, m2, l2, a2, *, t):
    h = pl.program_id(1)
    s = pl.program_id(2)
    i = ii_ref[s]
    j = jj_ref[s]

    @pl.when(j == 0)
    def _():
        for m_ref, l_ref, a_ref in ((m1, l1, a1), (m2, l2, a2)):
            m_ref[...] = jnp.full(m_ref.shape, -jnp.inf, F32)
            l_ref[...] = jnp.zeros(l_ref.shape, F32)
            a_ref[...] = jnp.zeros(a_ref.shape, F32)

    slope = _alibi_slope(jnp.full((1, 1), h, jnp.int32))
    row = i * t + lax.broadcasted_iota(jnp.int32, (t, t), 0)
    col = j * t + lax.broadcasted_iota(jnp.int32, (t, t), 1)
    rel = (row - col).astype(F32)
    bias = jnp.where(rel >= 0, -slope * rel, -1e30)
    q = q_ref[...].astype(BF16)
    k = k_ref[...].astype(BF16)
    v = v_ref[...].astype(BF16)
    for half, (m_ref, l_ref, a_ref) in enumerate(((m1, l1, a1), (m2, l2, a2))):
        sl = slice(half * DKA, (half + 1) * DKA)
        sc = _dot_nt(q[:, sl], k[:, sl]) * (DKA ** -0.5) + bias
        m_new = jnp.maximum(m_ref[...], jnp.max(sc, axis=-1, keepdims=True))
        alpha = jnp.exp(m_ref[...] - m_new)
        p = jnp.exp(sc - m_new)
        l_ref[...] = alpha * l_ref[...] + jnp.sum(p, axis=-1, keepdims=True)
        a_ref[...] = alpha * a_ref[...] + _dot(p.astype(BF16), v)
        m_ref[...] = m_new

    @pl.when(j == i)
    def _():
        lam = _lambda_full(lq1[...], lk1[...], lq2[...], lk2[...])
        o = a1[...] / l1[...] - lam * (a2[...] / l2[...])
        o_ref[...] = _subln(o, gain_ref[...])


def _dattn_prompt(z, ka, va, lams, subln_gain, batch, seq, t):
    nt = seq // t
    pairs = [(i, j) for i in range(nt) for j in range(i + 1)]
    ii = jnp.asarray([p[0] for p in pairs], jnp.int32)
    jj = jnp.asarray([p[1] for p in pairs], jnp.int32)
    vec = pl.BlockSpec((1, DKA), lambda b, h, s, ii, jj: (0, 0))
    grid_spec = pltpu.PrefetchScalarGridSpec(
        num_scalar_prefetch=2,
        grid=(batch, HA, len(pairs)),
        in_specs=[
            pl.BlockSpec((t, DVA), lambda b, h, s, ii, jj: (b * nt + ii[s], h)),
            pl.BlockSpec((t, DVA), lambda b, h, s, ii, jj: (b * nt + jj[s], h)),
            pl.BlockSpec((t, DVA), lambda b, h, s, ii, jj: (b * nt + jj[s], h)),
            vec, vec, vec, vec,
            pl.BlockSpec((1, DVA), lambda b, h, s, ii, jj: (0, 0)),
        ],
        out_specs=pl.BlockSpec((t, DVA), lambda b, h, s, ii, jj: (b * nt + ii[s], h)),
        scratch_shapes=[pltpu.VMEM((t, 1), F32), pltpu.VMEM((t, 1), F32), pltpu.VMEM((t, DVA), F32),
                        pltpu.VMEM((t, 1), F32), pltpu.VMEM((t, 1), F32), pltpu.VMEM((t, DVA), F32)],
    )
    return pl.pallas_call(
        functools.partial(_dattn_prompt_kernel, t=t),
        grid_spec=grid_spec,
        out_shape=jax.ShapeDtypeStruct((batch * seq, HA * DVA), F32),
        compiler_params=_cparams(("arbitrary", "arbitrary", "arbitrary")),
        name="dattn_prompt",
    )(ii, jj, z, ka, va, *[l.reshape(1, DKA) for l in lams], subln_gain.reshape(1, DVA))


def _rows_by_head(row, nrow):
    head = lax.broadcasted_iota(jnp.int32, (nrow, 1), 0) * HA // nrow
    out = jnp.zeros((nrow, DVA), F32)
    for h in range(HA):
        out = jnp.where(head == h, row[:, h * DVA:(h + 1) * DVA], out)
    return out


def _dattn_sample_kernel(pt_ref, q_ref, kn_ref, vn_ref, kp_ref, vp_ref, lq1, lk1, lq2, lk2, gain_ref,
                         o_ref, qm_ref, m_ref, l_ref, a_ref, *, n_pages, page):
    p = pl.program_id(1)
    nrow = 2 * HA
    rid = lax.broadcasted_iota(jnp.int32, (nrow, 1), 0)
    slope = _alibi_slope(rid // 2)
    past = n_pages * page

    @pl.when(p == 0)
    def _():
        half = lax.broadcasted_iota(jnp.int32, (nrow, DVA), 1) // DKA
        qm_ref[...] = jnp.where(half == rid % 2, _rows_by_head(q_ref[0], nrow), 0.0)
        m_ref[...] = jnp.full(m_ref.shape, -jnp.inf, F32)
        l_ref[...] = jnp.zeros(l_ref.shape, F32)
        a_ref[...] = jnp.zeros(a_ref.shape, F32)

    qm = qm_ref[...]
    n = lax.broadcasted_iota(jnp.int32, (1, page * HA), 1)
    rel = (past - (p * page + n // HA)).astype(F32)
    kf = kp_ref[...].reshape(page * HA, DVA).astype(BF16)
    vf = vp_ref[...].reshape(page * HA, DVA).astype(BF16)
    sc = _dot_nt(qm.astype(BF16), kf) * (DKA ** -0.5) - slope * rel
    sc = jnp.where(n % HA == rid // 2, sc, -1e30)
    m_new = jnp.maximum(m_ref[...], jnp.max(sc, axis=-1, keepdims=True))
    alpha = jnp.exp(m_ref[...] - m_new)
    pr = jnp.exp(sc - m_new)
    l_ref[...] = alpha * l_ref[...] + jnp.sum(pr, axis=-1, keepdims=True)
    a_ref[...] = alpha * a_ref[...] + _dot(pr.astype(BF16), vf)
    m_ref[...] = m_new

    @pl.when(p == n_pages - 1)
    def _():
        sc_n = jnp.sum(qm * _rows_by_head(kn_ref[0], nrow), axis=-1, keepdims=True) * (DKA ** -0.5)
        m_f = jnp.maximum(m_ref[...], sc_n)
        al = jnp.exp(m_ref[...] - m_f)
        pn = jnp.exp(sc_n - m_f)
        l_f = al * l_ref[...] + pn
        a_f = (al * a_ref[...] + pn * _rows_by_head(vn_ref[0], nrow)) / l_f
        lam = _lambda_full(lq1[...], lk1[...], lq2[...], lk2[...])
        outs = []
        for h in range(HA):
            o = a_f[2 * h:2 * h + 1] - lam * a_f[2 * h + 1:2 * h + 2]
            outs.append(_subln(o, gain_ref[...]))
        o_ref[0] = jnp.concatenate(outs, axis=-1)


def _dattn_sample(z, ka, va, cache_k, cache_v, page_table, lams, subln_gain):
    nb, n_pages = page_table.shape
    page = cache_k.shape[1]
    w = HA * DVA
    row3 = lambda a: a.reshape(nb, 1, a.shape[-1])
    vec = pl.BlockSpec((1, DKA), lambda b, p, pt: (0, 0))
    page_spec = pl.BlockSpec((None, page, HA, DVA), lambda b, p, pt: (pt[b * n_pages + p], 0, 0, 0))
    grid_spec = pltpu.PrefetchScalarGridSpec(
        num_scalar_prefetch=1,
        grid=(nb, n_pages),
        in_specs=[
            pl.BlockSpec((1, 1, w), lambda b, p, pt: (b, 0, 0)),
            pl.BlockSpec((1, 1, w), lambda b, p, pt: (b, 0, 0)),
            pl.BlockSpec((1, 1, w), lambda b, p, pt: (b, 0, 0)),
            page_spec, page_spec,
            vec, vec, vec, vec,
            pl.BlockSpec((1, DVA), lambda b, p, pt: (0, 0)),
        ],
        out_specs=pl.BlockSpec((1, 1, w), lambda b, p, pt: (b, 0, 0)),
        scratch_shapes=[pltpu.VMEM((2 * HA, DVA), F32), pltpu.VMEM((2 * HA, 1), F32),
                        pltpu.VMEM((2 * HA, 1), F32), pltpu.VMEM((2 * HA, DVA), F32)],
    )
    out = pl.pallas_call(
        functools.partial(_dattn_sample_kernel, n_pages=n_pages, page=page),
        grid_spec=grid_spec,
        out_shape=jax.ShapeDtypeStruct((nb, 1, w), F32),
        compiler_params=_cparams(("arbitrary", "arbitrary")),
        name="dattn_sample",
    )(page_table.reshape(-1), row3(z), row3(ka), row3(va), cache_k, cache_v,
      *[l.reshape(1, DKA) for l in lams], subln_gain.reshape(1, DVA))
    return out.reshape(nb, w)


def _ret_log_gamma(h):
    return math.log(1.0 - 2.0 ** (-5.0 - h))


def _ret_theta(width):
    theta = 1.0 / (ROPE_BASE ** jnp.linspace(0.0, 1.0, DKB // 2, dtype=F32))
    return jnp.tile(jnp.repeat(theta, 2), width // DKB).reshape(1, width)


def _rotate_pairs(x, cos, sin):
    n = x.shape[-1]
    lane = lax.broadcasted_iota(jnp.int32, x.shape, x.ndim - 1)
    nxt = pltpu.roll(x, n - 1, axis=x.ndim - 1)
    prv = pltpu.roll(x, 1, axis=x.ndim - 1)
    return x * cos + jnp.where(lane % 2 == 0, -nxt, prv) * sin


def _ret_prompt_kernel(q_ref, k_ref, v_ref, g_ref, th_ref, o_ref, s_ref, dec_ref, *, c):
    ci = pl.program_id(1)
    ri = lax.broadcasted_iota(jnp.int32, (c, c), 0)
    cj = lax.broadcasted_iota(jnp.int32, (c, c), 1)

    @pl.when(jnp.logical_and(pl.program_id(0) == 0, ci == 0))
    def _():
        diff = (ri - cj).astype(F32)
        for h in range(HB):
            dec_ref[h] = jnp.where(diff >= 0, jnp.exp(_ret_log_gamma(h) * jnp.maximum(diff, 0.0)), 0.0)

    @pl.when(ci == 0)
    def _():
        s_ref[...] = jnp.zeros(s_ref.shape, F32)

    idx = lax.broadcasted_iota(jnp.int32, (c, 1), 0)
    pos = (ci * c + idx).astype(F32)
    ang = pos * th_ref[...]
    cos = jnp.concatenate([jnp.cos(ang)] * (HB * DKB // 128), axis=-1)
    sin = jnp.concatenate([jnp.sin(ang)] * (HB * DKB // 128), axis=-1)
    qr = _rotate_pairs(q_ref[...], cos, sin)
    kr = _rotate_pairs(k_ref[...], cos, sin) * (DKB ** -0.5)
    idx_f = idx.astype(F32)
    outs = []
    for h in range(HB):
        lg = _ret_log_gamma(h)
        qh = qr[:, h * DKB:(h + 1) * DKB].astype(BF16)
        kh = kr[:, h * DKB:(h + 1) * DKB]
        vh = v_ref[:, h * DVB:(h + 1) * DVB].astype(BF16)
        s_old = s_ref[0, h]
        scores = _dot_nt(qh, kh.astype(BF16)) * dec_ref[h]
        o = _dot(scores.astype(BF16), vh)
        o = o + _dot(qh, s_old.astype(BF16)) * jnp.exp(lg * (idx_f + 1.0))
        kd = (kh * jnp.exp(lg * (c - 1.0 - idx_f))).astype(BF16)
        s_ref[0, h] = math.exp(lg * c) * s_old + _dot_tn(kd, vh)
        gate = g_ref[:, h * DVB:(h + 1) * DVB]
        outs.append(_rms(o) * (gate * _sigmoid(gate)))
    o_ref[...] = jnp.concatenate(outs, axis=-1)


def _ret_prompt(z, batch, seq, c):
    nc = seq // c
    wq = HB * DKB
    wv = HB * DVB
    return pl.pallas_call(
        functools.partial(_ret_prompt_kernel, c=c),
        grid=(batch, nc),
        in_specs=[
            pl.BlockSpec((c, wq), lambda b, i: (b * nc + i, 2)),
            pl.BlockSpec((c, wq), lambda b, i: (b * nc + i, 3)),
            pl.BlockSpec((c, wv), lambda b, i: (b * nc + i, 2)),
            pl.BlockSpec((c, wv), lambda b, i: (b * nc + i, 3)),
            pl.BlockSpec((1, 128), lambda b, i: (0, 0)),
        ],
        out_specs=[
            pl.BlockSpec((c, wv), lambda b, i: (b * nc + i, 0)),
            pl.BlockSpec((1, HB, DKB, DVB), lambda b, i: (b, 0, 0, 0)),
        ],
        out_shape=[
            jax.ShapeDtypeStruct((batch * seq, wv), F32),
            jax.ShapeDtypeStruct((batch, HB, DKB, DVB), F32),
        ],
        scratch_shapes=[pltpu.VMEM((HB, c, c), F32)],
        compiler_params=_cparams(("arbitrary", "arbitrary")),
        name="ret_prompt",
    )(z, z, z, z, _ret_theta(128))


def _col_of_row(row, eye):
    return jnp.sum(eye * row, axis=-1, keepdims=True)


def _ret_sample_kernel(q_ref, k_ref, v_ref, g_ref, th_ref, s_ref, o_ref, so_ref, *, pos):
    ang = float(pos) * th_ref[...]
    cos = jnp.cos(ang)
    sin = jnp.sin(ang)
    qr = _rotate_pairs(q_ref[0], cos, sin)
    kr = _rotate_pairs(k_ref[0], cos, sin) * (DKB ** -0.5)
    eye = (lax.broadcasted_iota(jnp.int32, (DKB, DKB), 0)
           == lax.broadcasted_iota(jnp.int32, (DKB, DKB), 1)).astype(F32)
    outs = []
    for h in range(HB):
        gamma = math.exp(_ret_log_gamma(h))
        qh = qr[:, h * DKB:(h + 1) * DKB]
        kh = kr[:, h * DKB:(h + 1) * DKB]
        vh = v_ref[0, :, h * DVB:(h + 1) * DVB]
        s_old = s_ref[0, h]
        qs = jnp.sum(_col_of_row(qh, eye) * s_old, axis=0, keepdims=True)
        qk = jnp.sum(qh * kh, axis=-1, keepdims=True)
        o = qk * vh + qs * gamma
        so_ref[0, h] = gamma * s_old + _col_of_row(kh, eye) * vh
        gate = g_ref[0, :, h * DVB:(h + 1) * DVB]
        outs.append(_rms(o) * (gate * _sigmoid(gate)))
    o_ref[0] = jnp.concatenate(outs, axis=-1)


def _ret_sample(z, state, pos):
    nb = z.shape[0]
    wq = HB * DKB
    wv = HB * DVB
    z3 = z.reshape(nb, 1, z.shape[-1])
    out, s_new = pl.pallas_call(
        functools.partial(_ret_sample_kernel, pos=pos),
        grid=(nb,),
        in_specs=[
            pl.BlockSpec((1, 1, wq), lambda b: (b, 0, 2)),
            pl.BlockSpec((1, 1, wq), lambda b: (b, 0, 3)),
            pl.BlockSpec((1, 1, wv), lambda b: (b, 0, 2)),
            pl.BlockSpec((1, 1, wv), lambda b: (b, 0, 3)),
            pl.BlockSpec((1, wq), lambda b: (0, 0)),
            pl.BlockSpec((1, HB, DKB, DVB), lambda b: (b, 0, 0, 0)),
        ],
        out_specs=[
            pl.BlockSpec((1, 1, wv), lambda b: (b, 0, 0)),
            pl.BlockSpec((1, HB, DKB, DVB), lambda b: (b, 0, 0, 0)),
        ],
        out_shape=[
            jax.ShapeDtypeStruct((nb, 1, wv), F32),
            jax.ShapeDtypeStruct(state.shape, F32),
        ],
        compiler_params=_cparams(("arbitrary",)),
        name="ret_sample",
    )(z3, z3, z3, z3, _ret_theta(wq), state)
    return out.reshape(nb, wv), s_new


def _mm_resid_kernel(*refs, n_in):
    a_refs = refs[:n_in]
    w_refs = refs[n_in:2 * n_in]
    x_ref = refs[2 * n_in]
    o_ref = refs[2 * n_in + 1]
    acc = x_ref[...]
    for a_ref, w_ref in zip(a_refs, w_refs):
        acc = acc + _dot(a_ref[...].astype(BF16), w_ref[...])
    o_ref[...] = acc


def _mm_resid(a_list, w_bf16, resid, tm, tn=1024):
    m, n = resid.shape
    in_specs = []
    for a in a_list:
        in_specs.append(pl.BlockSpec((tm, a.shape[1]), lambda i, j: (i, 0)))
    row = 0
    for a in a_list:
        kb = a.shape[1]
        in_specs.append(pl.BlockSpec((kb, tn), lambda i, j, r=row // kb: (r, j)))
        row += kb
    in_specs.append(pl.BlockSpec((tm, tn), lambda i, j: (i, j)))
    return pl.pallas_call(
        functools.partial(_mm_resid_kernel, n_in=len(a_list)),
        grid=(m // tm, n // tn),
        in_specs=in_specs,
        out_specs=pl.BlockSpec((tm, tn), lambda i, j: (i, j)),
        out_shape=jax.ShapeDtypeStruct((m, n), F32),
        compiler_params=_cparams(("arbitrary", "arbitrary")),
        name="mm_resid",
    )(*a_list, *([w_bf16] * len(a_list)), resid)


def _ffn_kernel(x_ref, g_ref, wg_ref, wu_ref, wd_ref, fg_ref, o_ref, xn_ref, acc_ref, *, final_norm):
    j = pl.program_id(1)

    @pl.when(j == 0)
    def _():
        xn_ref[...] = (_rms(x_ref[...]) * g_ref[...]).astype(BF16)
        acc_ref[...] = jnp.zeros(acc_ref.shape, F32)

    xn = xn_ref[...]
    gate = _dot(xn, wg_ref[...])
    up = _dot(xn, wu_ref[...])
    hid = (gate * _sigmoid(gate) * up).astype(BF16)
    acc_ref[...] += _dot(hid, wd_ref[...])

    @pl.when(j == pl.num_programs(1) - 1)
    def _():
        y = x_ref[...] + acc_ref[...]
        if final_norm:
            y = _rms(y) * fg_ref[...]
        o_ref[...] = y


def _ffn(x, gain, wg, wu, wd, final_gain, final_norm, tm, tf=512):
    m = x.shape[0]
    dff = wg.shape[1]
    return pl.pallas_call(
        functools.partial(_ffn_kernel, final_norm=final_norm),
        grid=(m // tm, dff // tf),
        in_specs=[
            pl.BlockSpec((tm, D_MODEL), lambda i, j: (i, 0)),
            pl.BlockSpec((1, D_MODEL), lambda i, j: (0, 0)),
            pl.BlockSpec((D_MODEL, tf), lambda i, j: (0, j)),
            pl.BlockSpec((D_MODEL, tf), lambda i, j: (0, j)),
            pl.BlockSpec((tf, D_MODEL), lambda i, j: (j, 0)),
            pl.BlockSpec((1, D_MODEL), lambda i, j: (0, 0)),
        ],
        out_specs=pl.BlockSpec((tm, D_MODEL), lambda i, j: (i, 0)),
        out_shape=jax.ShapeDtypeStruct((m, D_MODEL), F32),
        scratch_shapes=[pltpu.VMEM((tm, D_MODEL), BF16), pltpu.VMEM((tm, D_MODEL), F32)],
        compiler_params=_cparams(("arbitrary", "arbitrary")),
        name="ffn",
    )(x, gain.reshape(1, D_MODEL), wg, wu, wd, final_gain.reshape(1, D_MODEL))


def _norm_and_shift(x_ref, g_ref, prev_ref, carry_ref, i, tiles_per_seq):
    h = _rms(x_ref[...]) * g_ref[...]
    if prev_ref is None:
        tm = h.shape[0]
        first = (i % tiles_per_seq) == 0
        prev_row = jnp.where(first, 0.0, carry_ref[0:1, :])
        rows = lax.broadcasted_iota(jnp.int32, (tm, 1), 0)
        shifted = jnp.where(rows == 0, prev_row, pltpu.roll(h, 1, axis=0))
        carry_ref[0:1, :] = h[tm - 1:tm, :]
    else:
        shifted = prev_ref[...]
    return h, shifted - h


def _rwkv_rkv_kernel(*refs, per_row_prev, tiles_per_seq):
    if per_row_prev:
        x_ref, g_ref, prev_ref, mu_ref, w_ref, o_ref, hl_ref, h_ref, xx_ref, carry_ref = refs
    else:
        x_ref, g_ref, mu_ref, w_ref, o_ref, hl_ref, h_ref, xx_ref, carry_ref = refs
        prev_ref = None
    i = pl.program_id(0)
    j = pl.program_id(1)

    @pl.when(j == 0)
    def _():
        h, xx = _norm_and_shift(x_ref, g_ref, prev_ref, carry_ref, i, tiles_per_seq)
        h_ref[...] = h
        xx_ref[...] = xx
        if per_row_prev:
            hl_ref[...] = h
        else:
            hl_ref[0] = h[h.shape[0] - 1:, :]

    xm = (h_ref[...] + xx_ref[...] * mu_ref[0]).astype(BF16)
    o_ref[...] = _dot(xm, w_ref[...])


def _rwkv_rkv(x, gain, prev, mu_rkv, w_rkv, tm, tiles_per_seq):
    m = x.shape[0]
    tn = 1024
    per_seq = D_MODEL // tn
    per_row_prev = prev is not None
    in_specs = [pl.BlockSpec((tm, D_MODEL), lambda i, j: (i, 0)),
                pl.BlockSpec((1, D_MODEL), lambda i, j: (0, 0))]
    args = [x, gain.reshape(1, D_MODEL)]
    if per_row_prev:
        in_specs.append(pl.BlockSpec((tm, D_MODEL), lambda i, j: (i, 0)))
        args.append(prev)
        hl_spec = pl.BlockSpec((tm, D_MODEL), lambda i, j: (i, 0))
        hl_shape = jax.ShapeDtypeStruct((m, D_MODEL), F32)
    else:
        nseq = m // (tm * tiles_per_seq)
        hl_spec = pl.BlockSpec((1, 1, D_MODEL), lambda i, j: (i // tiles_per_seq, 0, 0))
        hl_shape = jax.ShapeDtypeStruct((nseq, 1, D_MODEL), F32)
    in_specs += [pl.BlockSpec((1, 1, D_MODEL), lambda i, j: (j // per_seq, 0, 0)),
                 pl.BlockSpec((D_MODEL, tn), lambda i, j: (0, j))]
    args += [mu_rkv, w_rkv]
    return pl.pallas_call(
        functools.partial(_rwkv_rkv_kernel, per_row_prev=per_row_prev, tiles_per_seq=tiles_per_seq),
        grid=(m // tm, 3 * per_seq),
        in_specs=in_specs,
        out_specs=[pl.BlockSpec((tm, tn), lambda i, j: (i, j)), hl_spec],
        out_shape=[jax.ShapeDtypeStruct((m, 3 * D_MODEL), F32), hl_shape],
        scratch_shapes=[pltpu.VMEM((tm, D_MODEL), F32), pltpu.VMEM((tm, D_MODEL), F32),
                        pltpu.VMEM((8, D_MODEL), F32)],
        compiler_params=_cparams(("arbitrary", "arbitrary")),
        name="rwkv_rkv",
    )(*args)


def _rwkv_lora_kernel(*refs, per_row_prev, tiles_per_seq):
    if per_row_prev:
        (x_ref, g_ref, prev_ref, mu_ref, w0_ref, w1_ref, w2_ref, a0_ref, a1_ref, a2_ref, g1_ref, g2_ref,
         lw_ref, a_ref, go_ref, carry_ref) = refs
    else:
        (x_ref, g_ref, mu_ref, w0_ref, w1_ref, w2_ref, a0_ref, a1_ref, a2_ref, g1_ref, g2_ref,
         lw_ref, a_ref, go_ref, carry_ref) = refs
        prev_ref = None
    h, xx = _norm_and_shift(x_ref, g_ref, prev_ref, carry_ref, pl.program_id(0), tiles_per_seq)
    xw = (h + xx * mu_ref[0]).astype(BF16)
    xa = (h + xx * mu_ref[1]).astype(BF16)
    xg = (h + xx * mu_ref[2]).astype(BF16)
    zw = w0_ref[...] + _dot(jnp.tanh(_dot(xw, w1_ref[...])).astype(BF16), w2_ref[...])
    softplus = jnp.maximum(-zw, 0.0) + jnp.log(1.0 + jnp.exp(-jnp.abs(zw)))
    lw_ref[...] = -jnp.exp(-softplus - 0.5)
    a_ref[...] = _sigmoid(a0_ref[...] + _dot(_dot(xa, a1_ref[...]).astype(BF16), a2_ref[...]))
    go_ref[...] = _dot(_sigmoid(_dot(xg, g1_ref[...])).astype(BF16), g2_ref[...])


def _rwkv_lora(x, gain, prev, mu_wag, w0, w1, w2, a0, a1, a2, g1, g2, tm, tiles_per_seq):
    m = x.shape[0]
    per_row_prev = prev is not None
    full = lambda a: pl.BlockSpec(a.shape, lambda i: (0,) * a.ndim)
    row = pl.BlockSpec((tm, D_MODEL), lambda i: (i, 0))
    args = [x, gain.reshape(1, D_MODEL)]
    in_specs = [row, full(args[1])]
    if per_row_prev:
        in_specs.append(row)
        args.append(prev)
    rest = [mu_wag, w0.reshape(1, D_MODEL), w1, w2, a0.reshape(1, D_MODEL), a1, a2, g1, g2]
    in_specs += [full(a) for a in rest]
    args += rest
    return pl.pallas_call(
        functools.partial(_rwkv_lora_kernel, per_row_prev=per_row_prev, tiles_per_seq=tiles_per_seq),
        grid=(m // tm,),
        in_specs=in_specs,
        out_specs=[row, row, row],
        out_shape=[jax.ShapeDtypeStruct((m, D_MODEL), F32)] * 3,
        scratch_shapes=[pltpu.VMEM((8, D_MODEL), F32)],
        compiler_params=_cparams(("arbitrary",)),
        name="rwkv_lora",
    )(*args)


def _wkv_keys(k, a, kk_gain, ka_gain, axis=-1):
    kk = k * kk_gain
    kk = kk / jnp.maximum(jnp.sqrt(jnp.sum(kk * kk, axis=axis, keepdims=True)), 1e-12)
    return kk, kk * a, k * (1.0 + (a - 1.0) * ka_gain)


def _wkv_finish(o, r, k2, v, g, rk, ln_g, ln_b, axis=-1):
    mean = jnp.mean(o, axis=axis, keepdims=True)
    var = jnp.mean(jnp.square(o - mean), axis=axis, keepdims=True)
    on = (o - mean) * lax.rsqrt(var + GN_EPS) * ln_g + ln_b
    bonus = jnp.sum(r * k2 * rk, axis=axis, keepdims=True) * v
    return (on + bonus) * g


def _bmm(a, b):
    return jnp.einsum("cij,cjk->cik", a, b, preferred_element_type=F32)


def _bmm_nt(a, b):
    return jnp.einsum("cik,cjk->cij", a, b, preferred_element_type=F32)


def _wkv_prompt_kernel(r_ref, k_ref, v_ref, lw_ref, a_ref, g_ref, kkg_ref, kag_ref, rk_ref, lng_ref, lnb_ref,
                       y_ref, so_ref, st_ref, *, tc):
    c = WKV_CHUNK
    nch = tc // c
    ti = pl.program_id(2)

    @pl.when(ti == 0)
    def _():
        st_ref[...] = jnp.zeros(st_ref.shape, F32)

    ri = lax.broadcasted_iota(jnp.int32, (c, c), 0)
    cj = lax.broadcasted_iota(jnp.int32, (c, c), 1)
    tri = (ri >= cj).astype(BF16)
    lower = (ri >= cj)[None]
    strict = (ri > cj)[None]
    eye = (ri == cj).astype(F32)
    outs = []
    for hh in range(2):
        sl = slice(hh * NC, (hh + 1) * NC)
        r = r_ref[:, sl]
        k = k_ref[:, sl]
        v = v_ref[:, sl]
        a = a_ref[:, sl]
        kk, bv, k2 = _wkv_keys(k, a, kkg_ref[:, sl], kag_ref[:, sl])
        lw3 = lw_ref[:, sl].reshape(nch, c, NC)
        p0 = lw3.astype(BF16)
        r1 = lw3 - p0.astype(F32)
        p1 = r1.astype(BF16)
        p2 = (r1 - p1.astype(F32)).astype(BF16)
        trib = jnp.broadcast_to(tri[None], (nch, c, c))
        cum = _bmm(trib, p0) + (_bmm(trib, p1) + _bmm(trib, p2))
        cum_last = cum[:, c - 1:c, :]
        w_in = jnp.exp(cum)
        w_ex = jnp.exp(cum - lw3)
        w_inv = jnp.exp(-cum)
        w_rest = jnp.exp(cum_last - cum)
        to3 = lambda x: x.reshape(nch, c, NC)
        am = -to3(kk) * w_ex
        rm = to3(r) * w_in
        bm = to3(bv) * w_inv
        km = to3(k2) * w_inv
        bt = to3(bv) * w_rest
        kt = to3(k2) * w_rest
        v3 = to3(v)
        mm_nt = lambda x, y: _dot_f32(x, y, _bmm_nt)
        mm = lambda x, y: _dot_f32(x, y, _bmm)
        l_ab = jnp.where(strict, mm_nt(am, bm), 0.0)
        l_ak = jnp.where(strict, mm_nt(am, km), 0.0)
        m_rb = jnp.where(lower, mm_nt(rm, bm), 0.0)
        m_rk = jnp.where(lower, mm_nt(rm, km), 0.0)
        x = jnp.concatenate([am, mm(l_ak, v3)], axis=-1)
        lp = l_ab
        steps = int(math.log2(c))
        for s in range(steps):
            x = x + mm(lp, x)
            if s + 1 < steps:
                lp = mm(lp, lp)
        gh = mm(m_rb, x) + jnp.concatenate([rm, mm(m_rk, v3)], axis=-1)
        st = st_ref[hh]
        o_chunks = []
        for ch in range(nch):
            pq = x[ch]
            ef = _dot_f32(bt[ch], pq, _dot_tn)
            e = ef[:, :NC] + eye * jnp.exp(cum_last[ch])
            f = ef[:, NC:] + _dot_f32(kt[ch], v3[ch], _dot_tn)
            o_chunks.append(_dot_f32(gh[ch][:, :NC], st) + gh[ch][:, NC:])
            st = _dot_f32(e, st) + f
        st_ref[hh] = st
        o = jnp.concatenate(o_chunks, axis=0)
        outs.append(_wkv_finish(o, r, k2, v, g_ref[:, sl], rk_ref[:, sl], lng_ref[:, sl], lnb_ref[:, sl]))

        @pl.when(ti == pl.num_programs(2) - 1)
        def _():
            so_ref[0, hh] = st.T
    y_ref[...] = jnp.concatenate(outs, axis=-1)


def _wkv_prompt(rkv, lw, a, g, kk_gain, ka_gain, rk, ln_g, ln_b, batch, seq, tc):
    nt = seq // tc
    nhp = HC // 2
    tok = lambda off: pl.BlockSpec((tc, 2 * NC), lambda b, hp, t: (b * nt + t, off + hp))
    par = pl.BlockSpec((1, 2 * NC), lambda b, hp, t: (0, hp))
    return pl.pallas_call(
        functools.partial(_wkv_prompt_kernel, tc=tc),
        grid=(batch, nhp, nt),
        in_specs=[tok(0), tok(nhp), tok(2 * nhp), tok(0), tok(0), tok(0), par, par, par, par, par],
        out_specs=[
            pl.BlockSpec((tc, 2 * NC), lambda b, hp, t: (b * nt + t, hp)),
            pl.BlockSpec((1, 2, NC, NC), lambda b, hp, t: (b, hp, 0, 0)),
        ],
        out_shape=[
            jax.ShapeDtypeStruct((batch * seq, D_MODEL), F32),
            jax.ShapeDtypeStruct((batch, HC, NC, NC), F32),
        ],
        scratch_shapes=[pltpu.VMEM((2, NC, NC), F32)],
        compiler_params=_cparams(("arbitrary", "arbitrary", "arbitrary")),
        name="wkv_prompt",
    )(rkv, rkv, rkv, lw, a, g, kk_gain, ka_gain, rk, ln_g, ln_b)


def _wkv_sample_kernel(r_ref, k_ref, v_ref, lw_ref, a_ref, g_ref, kkg_ref, kag_ref, rk_ref, lng_ref, lnb_ref,
                       s_ref, y_ref, so_ref, o_ref):
    ys = []
    for hh in range(2):
        sl = slice(hh * NC, (hh + 1) * NC)
        t = lambda ref: ref[...].T[sl, :]
        r, k, v, a, g = t(r_ref), t(k_ref), t(v_ref), t(a_ref), t(g_ref)
        w = jnp.exp(t(lw_ref))
        kk, bv, k2 = _wkv_keys(k, a, kkg_ref[sl, :], kag_ref[sl, :], axis=0)
        for i in range(NC):
            s_old = s_ref[hh, i]
            sk = jnp.sum(s_old * kk, axis=0, keepdims=True)
            s_new = s_old * w - sk * bv + v[i:i + 1, :] * k2
            so_ref[hh, i] = s_new
            o_ref[i:i + 1, :] = jnp.sum(s_new * r, axis=0, keepdims=True)
        ys.append(_wkv_finish(o_ref[...], r, k2, v, g, rk_ref[sl, :], lng_ref[sl, :], lnb_ref[sl, :], axis=0))
    y_ref[...] = jnp.concatenate(ys, axis=0).T


def _wkv_sample(rkv, lw, a, g, kk_gain, ka_gain, rk, ln_g, ln_b, state_t):
    nb = lw.shape[0]
    nhp = HC // 2
    tok = lambda off: pl.BlockSpec((nb, 2 * NC), lambda hp: (0, off + hp))
    par = pl.BlockSpec((2 * NC, 1), lambda hp: (hp, 0))
    col = lambda p: p.reshape(D_MODEL, 1)
    st_spec = pl.BlockSpec((2, NC, NC, nb), lambda hp: (hp, 0, 0, 0))
    return pl.pallas_call(
        _wkv_sample_kernel,
        grid=(nhp,),
        in_specs=[tok(0), tok(nhp), tok(2 * nhp), tok(0), tok(0), tok(0), par, par, par, par, par, st_spec],
        out_specs=[pl.BlockSpec((nb, 2 * NC), lambda hp: (0, hp)), st_spec],
        out_shape=[
            jax.ShapeDtypeStruct((nb, D_MODEL), F32),
            jax.ShapeDtypeStruct(state_t.shape, F32),
        ],
        scratch_shapes=[pltpu.VMEM((NC, nb), F32)],
        compiler_params=_cparams(("arbitrary",)),
        name="wkv_sample",
    )(rkv, rkv, rkv, lw, a, g, col(kk_gain), col(ka_gain), col(rk), col(ln_g), col(ln_b), state_t)


def _row_tile(m, want):
    return min(m, want)


def _trunk(x, batch, seq, past, W):
    m = x.shape[0]
    prompt = past is None
    tm = _row_tile(m, 512)
    lams = (W["lambda_q1"], W["lambda_k1"], W["lambda_q2"], W["lambda_k2"])

    ka, va, z = _inproj0(x, W["norm0_mix"], W["w_in0"], tm)
    if prompt:
        o_a = _dattn_prompt(z, ka, va, lams, W["subln_gain"], batch, seq, min(seq, 512))
        o_b, ret_new = _ret_prompt(z, batch, seq, min(seq, 256))
    else:
        o_a = _dattn_sample(z, ka, va, past["cache_k"], past["cache_v"], past["page_table"], lams,
                            W["subln_gain"])
        o_b, ret_new = _ret_sample(z, past["state_ret"], past["pos"])
    x = _mm_resid([o_a, o_b], W["w_out0"], x, tm)
    x = _ffn(x, W["norm0_ffn"], W["w_gate0"], W["w_up0"], W["w_down0"], W["norm_final"], False, tm)

    tiles_per_seq = max(seq // tm, 1)
    prev = None if prompt else past["state_shift"]
    rkv, hlast = _rwkv_rkv(x, W["norm1_mix"], prev, W["mu_rkv"], W["w_rkv"], tm, tiles_per_seq)
    tl = _row_tile(m, 256)
    lw, a, g = _rwkv_lora(x, W["norm1_mix"], prev, W["mu_wag"], W["decay_w0"], W["decay_w1"], W["decay_w2"],
                          W["aaa_a0"], W["aaa_a1"], W["aaa_a2"], W["gate_g1"], W["gate_g2"],
                          tl, max(seq // tl, 1))
    head_params = (W["k_k"], W["k_a"], W["r_k"], W["lnx_gain"], W["lnx_bias"])
    if prompt:
        y, wkv_new = _wkv_prompt(rkv, lw, a, g, *head_params, batch, seq, min(seq, 256))
        shift_new = hlast.reshape(batch, D_MODEL)
    else:
        state_t = jnp.transpose(past["state_wkv"], (1, 2, 3, 0))
        y, wkv_t = _wkv_sample(rkv, lw, a, g, *head_params, state_t)
        wkv_new = jnp.transpose(wkv_t, (3, 0, 1, 2))
        shift_new = hlast
    x = _mm_resid([y], W["w_out1"], x, tm)
    x = _ffn(x, W["norm1_ffn"], W["w_gate1"], W["w_up1"], W["w_down1"], W["norm_final"], True, tm)
    return x, ka, va, ret_new, wkv_new, shift_new


def kernel(x_prompt, x_sample, cache_k, cache_v, page_table, state_ret, state_wkv, state_shift, norm0_mix, w_in0, lambda_q1, lambda_k1, lambda_q2, lambda_k2, subln_gain, w_out0, norm0_ffn, w_gate0, w_up0, w_down0, norm1_mix, mu1, w_r1, w_k1, w_v1, decay_w0, decay_w1, decay_w2, aaa_a0, aaa_a1, aaa_a2, gate_g1, gate_g2, k_k, k_a, r_k, lnx_gain, lnx_bias, w_out1, norm1_ffn, w_gate1, w_up1, w_down1, norm_final):
    bf = lambda w: w.astype(BF16)
    row = lambda p: p.reshape(1, D_MODEL)
    W = dict(
        norm0_mix=norm0_mix, w_in0=bf(w_in0), lambda_q1=lambda_q1, lambda_k1=lambda_k1, lambda_q2=lambda_q2,
        lambda_k2=lambda_k2, subln_gain=subln_gain, w_out0=bf(w_out0), norm0_ffn=norm0_ffn,
        w_gate0=bf(w_gate0), w_up0=bf(w_up0), w_down0=bf(w_down0),
        norm1_mix=norm1_mix,
        mu_rkv=jnp.stack([mu1[0], mu1[2], mu1[3]]).reshape(3, 1, D_MODEL),
        mu_wag=jnp.stack([mu1[1], mu1[4], mu1[5]]).reshape(3, 1, D_MODEL),
        w_rkv=jnp.concatenate([bf(w_r1), bf(w_k1), bf(w_v1)], axis=1),
        decay_w0=decay_w0, decay_w1=bf(decay_w1), decay_w2=bf(decay_w2),
        aaa_a0=aaa_a0, aaa_a1=bf(aaa_a1), aaa_a2=bf(aaa_a2), gate_g1=bf(gate_g1), gate_g2=bf(gate_g2),
        k_k=row(k_k), k_a=row(k_a), r_k=row(r_k), lnx_gain=row(lnx_gain), lnx_bias=row(lnx_bias),
        w_out1=bf(w_out1), norm1_ffn=norm1_ffn, w_gate1=bf(w_gate1), w_up1=bf(w_up1), w_down1=bf(w_down1),
        norm_final=norm_final,
    )
    bsz, seq = x_prompt.shape[:2]
    y_p, k_p, v_p, ret_p, wkv_p, shift_p = _trunk(x_prompt.reshape(bsz * seq, D_MODEL), bsz, seq, None, W)
    nb, dseq = x_sample.shape[:2]
    past_len = page_table.shape[1] * cache_k.shape[1]
    past = dict(cache_k=cache_k, cache_v=cache_v, page_table=page_table, state_ret=state_ret,
                state_wkv=state_wkv, state_shift=state_shift, pos=past_len)
    y_s, k_s, v_s, ret_s, wkv_s, shift_s = _trunk(x_sample.reshape(nb * dseq, D_MODEL), nb, dseq, past, W)
    return (y_p.reshape(bsz, seq, D_MODEL), y_s.reshape(nb, dseq, D_MODEL),
            k_p.reshape(bsz, seq, HA, 2 * DKA), v_p.reshape(bsz, seq, HA, DVA),
            k_s.reshape(nb, dseq, HA, 2 * DKA), v_s.reshape(nb, dseq, HA, DVA),
            ret_p, ret_s, wkv_p, wkv_s, shift_p, shift_s)
```

```python
import functools
import math

import jax
import jax.numpy as jnp
from jax import lax
from jax.experimental import pallas as pl
from jax.experimental.pallas import tpu as pltpu

F32 = jnp.float32
BF16 = jnp.bfloat16

D_MODEL = 2048
EPS = 1e-6
DKA = 64
DVA = 128
HA = 8
LAMBDA_INIT = 0.8 - 0.6 * math.exp(-0.3 * 0)
DKB = 64
DVB = 128
HB = 8
ROPE_BASE = 10000.0
NC = 64
HC = D_MODEL // NC
GN_EPS = 64e-5
WKV_CHUNK = 64
PAGES_PER_STEP = 8
V7X_VMEM_BYTES = 64 * 1024 * 1024
VMEM_LIMIT = 48 * 1024 * 1024


def _cparams(sem):
    return pltpu.CompilerParams(dimension_semantics=sem, vmem_limit_bytes=VMEM_LIMIT)


def _rms(x, eps=EPS):
    return x * lax.rsqrt(jnp.mean(x * x, axis=-1, keepdims=True) + eps)


def _sigmoid(x):
    return 1.0 / (1.0 + jnp.exp(-x))


def _dot(a, b):
    return jnp.dot(a, b, preferred_element_type=F32)


def _dot_nt(a, b):
    return lax.dot_general(a, b, (((1,), (1,)), ((), ())), preferred_element_type=F32)


def _dot_tn(a, b):
    return lax.dot_general(a, b, (((0,), (0,)), ((), ())), preferred_element_type=F32)


def _split2(x):
    hi = x.astype(BF16)
    lo = (x - hi.astype(F32)).astype(BF16)
    return hi, lo


def _dot_f32(a, b, dot=_dot):
    ah, al = _split2(a)
    bh, bl = _split2(b)
    return dot(ah, bh) + (dot(ah, bl) + dot(al, bh))


def _inproj0_kernel(x_ref, g_ref, w_ref, ka_ref, va_ref, z_ref, xn_ref):
    j = pl.program_id(1)

    @pl.when(j == 0)
    def _():
        xn_ref[...] = (_rms(x_ref[...]) * g_ref[...]).astype(BF16)

    acc = _dot(xn_ref[...], w_ref[...])

    @pl.when(j == 1)
    def _():
        ka_ref[...] = acc

    @pl.when(j == 2)
    def _():
        va_ref[...] = acc

    @pl.when(jnp.logical_and(j != 1, j != 2))
    def _():
        z_ref[...] = acc


def _inproj0(x, gain, w_bf16, tm):
    m = x.shape[0]
    tn = 1024
    nj = w_bf16.shape[1] // tn
    return pl.pallas_call(
        _inproj0_kernel,
        grid=(m // tm, nj),
        in_specs=[
            pl.BlockSpec((tm, D_MODEL), lambda i, j: (i, 0)),
            pl.BlockSpec((1, D_MODEL), lambda i, j: (0, 0)),
            pl.BlockSpec((D_MODEL, tn), lambda i, j: (0, j)),
        ],
        out_specs=[
            pl.BlockSpec((tm, tn), lambda i, j: (i, 0)),
            pl.BlockSpec((tm, tn), lambda i, j: (i, 0)),
            pl.BlockSpec((tm, tn), lambda i, j: (i, jnp.maximum(j - 2, 0))),
        ],
        out_shape=[
            jax.ShapeDtypeStruct((m, tn), F32),
            jax.ShapeDtypeStruct((m, tn), F32),
            jax.ShapeDtypeStruct((m, 4 * tn), F32),
        ],
        scratch_shapes=[pltpu.VMEM((tm, D_MODEL), BF16)],
        compiler_params=_cparams(("arbitrary", "arbitrary")),
        name="inproj0",
    )(x, gain.reshape(1, D_MODEL), w_bf16)


def _lambda_full(lq1, lk1, lq2, lk2):
    s1 = jnp.sum(lq1 * lk1, axis=-1, keepdims=True)
    s2 = jnp.sum(lq2 * lk2, axis=-1, keepdims=True)
    return jnp.exp(s1) - jnp.exp(s2) + LAMBDA_INIT


def _alibi_slope(h_vec):
    slope = jnp.zeros(h_vec.shape, F32)
    for h in range(HA):
        slope = jnp.where(h_vec == h, 2.0 ** (-(h + 1)), slope)
    return slope


def _subln(o, gain):
    return _rms(o) * gain * (1.0 - LAMBDA_INIT)


def _dattn_prompt_kernel(ii_ref, jj_ref, q_ref, k_ref, v_ref, lq1, lk1, lq2, lk2, gain_ref, o_ref,
                         m1, l1, a1, m2, l2, a2, *, t):
    h = pl.program_id(1)
    s = pl.program_id(2)
    i = ii_ref[s]
    j = jj_ref[s]

    @pl.when(j == 0)
    def _():
        for m_ref, l_ref, a_ref in ((m1, l1, a1), (m2, l2, a2)):
            m_ref[...] = jnp.full(m_ref.shape, -jnp.inf, F32)
            l_ref[...] = jnp.zeros(l_ref.shape, F32)
            a_ref[...] = jnp.zeros(a_ref.shape, F32)

    def step(diagonal):
        slope = _alibi_slope(jnp.full((1, 1), h, jnp.int32))
        col = lax.broadcasted_iota(jnp.int32, (1, t), 1)
        bias = slope * ((j - i) * t + col - (t - 1)).astype(F32)
        if diagonal:
            keep = (lax.broadcasted_iota(jnp.int32, (t, t), 1) <= lax.broadcasted_iota(jnp.int32, (t, t), 0))
        q = (q_ref[...] * (DKA ** -0.5)).astype(BF16)
        k = k_ref[...].astype(BF16)
        v = v_ref[...].astype(BF16)
        for half, (m_ref, l_ref, a_ref) in enumerate(((m1, l1, a1), (m2, l2, a2))):
            sl = slice(half * DKA, (half + 1) * DKA)
            sc = _dot_nt(q[:, sl], k[:, sl]) + bias
            if diagonal:
                sc = jnp.where(keep, sc, -1e30)
            m_prev = m_ref[...]
            m_new = jnp.maximum(m_prev, jnp.max(sc, axis=-1, keepdims=True))
            alpha = jnp.exp(m_prev - m_new)
            p = jnp.exp(sc - jnp.concatenate([m_new] * (t // DVA), axis=-1))
            l_ref[...] = alpha * l_ref[...] + jnp.sum(p, axis=-1, keepdims=True)
            a_ref[...] = alpha * a_ref[...] + _dot(p.astype(BF16), v)
            m_ref[...] = m_new

    @pl.when(j < i)
    def _():
        step(False)

    @pl.when(j == i)
    def _():
        step(True)
        lam = _lambda_full(lq1[...], lk1[...], lq2[...], lk2[...])
        o = a1[...] / l1[...] - lam * (a2[...] / l2[...])
        o_ref[...] = _subln(o, gain_ref[...])


def _dattn_prompt(z, ka, va, lams, subln_gain, batch, seq, t):
    nt = seq // t
    pairs = [(i, j) for i in range(nt) for j in range(i + 1)]
    ii = jnp.asarray([p[0] for p in pairs], jnp.int32)
    jj = jnp.asarray([p[1] for p in pairs], jnp.int32)
    vec = pl.BlockSpec((1, DKA), lambda b, h, s, ii, jj: (0, 0))
    grid_spec = pltpu.PrefetchScalarGridSpec(
        num_scalar_prefetch=2,
        grid=(batch, HA, len(pairs)),
        in_specs=[
            pl.BlockSpec((t, DVA), lambda b, h, s, ii, jj: (b * nt + ii[s], h)),
            pl.BlockSpec((t, DVA), lambda b, h, s, ii, jj: (b * nt + jj[s], h)),
            pl.BlockSpec((t, DVA), lambda b, h, s, ii, jj: (b * nt + jj[s], h)),
            vec, vec, vec, vec,
            pl.BlockSpec((1, DVA), lambda b, h, s, ii, jj: (0, 0)),
        ],
        out_specs=pl.BlockSpec((t, DVA), lambda b, h, s, ii, jj: (b * nt + ii[s], h)),
        scratch_shapes=[pltpu.VMEM((t, DVA), F32)] * 6,
    )
    return pl.pallas_call(
        functools.partial(_dattn_prompt_kernel, t=t),
        grid_spec=grid_spec,
        out_shape=jax.ShapeDtypeStruct((batch * seq, HA * DVA), F32),
        compiler_params=_cparams(("arbitrary", "arbitrary", "arbitrary")),
        name="dattn_prompt",
    )(ii, jj, z, ka, va, *[l.reshape(1, DKA) for l in lams], subln_gain.reshape(1, DVA))


def _rows_by_head(row, nrow):
    head = lax.broadcasted_iota(jnp.int32, (nrow, 1), 0) * HA // nrow
    out = jnp.zeros((nrow, DVA), F32)
    for h in range(HA):
        out = jnp.where(head == h, row[:, h * DVA:(h + 1) * DVA], out)
    return out


def _dattn_sample_kernel(pt_ref, q_ref, kn_ref, vn_ref, *rest, n_pages, page, pps):
    kp_refs = rest[:pps]
    vp_refs = rest[pps:2 * pps]
    lq1, lk1, lq2, lk2, gain_ref, o_ref, qm_ref, m_ref, l_ref, a_ref = rest[2 * pps:]
    p = pl.program_id(1)
    nrow = 2 * HA
    rid = lax.broadcasted_iota(jnp.int32, (nrow, 1), 0)
    slope = _alibi_slope(rid // 2)
    past = n_pages * page

    @pl.when(p == 0)
    def _():
        half = lax.broadcasted_iota(jnp.int32, (nrow, DVA), 1) // DKA
        qm_ref[...] = jnp.where(half == rid % 2, _rows_by_head(q_ref[0], nrow), 0.0)
        m_ref[...] = jnp.full(m_ref.shape, -jnp.inf, F32)
        l_ref[...] = jnp.zeros(l_ref.shape, F32)
        a_ref[...] = jnp.zeros(a_ref.shape, F32)

    qm = qm_ref[...]
    qmb = qm.astype(BF16)
    n = lax.broadcasted_iota(jnp.int32, (1, page * HA), 1)
    own_head = n % HA == rid // 2
    scores = []
    for r in range(pps):
        rel = (past - ((p * pps + r) * page + n // HA)).astype(F32)
        kf = kp_refs[r][...].reshape(page * HA, DVA).astype(BF16)
        sc = _dot_nt(qmb, kf) * (DKA ** -0.5) - slope * rel
        scores.append(jnp.where(own_head, sc, -1e30))
    m_new = m_ref[...]
    for sc in scores:
        m_new = jnp.maximum(m_new, jnp.max(sc, axis=-1, keepdims=True))
    alpha = jnp.exp(m_ref[...] - m_new)
    l_new = alpha * l_ref[...]
    a_new = alpha * a_ref[...]
    for r, sc in enumerate(scores):
        pr = jnp.exp(sc - m_new)
        l_new = l_new + jnp.sum(pr, axis=-1, keepdims=True)
        a_new = a_new + _dot(pr.astype(BF16), vp_refs[r][...].reshape(page * HA, DVA).astype(BF16))
    l_ref[...] = l_new
    a_ref[...] = a_new
    m_ref[...] = m_new

    @pl.when(p == n_pages // pps - 1)
    def _():
        sc_n = jnp.sum(qm * _rows_by_head(kn_ref[0], nrow), axis=-1, keepdims=True) * (DKA ** -0.5)
        m_f = jnp.maximum(m_ref[...], sc_n)
        al = jnp.exp(m_ref[...] - m_f)
        pn = jnp.exp(sc_n - m_f)
        l_f = al * l_ref[...] + pn
        a_f = (al * a_ref[...] + pn * _rows_by_head(vn_ref[0], nrow)) / l_f
        lam = _lambda_full(lq1[...], lk1[...], lq2[...], lk2[...])
        outs = []
        for h in range(HA):
            o = a_f[2 * h:2 * h + 1] - lam * a_f[2 * h + 1:2 * h + 2]
            outs.append(_subln(o, gain_ref[...]))
        o_ref[0] = jnp.concatenate(outs, axis=-1)


def _dattn_sample(z, ka, va, cache_k, cache_v, page_table, lams, subln_gain):
    nb, n_pages = page_table.shape
    page = cache_k.shape[1]
    w = HA * DVA
    row3 = lambda a: a.reshape(nb, 1, a.shape[-1])
    vec = pl.BlockSpec((1, DKA), lambda b, p, pt: (0, 0))
    pps = max(d for d in range(1, PAGES_PER_STEP + 1) if n_pages % d == 0)
    page_specs = [
        pl.BlockSpec((None, page, HA, DVA), lambda b, p, pt, r=r: (pt[b * n_pages + p * pps + r], 0, 0, 0))
        for r in range(pps)]
    grid_spec = pltpu.PrefetchScalarGridSpec(
        num_scalar_prefetch=1,
        grid=(nb, n_pages // pps),
        in_specs=[
            pl.BlockSpec((1, 1, w), lambda b, p, pt: (b, 0, 0)),
            pl.BlockSpec((1, 1, w), lambda b, p, pt: (b, 0, 0)),
            pl.BlockSpec((1, 1, w), lambda b, p, pt: (b, 0, 0)),
            *page_specs, *page_specs,
            vec, vec, vec, vec,
            pl.BlockSpec((1, DVA), lambda b, p, pt: (0, 0)),
        ],
        out_specs=pl.BlockSpec((1, 1, w), lambda b, p, pt: (b, 0, 0)),
        scratch_shapes=[pltpu.VMEM((2 * HA, DVA), F32), pltpu.VMEM((2 * HA, 1), F32),
                        pltpu.VMEM((2 * HA, 1), F32), pltpu.VMEM((2 * HA, DVA), F32)],
    )
    out = pl.pallas_call(
        functools.partial(_dattn_sample_kernel, n_pages=n_pages, page=page, pps=pps),
        grid_spec=grid_spec,
        out_shape=jax.ShapeDtypeStruct((nb, 1, w), F32),
        compiler_params=_cparams(("arbitrary", "arbitrary")),
        name="dattn_sample",
    )(page_table.reshape(-1), row3(z), row3(ka), row3(va), *([cache_k] * pps), *([cache_v] * pps),
      *[l.reshape(1, DKA) for l in lams], subln_gain.reshape(1, DVA))
    return out.reshape(nb, w)


def _ret_log_gamma(h):
    return math.log(1.0 - 2.0 ** (-5.0 - h))


def _ret_theta(width):
    theta = 1.0 / (ROPE_BASE ** jnp.linspace(0.0, 1.0, DKB // 2, dtype=F32))
    return jnp.tile(jnp.repeat(theta, 2), width // DKB).reshape(1, width)


def _rotate_pairs(x, cos, sin):
    n = x.shape[-1]
    lane = lax.broadcasted_iota(jnp.int32, x.shape, x.ndim - 1)
    nxt = pltpu.roll(x, n - 1, axis=x.ndim - 1)
    prv = pltpu.roll(x, 1, axis=x.ndim - 1)
    return x * cos + jnp.where(lane % 2 == 0, -nxt, prv) * sin


def _ret_prompt_kernel(q_ref, k_ref, v_ref, g_ref, th_ref, o_ref, s_ref, dec_ref, *, c):
    ci = pl.program_id(1)
    ri = lax.broadcasted_iota(jnp.int32, (c, c), 0)
    cj = lax.broadcasted_iota(jnp.int32, (c, c), 1)

    @pl.when(jnp.logical_and(pl.program_id(0) == 0, ci == 0))
    def _():
        diff = (ri - cj).astype(F32)
        for h in range(HB):
            dec_ref[h] = jnp.where(diff >= 0, jnp.exp(_ret_log_gamma(h) * jnp.maximum(diff, 0.0)), 0.0)

    @pl.when(ci == 0)
    def _():
        s_ref[...] = jnp.zeros(s_ref.shape, F32)

    idx = lax.broadcasted_iota(jnp.int32, (c, 1), 0)
    pos = (ci * c + idx).astype(F32)
    ang = pos * th_ref[...]
    cos = jnp.concatenate([jnp.cos(ang)] * (HB * DKB // 128), axis=-1)
    sin = jnp.concatenate([jnp.sin(ang)] * (HB * DKB // 128), axis=-1)
    qr = _rotate_pairs(q_ref[...], cos, sin)
    kr = _rotate_pairs(k_ref[...], cos, sin) * (DKB ** -0.5)
    idx_f = idx.astype(F32)
    outs = []
    for h in range(HB):
        lg = _ret_log_gamma(h)
        qh = qr[:, h * DKB:(h + 1) * DKB].astype(BF16)
        kh = kr[:, h * DKB:(h + 1) * DKB]
        vh = v_ref[:, h * DVB:(h + 1) * DVB].astype(BF16)
        s_old = s_ref[0, h]
        scores = _dot_nt(qh, kh.astype(BF16)) * dec_ref[h]
        o = _dot(scores.astype(BF16), vh)
        o = o + _dot(qh, s_old.astype(BF16)) * jnp.exp(lg * (idx_f + 1.0))
        kd = (kh * jnp.exp(lg * (c - 1.0 - idx_f))).astype(BF16)
        s_ref[0, h] = math.exp(lg * c) * s_old + _dot_tn(kd, vh)
        gate = g_ref[:, h * DVB:(h + 1) * DVB]
        outs.append(_rms(o) * (gate * _sigmoid(gate)))
    o_ref[...] = jnp.concatenate(outs, axis=-1)


def _ret_prompt(z, batch, seq, c):
    nc = seq // c
    wq = HB * DKB
    wv = HB * DVB
    return pl.pallas_call(
        functools.partial(_ret_prompt_kernel, c=c),
        grid=(batch, nc),
        in_specs=[
            pl.BlockSpec((c, wq), lambda b, i: (b * nc + i, 2)),
            pl.BlockSpec((c, wq), lambda b, i: (b * nc + i, 3)),
            pl.BlockSpec((c, wv), lambda b, i: (b * nc + i, 2)),
            pl.BlockSpec((c, wv), lambda b, i: (b * nc + i, 3)),
            pl.BlockSpec((1, 128), lambda b, i: (0, 0)),
        ],
        out_specs=[
            pl.BlockSpec((c, wv), lambda b, i: (b * nc + i, 0)),
            pl.BlockSpec((1, HB, DKB, DVB), lambda b, i: (b, 0, 0, 0)),
        ],
        out_shape=[
            jax.ShapeDtypeStruct((batch * seq, wv), F32),
            jax.ShapeDtypeStruct((batch, HB, DKB, DVB), F32),
        ],
        scratch_shapes=[pltpu.VMEM((HB, c, c), F32)],
        compiler_params=_cparams(("arbitrary", "arbitrary")),
        name="ret_prompt",
    )(z, z, z, z, _ret_theta(128))


def _col_of_row(row, eye):
    return jnp.sum(eye * row, axis=-1, keepdims=True)


def _ret_sample_kernel(q_ref, k_ref, v_ref, g_ref, th_ref, s_ref, o_ref, so_ref, *, pos):
    ang = float(pos) * th_ref[...]
    cos = jnp.cos(ang)
    sin = jnp.sin(ang)
    qr = _rotate_pairs(q_ref[0], cos, sin)
    kr = _rotate_pairs(k_ref[0], cos, sin) * (DKB ** -0.5)
    eye = (lax.broadcasted_iota(jnp.int32, (DKB, DKB), 0)
           == lax.broadcasted_iota(jnp.int32, (DKB, DKB), 1)).astype(F32)
    outs = []
    for h in range(HB):
        gamma = math.exp(_ret_log_gamma(h))
        qh = qr[:, h * DKB:(h + 1) * DKB]
        kh = kr[:, h * DKB:(h + 1) * DKB]
        vh = v_ref[0, :, h * DVB:(h + 1) * DVB]
        s_old = s_ref[0, h]
        qs = jnp.sum(_col_of_row(qh, eye) * s_old, axis=0, keepdims=True)
        qk = jnp.sum(qh * kh, axis=-1, keepdims=True)
        o = qk * vh + qs * gamma
        so_ref[0, h] = gamma * s_old + _col_of_row(kh, eye) * vh
        gate = g_ref[0, :, h * DVB:(h + 1) * DVB]
        outs.append(_rms(o) * (gate * _sigmoid(gate)))
    o_ref[0] = jnp.concatenate(outs, axis=-1)


def _ret_sample(z, state, pos):
    nb = z.shape[0]
    wq = HB * DKB
    wv = HB * DVB
    z3 = z.reshape(nb, 1, z.shape[-1])
    out, s_new = pl.pallas_call(
        functools.partial(_ret_sample_kernel, pos=pos),
        grid=(nb,),
        in_specs=[
            pl.BlockSpec((1, 1, wq), lambda b: (b, 0, 2)),
            pl.BlockSpec((1, 1, wq), lambda b: (b, 0, 3)),
            pl.BlockSpec((1, 1, wv), lambda b: (b, 0, 2)),
            pl.BlockSpec((1, 1, wv), lambda b: (b, 0, 3)),
            pl.BlockSpec((1, wq), lambda b: (0, 0)),
            pl.BlockSpec((1, HB, DKB, DVB), lambda b: (b, 0, 0, 0)),
        ],
        out_specs=[
            pl.BlockSpec((1, 1, wv), lambda b: (b, 0, 0)),
            pl.BlockSpec((1, HB, DKB, DVB), lambda b: (b, 0, 0, 0)),
        ],
        out_shape=[
            jax.ShapeDtypeStruct((nb, 1, wv), F32),
            jax.ShapeDtypeStruct(state.shape, F32),
        ],
        compiler_params=_cparams(("arbitrary",)),
        name="ret_sample",
    )(z3, z3, z3, z3, _ret_theta(wq), state)
    return out.reshape(nb, wv), s_new


def _mm_resid_kernel(*refs, n_in):
    a_refs = refs[:n_in]
    w_refs = refs[n_in:2 * n_in]
    x_ref = refs[2 * n_in]
    o_ref = refs[2 * n_in + 1]
    acc = x_ref[...]
    for a_ref, w_ref in zip(a_refs, w_refs):
        acc = acc + _dot(a_ref[...].astype(BF16), w_ref[...])
    o_ref[...] = acc


def _mm_resid(a_list, w_bf16, resid, tm, tn=1024):
    m, n = resid.shape
    in_specs = []
    for a in a_list:
        in_specs.append(pl.BlockSpec((tm, a.shape[1]), lambda i, j: (i, 0)))
    row = 0
    for a in a_list:
        kb = a.shape[1]
        in_specs.append(pl.BlockSpec((kb, tn), lambda i, j, r=row // kb: (r, j)))
        row += kb
    in_specs.append(pl.BlockSpec((tm, tn), lambda i, j: (i, j)))
    return pl.pallas_call(
        functools.partial(_mm_resid_kernel, n_in=len(a_list)),
        grid=(m // tm, n // tn),
        in_specs=in_specs,
        out_specs=pl.BlockSpec((tm, tn), lambda i, j: (i, j)),
        out_shape=jax.ShapeDtypeStruct((m, n), F32),
        compiler_params=_cparams(("arbitrary", "arbitrary")),
        name="mm_resid",
    )(*a_list, *([w_bf16] * len(a_list)), resid)


def _ffn_kernel(x_ref, g_ref, wg_ref, wu_ref, wd_ref, fg_ref, o_ref, xn_ref, acc_ref, *, final_norm):
    j = pl.program_id(1)

    @pl.when(j == 0)
    def _():
        xn_ref[...] = (_rms(x_ref[...]) * g_ref[...]).astype(BF16)
        acc_ref[...] = jnp.zeros(acc_ref.shape, F32)

    xn = xn_ref[...]
    gate = _dot(xn, wg_ref[...])
    up = _dot(xn, wu_ref[...])
    hid = (gate * _sigmoid(gate) * up).astype(BF16)
    acc_ref[...] += _dot(hid, wd_ref[...])

    @pl.when(j == pl.num_programs(1) - 1)
    def _():
        y = x_ref[...] + acc_ref[...]
        if final_norm:
            y = _rms(y) * fg_ref[...]
        o_ref[...] = y


def _ffn(x, gain, wg, wu, wd, final_gain, final_norm, tm, tf=512):
    m = x.shape[0]
    dff = wg.shape[1]
    return pl.pallas_call(
        functools.partial(_ffn_kernel, final_norm=final_norm),
        grid=(m // tm, dff // tf),
        in_specs=[
            pl.BlockSpec((tm, D_MODEL), lambda i, j: (i, 0)),
            pl.BlockSpec((1, D_MODEL), lambda i, j: (0, 0)),
            pl.BlockSpec((D_MODEL, tf), lambda i, j: (0, j)),
            pl.BlockSpec((D_MODEL, tf), lambda i, j: (0, j)),
            pl.BlockSpec((tf, D_MODEL), lambda i, j: (j, 0)),
            pl.BlockSpec((1, D_MODEL), lambda i, j: (0, 0)),
        ],
        out_specs=pl.BlockSpec((tm, D_MODEL), lambda i, j: (i, 0)),
        out_shape=jax.ShapeDtypeStruct((m, D_MODEL), F32),
        scratch_shapes=[pltpu.VMEM((tm, D_MODEL), BF16), pltpu.VMEM((tm, D_MODEL), F32)],
        compiler_params=_cparams(("arbitrary", "arbitrary")),
        name="ffn",
    )(x, gain.reshape(1, D_MODEL), wg, wu, wd, final_gain.reshape(1, D_MODEL))


def _norm_and_shift(x_ref, g_ref, prev_ref, carry_ref, i, tiles_per_seq):
    h = _rms(x_ref[...]) * g_ref[...]
    if prev_ref is None:
        tm = h.shape[0]
        first = (i % tiles_per_seq) == 0
        prev_row = jnp.where(first, 0.0, carry_ref[0:1, :])
        rows = lax.broadcasted_iota(jnp.int32, (tm, 1), 0)
        shifted = jnp.where(rows == 0, prev_row, pltpu.roll(h, 1, axis=0))
        carry_ref[0:1, :] = h[tm - 1:tm, :]
    else:
        shifted = prev_ref[...]
    return h, shifted - h


def _rwkv_rkv_kernel(*refs, per_row_prev, tiles_per_seq):
    if per_row_prev:
        x_ref, g_ref, prev_ref, mu_ref, w_ref, o_ref, hl_ref, h_ref, xx_ref, carry_ref = refs
    else:
        x_ref, g_ref, mu_ref, w_ref, o_ref, hl_ref, h_ref, xx_ref, carry_ref = refs
        prev_ref = None
    i = pl.program_id(0)
    j = pl.program_id(1)

    @pl.when(j == 0)
    def _():
        h, xx = _norm_and_shift(x_ref, g_ref, prev_ref, carry_ref, i, tiles_per_seq)
        h_ref[...] = h
        xx_ref[...] = xx
        if per_row_prev:
            hl_ref[...] = h
        else:
            hl_ref[0] = h[h.shape[0] - 1:, :]

    xm = (h_ref[...] + xx_ref[...] * mu_ref[0]).astype(BF16)
    o_ref[...] = _dot(xm, w_ref[...])


def _rwkv_rkv(x, gain, prev, mu_rkv, w_rkv, tm, tiles_per_seq):
    m = x.shape[0]
    tn = 1024
    per_seq = D_MODEL // tn
    per_row_prev = prev is not None
    in_specs = [pl.BlockSpec((tm, D_MODEL), lambda i, j: (i, 0)),
                pl.BlockSpec((1, D_MODEL), lambda i, j: (0, 0))]
    args = [x, gain.reshape(1, D_MODEL)]
    if per_row_prev:
        in_specs.append(pl.BlockSpec((tm, D_MODEL), lambda i, j: (i, 0)))
        args.append(prev)
        hl_spec = pl.BlockSpec((tm, D_MODEL), lambda i, j: (i, 0))
        hl_shape = jax.ShapeDtypeStruct((m, D_MODEL), F32)
    else:
        nseq = m // (tm * tiles_per_seq)
        hl_spec = pl.BlockSpec((1, 1, D_MODEL), lambda i, j: (i // tiles_per_seq, 0, 0))
        hl_shape = jax.ShapeDtypeStruct((nseq, 1, D_MODEL), F32)
    in_specs += [pl.BlockSpec((1, 1, D_MODEL), lambda i, j: (j // per_seq, 0, 0)),
                 pl.BlockSpec((D_MODEL, tn), lambda i, j: (0, j))]
    args += [mu_rkv, w_rkv]
    return pl.pallas_call(
        functools.partial(_rwkv_rkv_kernel, per_row_prev=per_row_prev, tiles_per_seq=tiles_per_seq),
        grid=(m // tm, 3 * per_seq),
        in_specs=in_specs,
        out_specs=[pl.BlockSpec((tm, tn), lambda i, j: (i, j)), hl_spec],
        out_shape=[jax.ShapeDtypeStruct((m, 3 * D_MODEL), F32), hl_shape],
        scratch_shapes=[pltpu.VMEM((tm, D_MODEL), F32), pltpu.VMEM((tm, D_MODEL), F32),
                        pltpu.VMEM((8, D_MODEL), F32)],
        compiler_params=_cparams(("arbitrary", "arbitrary")),
        name="rwkv_rkv",
    )(*args)


def _rwkv_lora_kernel(*refs, per_row_prev, tiles_per_seq):
    if per_row_prev:
        (x_ref, g_ref, prev_ref, mu_ref, w0_ref, w1_ref, w2_ref, a0_ref, a1_ref, a2_ref, g1_ref, g2_ref,
         lw_ref, a_ref, go_ref, carry_ref) = refs
    else:
        (x_ref, g_ref, mu_ref, w0_ref, w1_ref, w2_ref, a0_ref, a1_ref, a2_ref, g1_ref, g2_ref,
         lw_ref, a_ref, go_ref, carry_ref) = refs
        prev_ref = None
    h, xx = _norm_and_shift(x_ref, g_ref, prev_ref, carry_ref, pl.program_id(0), tiles_per_seq)
    xw = (h + xx * mu_ref[0]).astype(BF16)
    xa = (h + xx * mu_ref[1]).astype(BF16)
    xg = (h + xx * mu_ref[2]).astype(BF16)
    zw = w0_ref[...] + _dot(jnp.tanh(_dot(xw, w1_ref[...])).astype(BF16), w2_ref[...])
    softplus = jnp.maximum(-zw, 0.0) + jnp.log(1.0 + jnp.exp(-jnp.abs(zw)))
    lw_ref[...] = -jnp.exp(-softplus - 0.5)
    a_ref[...] = _sigmoid(a0_ref[...] + _dot(_dot(xa, a1_ref[...]).astype(BF16), a2_ref[...]))
    go_ref[...] = _dot(_sigmoid(_dot(xg, g1_ref[...])).astype(BF16), g2_ref[...])


def _rwkv_lora(x, gain, prev, mu_wag, w0, w1, w2, a0, a1, a2, g1, g2, tm, tiles_per_seq):
    m = x.shape[0]
    per_row_prev = prev is not None
    full = lambda a: pl.BlockSpec(a.shape, lambda i: (0,) * a.ndim)
    row = pl.BlockSpec((tm, D_MODEL), lambda i: (i, 0))
    args = [x, gain.reshape(1, D_MODEL)]
    in_specs = [row, full(args[1])]
    if per_row_prev:
        in_specs.append(row)
        args.append(prev)
    rest = [mu_wag, w0.reshape(1, D_MODEL), w1, w2, a0.reshape(1, D_MODEL), a1, a2, g1, g2]
    in_specs += [full(a) for a in rest]
    args += rest
    return pl.pallas_call(
        functools.partial(_rwkv_lora_kernel, per_row_prev=per_row_prev, tiles_per_seq=tiles_per_seq),
        grid=(m // tm,),
        in_specs=in_specs,
        out_specs=[row, row, row],
        out_shape=[jax.ShapeDtypeStruct((m, D_MODEL), F32)] * 3,
        scratch_shapes=[pltpu.VMEM((8, D_MODEL), F32)],
        compiler_params=_cparams(("arbitrary",)),
        name="rwkv_lora",
    )(*args)


def _wkv_keys(k, a, kk_gain, ka_gain, head_sum):
    kk = k * kk_gain
    kk = kk / jnp.maximum(jnp.sqrt(head_sum(kk * kk)), 1e-12)
    return kk, kk * a, k * (1.0 + (a - 1.0) * ka_gain)


def _wkv_finish(o, r, k2, v, g, rk, ln_g, ln_b, head_sum):
    mean = head_sum(o) * (1.0 / NC)
    var = head_sum(jnp.square(o - mean)) * (1.0 / NC)
    on = (o - mean) * lax.rsqrt(var + GN_EPS) * ln_g + ln_b
    bonus = head_sum(r * k2 * rk) * v
    return (on + bonus) * g


def _pair_sum(x):
    lane = lax.broadcasted_iota(jnp.int32, x.shape, 1)
    s0 = jnp.sum(x[:, :NC], axis=-1, keepdims=True)
    s1 = jnp.sum(x[:, NC:], axis=-1, keepdims=True)
    return jnp.where(lane < NC, s0, s1)


def _bmm(a, b):
    return jnp.einsum("cij,cjk->cik", a, b, preferred_element_type=F32)


def _bmm_nt(a, b):
    return jnp.einsum("cik,cjk->cij", a, b, preferred_element_type=F32)


def _wkv_prompt_kernel(r_ref, k_ref, v_ref, lw_ref, a_ref, g_ref, kkg_ref, kag_ref, rk_ref, lng_ref, lnb_ref,
                       y_ref, so_ref, st_ref, *, tc):
    c = WKV_CHUNK
    nch = tc // c
    ti = pl.program_id(2)

    @pl.when(ti == 0)
    def _():
        st_ref[...] = jnp.zeros(st_ref.shape, F32)

    ri = lax.broadcasted_iota(jnp.int32, (c, c), 0)
    cj = lax.broadcasted_iota(jnp.int32, (c, c), 1)
    tri = (ri >= cj).astype(BF16)
    eye = (ri == cj).astype(F32)
    gi = lax.broadcasted_iota(jnp.int32, (2 * c, 2 * c), 0)
    gj = lax.broadcasted_iota(jnp.int32, (2 * c, 2 * c), 1) % c
    gmask = (gj < jnp.where(gi < c, gi, gi - c + 1))[None]

    r = r_ref[...]
    v = v_ref[...]
    kk, bv, k2 = _wkv_keys(k_ref[...], a_ref[...], kkg_ref[...], kag_ref[...], _pair_sum)
    to3 = lambda x: x.reshape(nch, c, 2 * NC)
    lw3 = to3(lw_ref[...])
    p0 = lw3.astype(BF16)
    r1 = lw3 - p0.astype(F32)
    p1 = r1.astype(BF16)
    p2 = (r1 - p1.astype(F32)).astype(BF16)
    trib = jnp.broadcast_to(tri[None], (nch, c, c))
    cum = _bmm(trib, p0) + (_bmm(trib, p1) + _bmm(trib, p2))
    cum_last = cum[:, c - 1:c, :]
    w_rest = jnp.exp(cum_last - cum)
    w_inv = jnp.exp(-cum)
    am = -to3(kk) * jnp.exp(cum - lw3)
    rm = to3(r) * jnp.exp(cum)
    bm = (to3(bv) * w_inv).astype(BF16)
    km = (to3(k2) * w_inv).astype(BF16)
    bt = (to3(bv) * w_rest).astype(BF16)
    kt = (to3(k2) * w_rest).astype(BF16)
    v3 = to3(v)
    w_end = jnp.exp(cum_last)

    heads = lambda x: jnp.concatenate([x[:, :, :NC], x[:, :, NC:]], axis=0)
    am_h, rm_h, v_h = heads(am), heads(rm), heads(v3)
    x1 = jnp.concatenate([am_h, rm_h], axis=1).astype(BF16)
    x2 = jnp.concatenate([heads(bm), heads(km)], axis=1)
    g4 = jnp.where(gmask, _bmm_nt(x1, x2), 0.0)
    lv = _bmm(g4[:, :, c:].astype(BF16), v_h.astype(BF16))
    x = jnp.concatenate([am_h, lv[:, :c]], axis=-1)
    lp = g4[:, :c, :c].astype(BF16)
    steps = int(math.log2(c))
    for s in range(steps):
        x = x + _bmm(lp, x.astype(BF16))
        if s + 1 < steps:
            lp = _bmm(lp, lp).astype(BF16)
    gh = _bmm(g4[:, c:, :c].astype(BF16), x.astype(BF16)) + jnp.concatenate([rm_h, lv[:, c:]], axis=-1)
    btk = jnp.concatenate([heads(bt), heads(kt)], axis=1)
    zv = jnp.concatenate([jnp.zeros_like(v_h), v_h], axis=-1)
    pqv = jnp.concatenate([x, zv], axis=1).astype(BF16)
    ef = [_dot_tn(btk[n], pqv[n]) for n in range(2 * nch)]
    sts = [st_ref[0], st_ref[1]]
    o_chunks = [[], []]
    for ch in range(nch):
        for hh in range(2):
            n = hh * nch + ch
            ge = jnp.concatenate([gh[n][:, :NC], ef[n][:, :NC]], axis=0).astype(BF16)
            res = _dot(ge, sts[hh].astype(BF16))
            o_chunks[hh].append(res[:c] + gh[n][:, NC:])
            w_col = jnp.sum(eye * w_end[ch][:, hh * NC:(hh + 1) * NC], axis=-1, keepdims=True)
            sts[hh] = w_col * sts[hh] + res[c:] + ef[n][:, NC:]
    st_ref[0] = sts[0]
    st_ref[1] = sts[1]

    @pl.when(ti == pl.num_programs(2) - 1)
    def _():
        so_ref[0, 0] = sts[0].T
        so_ref[0, 1] = sts[1].T

    o = jnp.concatenate([jnp.concatenate(oc, axis=0) for oc in o_chunks], axis=-1)
    y_ref[...] = _wkv_finish(o, r, k2, v, g_ref[...], rk_ref[...], lng_ref[...], lnb_ref[...], _pair_sum)


def _wkv_prompt(rkv, lw, a, g, kk_gain, ka_gain, rk, ln_g, ln_b, batch, seq, tc):
    nt = seq // tc
    nhp = HC // 2
    tok = lambda off: pl.BlockSpec((tc, 2 * NC), lambda b, hp, t: (b * nt + t, off + hp))
    par = pl.BlockSpec((1, 2 * NC), lambda b, hp, t: (0, hp))
    return pl.pallas_call(
        functools.partial(_wkv_prompt_kernel, tc=tc),
        grid=(batch, nhp, nt),
        in_specs=[tok(0), tok(nhp), tok(2 * nhp), tok(0), tok(0), tok(0), par, par, par, par, par],
        out_specs=[
            pl.BlockSpec((tc, 2 * NC), lambda b, hp, t: (b * nt + t, hp)),
            pl.BlockSpec((1, 2, NC, NC), lambda b, hp, t: (b, hp, 0, 0)),
        ],
        out_shape=[
            jax.ShapeDtypeStruct((batch * seq, D_MODEL), F32),
            jax.ShapeDtypeStruct((batch, HC, NC, NC), F32),
        ],
        scratch_shapes=[pltpu.VMEM((2, NC, NC), F32)],
        compiler_params=_cparams(("arbitrary", "arbitrary", "arbitrary")),
        name="wkv_prompt",
    )(rkv, rkv, rkv, lw, a, g, kk_gain, ka_gain, rk, ln_g, ln_b)


def _wkv_sample_kernel(r_ref, k_ref, v_ref, lw_ref, a_ref, g_ref, kkg_ref, kag_ref, rk_ref, lng_ref, lnb_ref,
                       s_ref, y_ref, so_ref, o_ref):
    ys = []
    col_sum = lambda x: jnp.sum(x, axis=0, keepdims=True)
    for hh in range(2):
        sl = slice(hh * NC, (hh + 1) * NC)
        t = lambda ref: ref[...].T[sl, :]
        r, k, v, a, g = t(r_ref), t(k_ref), t(v_ref), t(a_ref), t(g_ref)
        w = jnp.exp(t(lw_ref))
        kk, bv, k2 = _wkv_keys(k, a, kkg_ref[sl, :], kag_ref[sl, :], col_sum)
        for i in range(NC):
            s_old = s_ref[hh, i]
            sk = jnp.sum(s_old * kk, axis=0, keepdims=True)
            s_new = s_old * w - sk * bv + v[i:i + 1, :] * k2
            so_ref[hh, i] = s_new
            o_ref[i:i + 1, :] = jnp.sum(s_new * r, axis=0, keepdims=True)
        ys.append(_wkv_finish(o_ref[...], r, k2, v, g, rk_ref[sl, :], lng_ref[sl, :], lnb_ref[sl, :], col_sum))
    y_ref[...] = jnp.concatenate(ys, axis=0).T


def _wkv_sample(rkv, lw, a, g, kk_gain, ka_gain, rk, ln_g, ln_b, state_t):
    nb = lw.shape[0]
    nhp = HC // 2
    tok = lambda off: pl.BlockSpec((nb, 2 * NC), lambda hp: (0, off + hp))
    par = pl.BlockSpec((2 * NC, 1), lambda hp: (hp, 0))
    col = lambda p: p.reshape(D_MODEL, 1)
    st_spec = pl.BlockSpec((2, NC, NC, nb), lambda hp: (hp, 0, 0, 0))
    return pl.pallas_call(
        _wkv_sample_kernel,
        grid=(nhp,),
        in_specs=[tok(0), tok(nhp), tok(2 * nhp), tok(0), tok(0), tok(0), par, par, par, par, par, st_spec],
        out_specs=[pl.BlockSpec((nb, 2 * NC), lambda hp: (0, hp)), st_spec],
        out_shape=[
            jax.ShapeDtypeStruct((nb, D_MODEL), F32),
            jax.ShapeDtypeStruct(state_t.shape, F32),
        ],
        scratch_shapes=[pltpu.VMEM((NC, nb), F32)],
        compiler_params=_cparams(("arbitrary",)),
        name="wkv_sample",
    )(rkv, rkv, rkv, lw, a, g, col(kk_gain), col(ka_gain), col(rk), col(ln_g), col(ln_b), state_t)


def _row_tile(m, want):
    return min(m, want)


def _trunk(x, batch, seq, past, W):
    m = x.shape[0]
    prompt = past is None
    tm = _row_tile(m, 512)
    lams = (W["lambda_q1"], W["lambda_k1"], W["lambda_q2"], W["lambda_k2"])

    ka, va, z = _inproj0(x, W["norm0_mix"], W["w_in0"], tm)
    if prompt:
        o_a = _dattn_prompt(z, ka, va, lams, W["subln_gain"], batch, seq, min(seq, 512))
        o_b, ret_new = _ret_prompt(z, batch, seq, min(seq, 256))
    else:
        o_a = _dattn_sample(z, ka, va, past["cache_k"], past["cache_v"], past["page_table"], lams,
                            W["subln_gain"])
        o_b, ret_new = _ret_sample(z, past["state_ret"], past["pos"])
    x = _mm_resid([o_a, o_b], W["w_out0"], x, tm)
    x = _ffn(x, W["norm0_ffn"], W["w_gate0"], W["w_up0"], W["w_down0"], W["norm_final"], False, tm)

    tiles_per_seq = max(seq // tm, 1)
    prev = None if prompt else past["state_shift"]
    rkv, hlast = _rwkv_rkv(x, W["norm1_mix"], prev, W["mu_rkv"], W["w_rkv"], tm, tiles_per_seq)
    tl = _row_tile(m, 256)
    lw, a, g = _rwkv_lora(x, W["norm1_mix"], prev, W["mu_wag"], W["decay_w0"], W["decay_w1"], W["decay_w2"],
                          W["aaa_a0"], W["aaa_a1"], W["aaa_a2"], W["gate_g1"], W["gate_g2"],
                          tl, max(seq // tl, 1))
    head_params = (W["k_k"], W["k_a"], W["r_k"], W["lnx_gain"], W["lnx_bias"])
    if prompt:
        y, wkv_new = _wkv_prompt(rkv, lw, a, g, *head_params, batch, seq, min(seq, 512))
        shift_new = hlast.reshape(batch, D_MODEL)
    else:
        state_t = jnp.transpose(past["state_wkv"], (1, 2, 3, 0))
        y, wkv_t = _wkv_sample(rkv, lw, a, g, *head_params, state_t)
        wkv_new = jnp.transpose(wkv_t, (3, 0, 1, 2))
        shift_new = hlast
    x = _mm_resid([y], W["w_out1"], x, tm)
    x = _ffn(x, W["norm1_ffn"], W["w_gate1"], W["w_up1"], W["w_down1"], W["norm_final"], True, tm)
    return x, ka, va, ret_new, wkv_new, shift_new


def kernel(x_prompt, x_sample, cache_k, cache_v, page_table, state_ret, state_wkv, state_shift, norm0_mix, w_in0, lambda_q1, lambda_k1, lambda_q2, lambda_k2, subln_gain, w_out0, norm0_ffn, w_gate0, w_up0, w_down0, norm1_mix, mu1, w_r1, w_k1, w_v1, decay_w0, decay_w1, decay_w2, aaa_a0, aaa_a1, aaa_a2, gate_g1, gate_g2, k_k, k_a, r_k, lnx_gain, lnx_bias, w_out1, norm1_ffn, w_gate1, w_up1, w_down1, norm_final):
    bf = lambda w: w.astype(BF16)
    row = lambda p: p.reshape(1, D_MODEL)
    W = dict(
        norm0_mix=norm0_mix, w_in0=bf(w_in0), lambda_q1=lambda_q1, lambda_k1=lambda_k1, lambda_q2=lambda_q2,
        lambda_k2=lambda_k2, subln_gain=subln_gain, w_out0=bf(w_out0), norm0_ffn=norm0_ffn,
        w_gate0=bf(w_gate0), w_up0=bf(w_up0), w_down0=bf(w_down0),
        norm1_mix=norm1_mix,
        mu_rkv=jnp.stack([mu1[0], mu1[2], mu1[3]]).reshape(3, 1, D_MODEL),
        mu_wag=jnp.stack([mu1[1], mu1[4], mu1[5]]).reshape(3, 1, D_MODEL),
        w_rkv=jnp.concatenate([bf(w_r1), bf(w_k1), bf(w_v1)], axis=1),
        decay_w0=decay_w0, decay_w1=bf(decay_w1), decay_w2=bf(decay_w2),
        aaa_a0=aaa_a0, aaa_a1=bf(aaa_a1), aaa_a2=bf(aaa_a2), gate_g1=bf(gate_g1), gate_g2=bf(gate_g2),
        k_k=row(k_k), k_a=row(k_a), r_k=row(r_k), lnx_gain=row(lnx_gain), lnx_bias=row(lnx_bias),
        w_out1=bf(w_out1), norm1_ffn=norm1_ffn, w_gate1=bf(w_gate1), w_up1=bf(w_up1), w_down1=bf(w_down1),
        norm_final=norm_final,
    )
    bsz, seq = x_prompt.shape[:2]
    y_p, k_p, v_p, ret_p, wkv_p, shift_p = _trunk(x_prompt.reshape(bsz * seq, D_MODEL), bsz, seq, None, W)
    nb, dseq = x_sample.shape[:2]
    past_len = page_table.shape[1] * cache_k.shape[1]
    past = dict(cache_k=cache_k, cache_v=cache_v, page_table=page_table, state_ret=state_ret,
                state_wkv=state_wkv, state_shift=state_shift, pos=past_len)
    y_s, k_s, v_s, ret_s, wkv_s, shift_s = _trunk(x_sample.reshape(nb * dseq, D_MODEL), nb, dseq, past, W)
    return (y_p.reshape(bsz, seq, D_MODEL), y_s.reshape(nb, dseq, D_MODEL),
            k_p.reshape(bsz, seq, HA, 2 * DKA), v_p.reshape(bsz, seq, HA, DVA),
            k_s.reshape(nb, dseq, HA, 2 * DKA), v_s.reshape(nb, dseq, HA, DVA),
            ret_p, ret_s, wkv_p, wkv_s, shift_p, shift_s)
```

```python
import functools
import math

import jax
import jax.numpy as jnp
from jax import lax
from jax.experimental import pallas as pl
from jax.experimental.pallas import tpu as pltpu

F32 = jnp.float32
BF16 = jnp.bfloat16

D_MODEL = 2048
EPS = 1e-6
DKA = 64
DVA = 128
HA = 8
LAMBDA_INIT = 0.8 - 0.6 * math.exp(-0.3 * 0)
DKB = 64
DVB = 128
HB = 8
ROPE_BASE = 10000.0
NC = 64
HC = D_MODEL // NC
GN_EPS = 64e-5
WKV_CHUNK = 64
PAGES_PER_STEP = 8
VMEM_LIMIT = 48 * 1024 * 1024


def _cparams(sem):
    return pltpu.CompilerParams(dimension_semantics=sem, vmem_limit_bytes=VMEM_LIMIT)


def _rms(x, eps=EPS):
    return x * lax.rsqrt(jnp.mean(x * x, axis=-1, keepdims=True) + eps)


def _sigmoid(x):
    return 1.0 / (1.0 + jnp.exp(-x))


def _dot(a, b):
    return jnp.dot(a, b, preferred_element_type=F32)


def _dot_nt(a, b):
    return lax.dot_general(a, b, (((1,), (1,)), ((), ())), preferred_element_type=F32)


def _dot_tn(a, b):
    return lax.dot_general(a, b, (((0,), (0,)), ((), ())), preferred_element_type=F32)


def _split2(x):
    hi = x.astype(BF16)
    lo = (x - hi.astype(F32)).astype(BF16)
    return hi, lo


def _dot_f32(a, b, dot=_dot):
    ah, al = _split2(a)
    bh, bl = _split2(b)
    return dot(ah, bh) + (dot(ah, bl) + dot(al, bh))


def _inproj0_kernel(x_ref, g_ref, w_ref, ka_ref, va_ref, z_ref, xn_ref):
    j = pl.program_id(1)

    @pl.when(j == 0)
    def _():
        xn_ref[...] = (_rms(x_ref[...]) * g_ref[...]).astype(BF16)

    acc = _dot(xn_ref[...], w_ref[...])

    @pl.when(j == 1)
    def _():
        ka_ref[...] = acc

    @pl.when(j == 2)
    def _():
        va_ref[...] = acc

    @pl.when(jnp.logical_and(j != 1, j != 2))
    def _():
        z_ref[...] = acc


def _inproj0(x, gain, w_bf16, tm):
    m = x.shape[0]
    tn = 1024
    nj = w_bf16.shape[1] // tn
    return pl.pallas_call(
        _inproj0_kernel,
        grid=(m // tm, nj),
        in_specs=[
            pl.BlockSpec((tm, D_MODEL), lambda i, j: (i, 0)),
            pl.BlockSpec((1, D_MODEL), lambda i, j: (0, 0)),
            pl.BlockSpec((D_MODEL, tn), lambda i, j: (0, j)),
        ],
        out_specs=[
            pl.BlockSpec((tm, tn), lambda i, j: (i, 0)),
            pl.BlockSpec((tm, tn), lambda i, j: (i, 0)),
            pl.BlockSpec((tm, tn), lambda i, j: (i, jnp.maximum(j - 2, 0))),
        ],
        out_shape=[
            jax.ShapeDtypeStruct((m, tn), F32),
            jax.ShapeDtypeStruct((m, tn), F32),
            jax.ShapeDtypeStruct((m, 4 * tn), F32),
        ],
        scratch_shapes=[pltpu.VMEM((tm, D_MODEL), BF16)],
        compiler_params=_cparams(("arbitrary", "arbitrary")),
        name="inproj0",
    )(x, gain.reshape(1, D_MODEL), w_bf16)


def _lambda_full(lq1, lk1, lq2, lk2):
    s1 = jnp.sum(lq1 * lk1, axis=-1, keepdims=True)
    s2 = jnp.sum(lq2 * lk2, axis=-1, keepdims=True)
    return jnp.exp(s1) - jnp.exp(s2) + LAMBDA_INIT


def _alibi_slope(h_vec):
    slope = jnp.zeros(h_vec.shape, F32)
    for h in range(HA):
        slope = jnp.where(h_vec == h, 2.0 ** (-(h + 1)), slope)
    return slope


def _subln(o, gain):
    return _rms(o) * gain * (1.0 - LAMBDA_INIT)


def _dattn_prompt_kernel(ii_ref, jj_ref, q_ref, k_ref, v_ref, lq1, lk1, lq2, lk2, gain_ref, o_ref,
                         m1, l1, a1, m2, l2, a2, *, t):
    h = pl.program_id(1)
    s = pl.program_id(2)
    i = ii_ref[s]
    j = jj_ref[s]

    @pl.when(j == 0)
    def _():
        for m_ref, l_ref, a_ref in ((m1, l1, a1), (m2, l2, a2)):
            m_ref[...] = jnp.full(m_ref.shape, -jnp.inf, F32)
            l_ref[...] = jnp.zeros(l_ref.shape, F32)
            a_ref[...] = jnp.zeros(a_ref.shape, F32)

    def step(diagonal):
        slope = _alibi_slope(jnp.full((1, 1), h, jnp.int32))
        col = lax.broadcasted_iota(jnp.int32, (1, t), 1)
        bias = slope * ((j - i) * t + col - (t - 1)).astype(F32)
        q = (q_ref[...] * (DKA ** -0.5)).astype(BF16)
        k = k_ref[...].astype(BF16)
        v = v_ref[...].astype(BF16)
        if diagonal:
            keep = (lax.broadcasted_iota(jnp.int32, (t, t), 1) <= lax.broadcasted_iota(jnp.int32, (t, t), 0))
        for half, (m_ref, l_ref, a_ref) in enumerate(((m1, l1, a1), (m2, l2, a2))):
            sl = slice(half * DKA, (half + 1) * DKA)
            sc = _dot_nt(q[:, sl], k[:, sl]) + bias
            if diagonal:
                sc = jnp.where(keep, sc, -1e30)
            m_prev = m_ref[...]
            m_new = jnp.maximum(m_prev, jnp.max(sc, axis=-1, keepdims=True))
            alpha = jnp.exp(m_prev - m_new)
            p = jnp.exp(sc - jnp.concatenate([m_new] * (t // DVA), axis=-1))
            l_ref[...] = alpha * l_ref[...] + jnp.sum(p, axis=-1, keepdims=True)
            a_ref[...] = alpha * a_ref[...] + _dot(p.astype(BF16), v)
            m_ref[...] = m_new

    @pl.when(j < i)
    def _():
        step(False)

    @pl.when(j == i)
    def _():
        step(True)
        lam = _lambda_full(lq1[...], lk1[...], lq2[...], lk2[...])
        o = a1[...] / l1[...] - lam * (a2[...] / l2[...])
        o_ref[...] = _subln(o, gain_ref[...])


def _dattn_prompt(z, ka, va, lams, subln_gain, batch, seq, t):
    nt = seq // t
    pairs = [(i, j) for i in range(nt) for j in range(i + 1)]
    ii = jnp.asarray([p[0] for p in pairs], jnp.int32)
    jj = jnp.asarray([p[1] for p in pairs], jnp.int32)
    vec = pl.BlockSpec((1, DKA), lambda b, h, s, ii, jj: (0, 0))
    grid_spec = pltpu.PrefetchScalarGridSpec(
        num_scalar_prefetch=2,
        grid=(batch, HA, len(pairs)),
        in_specs=[
            pl.BlockSpec((t, DVA), lambda b, h, s, ii, jj: (b * nt + ii[s], h)),
            pl.BlockSpec((t, DVA), lambda b, h, s, ii, jj: (b * nt + jj[s], h)),
            pl.BlockSpec((t, DVA), lambda b, h, s, ii, jj: (b * nt + jj[s], h)),
            vec, vec, vec, vec,
            pl.BlockSpec((1, DVA), lambda b, h, s, ii, jj: (0, 0)),
        ],
        out_specs=pl.BlockSpec((t, DVA), lambda b, h, s, ii, jj: (b * nt + ii[s], h)),
        scratch_shapes=[pltpu.VMEM((t, DVA), F32)] * 6,
    )
    return pl.pallas_call(
        functools.partial(_dattn_prompt_kernel, t=t),
        grid_spec=grid_spec,
        out_shape=jax.ShapeDtypeStruct((batch * seq, HA * DVA), F32),
        compiler_params=_cparams(("arbitrary", "arbitrary", "arbitrary")),
        name="dattn_prompt",
    )(ii, jj, z, ka, va, *[l.reshape(1, DKA) for l in lams], subln_gain.reshape(1, DVA))


def _rows_by_head(row, nrow):
    head = lax.broadcasted_iota(jnp.int32, (nrow, 1), 0) * HA // nrow
    out = jnp.zeros((nrow, DVA), F32)
    for h in range(HA):
        out = jnp.where(head == h, row[:, h * DVA:(h + 1) * DVA], out)
    return out


def _dattn_sample_kernel(pt_ref, q_ref, kn_ref, vn_ref, *rest, n_pages, page, pps):
    kp_refs = rest[:pps]
    vp_refs = rest[pps:2 * pps]
    lq1, lk1, lq2, lk2, gain_ref, o_ref, qm_ref, m_ref, l_ref, a_ref = rest[2 * pps:]
    p = pl.program_id(1)
    nrow = 2 * HA
    rid = lax.broadcasted_iota(jnp.int32, (nrow, 1), 0)
    slope = _alibi_slope(rid // 2)
    past = n_pages * page

    @pl.when(p == 0)
    def _():
        half = lax.broadcasted_iota(jnp.int32, (nrow, DVA), 1) // DKA
        qm_ref[...] = jnp.where(half == rid % 2, _rows_by_head(q_ref[0], nrow), 0.0)
        m_ref[...] = jnp.full(m_ref.shape, -jnp.inf, F32)
        l_ref[...] = jnp.zeros(l_ref.shape, F32)
        a_ref[...] = jnp.zeros(a_ref.shape, F32)

    qm = qm_ref[...]
    qmb = qm.astype(BF16)
    n = lax.broadcasted_iota(jnp.int32, (1, page * HA), 1)
    own_head = n % HA == rid // 2
    scores = []
    for r in range(pps):
        rel = (past - ((p * pps + r) * page + n // HA)).astype(F32)
        kf = kp_refs[r][...].reshape(page * HA, DVA).astype(BF16)
        sc = _dot_nt(qmb, kf) * (DKA ** -0.5) - slope * rel
        scores.append(jnp.where(own_head, sc, -1e30))
    m_new = m_ref[...]
    for sc in scores:
        m_new = jnp.maximum(m_new, jnp.max(sc, axis=-1, keepdims=True))
    alpha = jnp.exp(m_ref[...] - m_new)
    l_new = alpha * l_ref[...]
    a_new = alpha * a_ref[...]
    for r, sc in enumerate(scores):
        pr = jnp.exp(sc - m_new)
        l_new = l_new + jnp.sum(pr, axis=-1, keepdims=True)
        a_new = a_new + _dot(pr.astype(BF16), vp_refs[r][...].reshape(page * HA, DVA).astype(BF16))
    l_ref[...] = l_new
    a_ref[...] = a_new
    m_ref[...] = m_new

    @pl.when(p == n_pages // pps - 1)
    def _():
        sc_n = jnp.sum(qm * _rows_by_head(kn_ref[0], nrow), axis=-1, keepdims=True) * (DKA ** -0.5)
        m_f = jnp.maximum(m_ref[...], sc_n)
        al = jnp.exp(m_ref[...] - m_f)
        pn = jnp.exp(sc_n - m_f)
        l_f = al * l_ref[...] + pn
        a_f = (al * a_ref[...] + pn * _rows_by_head(vn_ref[0], nrow)) / l_f
        lam = _lambda_full(lq1[...], lk1[...], lq2[...], lk2[...])
        outs = []
        for h in range(HA):
            o = a_f[2 * h:2 * h + 1] - lam * a_f[2 * h + 1:2 * h + 2]
            outs.append(_subln(o, gain_ref[...]))
        o_ref[0] = jnp.concatenate(outs, axis=-1)


def _dattn_sample(z, ka, va, cache_k, cache_v, page_table, lams, subln_gain):
    nb, n_pages = page_table.shape
    page = cache_k.shape[1]
    w = HA * DVA
    row3 = lambda a: a.reshape(nb, 1, a.shape[-1])
    vec = pl.BlockSpec((1, DKA), lambda b, p, pt: (0, 0))
    pps = max(d for d in range(1, PAGES_PER_STEP + 1) if n_pages % d == 0)
    page_specs = [
        pl.BlockSpec((None, page, HA, DVA), lambda b, p, pt, r=r: (pt[b * n_pages + p * pps + r], 0, 0, 0))
        for r in range(pps)]
    grid_spec = pltpu.PrefetchScalarGridSpec(
        num_scalar_prefetch=1,
        grid=(nb, n_pages // pps),
        in_specs=[
            pl.BlockSpec((1, 1, w), lambda b, p, pt: (b, 0, 0)),
            pl.BlockSpec((1, 1, w), lambda b, p, pt: (b, 0, 0)),
            pl.BlockSpec((1, 1, w), lambda b, p, pt: (b, 0, 0)),
            *page_specs, *page_specs,
            vec, vec, vec, vec,
            pl.BlockSpec((1, DVA), lambda b, p, pt: (0, 0)),
        ],
        out_specs=pl.BlockSpec((1, 1, w), lambda b, p, pt: (b, 0, 0)),
        scratch_shapes=[pltpu.VMEM((2 * HA, DVA), F32), pltpu.VMEM((2 * HA, 1), F32),
                        pltpu.VMEM((2 * HA, 1), F32), pltpu.VMEM((2 * HA, DVA), F32)],
    )
    out = pl.pallas_call(
        functools.partial(_dattn_sample_kernel, n_pages=n_pages, page=page, pps=pps),
        grid_spec=grid_spec,
        out_shape=jax.ShapeDtypeStruct((nb, 1, w), F32),
        compiler_params=_cparams(("arbitrary", "arbitrary")),
        name="dattn_sample",
    )(page_table.reshape(-1), row3(z), row3(ka), row3(va), *([cache_k] * pps), *([cache_v] * pps),
      *[l.reshape(1, DKA) for l in lams], subln_gain.reshape(1, DVA))
    return out.reshape(nb, w)


def _ret_log_gamma(h):
    return math.log(1.0 - 2.0 ** (-5.0 - h))


def _ret_theta(width):
    theta = 1.0 / (ROPE_BASE ** jnp.linspace(0.0, 1.0, DKB // 2, dtype=F32))
    return jnp.tile(jnp.repeat(theta, 2), width // DKB).reshape(1, width)


def _rotate_pairs(x, cos, sin):
    n = x.shape[-1]
    lane = lax.broadcasted_iota(jnp.int32, x.shape, x.ndim - 1)
    nxt = pltpu.roll(x, n - 1, axis=x.ndim - 1)
    prv = pltpu.roll(x, 1, axis=x.ndim - 1)
    return x * cos + jnp.where(lane % 2 == 0, -nxt, prv) * sin


def _ret_prompt_kernel(q_ref, k_ref, v_ref, g_ref, th_ref, o_ref, s_ref, dec_ref, *, c):
    ci = pl.program_id(1)
    ri = lax.broadcasted_iota(jnp.int32, (c, c), 0)
    cj = lax.broadcasted_iota(jnp.int32, (c, c), 1)

    @pl.when(jnp.logical_and(pl.program_id(0) == 0, ci == 0))
    def _():
        diff = (ri - cj).astype(F32)
        for h in range(HB):
            dec_ref[h] = jnp.where(diff >= 0, jnp.exp(_ret_log_gamma(h) * jnp.maximum(diff, 0.0)), 0.0)

    @pl.when(ci == 0)
    def _():
        s_ref[...] = jnp.zeros(s_ref.shape, F32)

    idx = lax.broadcasted_iota(jnp.int32, (c, 1), 0)
    pos = (ci * c + idx).astype(F32)
    ang = pos * th_ref[...]
    cos = jnp.concatenate([jnp.cos(ang)] * (HB * DKB // 128), axis=-1)
    sin = jnp.concatenate([jnp.sin(ang)] * (HB * DKB // 128), axis=-1)
    qr = _rotate_pairs(q_ref[...], cos, sin)
    kr = _rotate_pairs(k_ref[...], cos, sin) * (DKB ** -0.5)
    idx_f = idx.astype(F32)
    outs = []
    for h in range(HB):
        lg = _ret_log_gamma(h)
        qh = qr[:, h * DKB:(h + 1) * DKB].astype(BF16)
        kh = kr[:, h * DKB:(h + 1) * DKB]
        vh = v_ref[:, h * DVB:(h + 1) * DVB].astype(BF16)
        s_old = s_ref[0, h]
        scores = _dot_nt(qh, kh.astype(BF16)) * dec_ref[h]
        o = _dot(scores.astype(BF16), vh)
        o = o + _dot(qh, s_old.astype(BF16)) * jnp.exp(lg * (idx_f + 1.0))
        kd = (kh * jnp.exp(lg * (c - 1.0 - idx_f))).astype(BF16)
        s_ref[0, h] = math.exp(lg * c) * s_old + _dot_tn(kd, vh)
        gate = g_ref[:, h * DVB:(h + 1) * DVB]
        outs.append(_rms(o) * (gate * _sigmoid(gate)))
    o_ref[...] = jnp.concatenate(outs, axis=-1)


def _ret_prompt(z, batch, seq, c):
    nc = seq // c
    wq = HB * DKB
    wv = HB * DVB
    return pl.pallas_call(
        functools.partial(_ret_prompt_kernel, c=c),
        grid=(batch, nc),
        in_specs=[
            pl.BlockSpec((c, wq), lambda b, i: (b * nc + i, 2)),
            pl.BlockSpec((c, wq), lambda b, i: (b * nc + i, 3)),
            pl.BlockSpec((c, wv), lambda b, i: (b * nc + i, 2)),
            pl.BlockSpec((c, wv), lambda b, i: (b * nc + i, 3)),
            pl.BlockSpec((1, 128), lambda b, i: (0, 0)),
        ],
        out_specs=[
            pl.BlockSpec((c, wv), lambda b, i: (b * nc + i, 0)),
            pl.BlockSpec((1, HB, DKB, DVB), lambda b, i: (b, 0, 0, 0)),
        ],
        out_shape=[
            jax.ShapeDtypeStruct((batch * seq, wv), F32),
            jax.ShapeDtypeStruct((batch, HB, DKB, DVB), F32),
        ],
        scratch_shapes=[pltpu.VMEM((HB, c, c), F32)],
        compiler_params=_cparams(("arbitrary", "arbitrary")),
        name="ret_prompt",
    )(z, z, z, z, _ret_theta(128))


def _ret_sample_kernel(qk_ref, v_ref, g_ref, th_ref, s_ref, o_ref, so_ref, *, pos):
    ang = float(pos) * th_ref[...]
    x = qk_ref[...]
    ncol = x.shape[1]
    d = lax.broadcasted_iota(jnp.int32, x.shape, 0)
    partner = jnp.where(d % 2 == 0, -pltpu.roll(x, DKB - 1, axis=0), pltpu.roll(x, 1, axis=0))
    qk = x * jnp.cos(ang) + partner * jnp.sin(ang)
    is_k = lax.broadcasted_iota(jnp.int32, (1, ncol), 1) % (2 * HB) >= HB
    qkb = (qk * jnp.where(is_k, DKB ** -0.5, 1.0)).astype(BF16)
    sel_row = lax.broadcasted_iota(jnp.int32, (ncol, HB * DVB), 0)
    sel_head = lax.broadcasted_iota(jnp.int32, (ncol, HB * DVB), 1) // DVB
    rows = []
    for s in range(v_ref.shape[0]):
        q_sel = jnp.where(sel_row == s * 2 * HB + sel_head, 1.0, 0.0)
        kv_sel = jnp.where(sel_row == s * 2 * HB + HB + sel_head, v_ref[s:s + 1, :], 0.0)
        res = _dot(qkb, jnp.concatenate([q_sel, kv_sel], axis=-1).astype(BF16))
        outs = []
        for h in range(HB):
            q_lanes = res[:, h * DVB:(h + 1) * DVB]
            outer = res[:, (HB + h) * DVB:(HB + h + 1) * DVB]
            s_new = math.exp(_ret_log_gamma(h)) * s_ref[s, h] + outer
            so_ref[s, h] = s_new
            o = jnp.sum(q_lanes * s_new, axis=0, keepdims=True)
            gate = g_ref[s:s + 1, h * DVB:(h + 1) * DVB]
            outs.append(_rms(o) * (gate * _sigmoid(gate)))
        rows.append(jnp.concatenate(outs, axis=-1))
    o_ref[...] = jnp.concatenate(rows, axis=0)


def _ret_sample(z, state, pos):
    nb = z.shape[0]
    wq = HB * DKB
    wv = HB * DVB
    sb = 8 if nb % 8 == 0 else nb
    qk_t = jnp.transpose(z[:, wv:wv + 2 * wq].reshape(nb, 2 * HB, DKB), (2, 0, 1)).reshape(DKB, nb * 2 * HB)
    theta_col = _ret_theta(DKB).reshape(DKB, 1)
    return pl.pallas_call(
        functools.partial(_ret_sample_kernel, pos=pos),
        grid=(nb // sb,),
        in_specs=[
            pl.BlockSpec((DKB, sb * 2 * HB), lambda b: (0, b)),
            pl.BlockSpec((sb, wv), lambda b: (b, 2)),
            pl.BlockSpec((sb, wv), lambda b: (b, 3)),
            pl.BlockSpec((DKB, 1), lambda b: (0, 0)),
            pl.BlockSpec((sb, HB, DKB, DVB), lambda b: (b, 0, 0, 0)),
        ],
        out_specs=[
            pl.BlockSpec((sb, wv), lambda b: (b, 0)),
            pl.BlockSpec((sb, HB, DKB, DVB), lambda b: (b, 0, 0, 0)),
        ],
        out_shape=[
            jax.ShapeDtypeStruct((nb, wv), F32),
            jax.ShapeDtypeStruct(state.shape, F32),
        ],
        compiler_params=_cparams(("arbitrary",)),
        name="ret_sample",
    )(qk_t, z, z, theta_col, state)


def _mm_resid_kernel(*refs, n_in):
    a_refs = refs[:n_in]
    w_refs = refs[n_in:2 * n_in]
    x_ref = refs[2 * n_in]
    o_ref = refs[2 * n_in + 1]
    acc = x_ref[...]
    for a_ref, w_ref in zip(a_refs, w_refs):
        acc = acc + _dot(a_ref[...].astype(BF16), w_ref[...])
    o_ref[...] = acc


def _mm_resid(a_list, w_bf16, resid, tm):
    m, n = resid.shape
    tn = n
    in_specs = []
    for a in a_list:
        in_specs.append(pl.BlockSpec((tm, a.shape[1]), lambda i, j: (i, 0)))
    row = 0
    for a in a_list:
        kb = a.shape[1]
        in_specs.append(pl.BlockSpec((kb, tn), lambda i, j, r=row // kb: (r, j)))
        row += kb
    in_specs.append(pl.BlockSpec((tm, tn), lambda i, j: (i, j)))
    return pl.pallas_call(
        functools.partial(_mm_resid_kernel, n_in=len(a_list)),
        grid=(m // tm, n // tn),
        in_specs=in_specs,
        out_specs=pl.BlockSpec((tm, tn), lambda i, j: (i, j)),
        out_shape=jax.ShapeDtypeStruct((m, n), F32),
        compiler_params=_cparams(("arbitrary", "arbitrary")),
        name="mm_resid",
    )(*a_list, *([w_bf16] * len(a_list)), resid)


def _ffn_kernel(x_ref, g_ref, wg_ref, wu_ref, wd_ref, fg_ref, o_ref, xn_ref, acc_ref, *, final_norm):
    j = pl.program_id(1)

    @pl.when(j == 0)
    def _():
        xn_ref[...] = (_rms(x_ref[...]) * g_ref[...]).astype(BF16)
        acc_ref[...] = jnp.zeros(acc_ref.shape, F32)

    xn = xn_ref[...]
    gate = _dot(xn, wg_ref[...])
    up = _dot(xn, wu_ref[...])
    hid = (gate * _sigmoid(gate) * up).astype(BF16)
    acc_ref[...] += _dot(hid, wd_ref[...])

    @pl.when(j == pl.num_programs(1) - 1)
    def _():
        y = x_ref[...] + acc_ref[...]
        if final_norm:
            y = _rms(y) * fg_ref[...]
        o_ref[...] = y


def _ffn(x, gain, wg, wu, wd, final_gain, final_norm, tm, tf=512):
    m = x.shape[0]
    dff = wg.shape[1]
    return pl.pallas_call(
        functools.partial(_ffn_kernel, final_norm=final_norm),
        grid=(m // tm, dff // tf),
        in_specs=[
            pl.BlockSpec((tm, D_MODEL), lambda i, j: (i, 0)),
            pl.BlockSpec((1, D_MODEL), lambda i, j: (0, 0)),
            pl.BlockSpec((D_MODEL, tf), lambda i, j: (0, j)),
            pl.BlockSpec((D_MODEL, tf), lambda i, j: (0, j)),
            pl.BlockSpec((tf, D_MODEL), lambda i, j: (j, 0)),
            pl.BlockSpec((1, D_MODEL), lambda i, j: (0, 0)),
        ],
        out_specs=pl.BlockSpec((tm, D_MODEL), lambda i, j: (i, 0)),
        out_shape=jax.ShapeDtypeStruct((m, D_MODEL), F32),
        scratch_shapes=[pltpu.VMEM((tm, D_MODEL), BF16), pltpu.VMEM((tm, D_MODEL), F32)],
        compiler_params=_cparams(("arbitrary", "arbitrary")),
        name="ffn",
    )(x, gain.reshape(1, D_MODEL), wg, wu, wd, final_gain.reshape(1, D_MODEL))


def _norm_and_shift(x_ref, g_ref, prev_ref, carry_ref, i, tiles_per_seq):
    h = _rms(x_ref[...]) * g_ref[...]
    if prev_ref is None:
        tm = h.shape[0]
        first = (i % tiles_per_seq) == 0
        prev_row = jnp.where(first, 0.0, carry_ref[0:1, :])
        rows = lax.broadcasted_iota(jnp.int32, (tm, 1), 0)
        shifted = jnp.where(rows == 0, prev_row, pltpu.roll(h, 1, axis=0))
        carry_ref[0:1, :] = h[tm - 1:tm, :]
    else:
        shifted = prev_ref[...]
    return h, shifted - h


def _rwkv_rkv_kernel(*refs, per_row_prev, tiles_per_seq):
    if per_row_prev:
        x_ref, g_ref, prev_ref, mu_ref, w_ref, o_ref, hl_ref, h_ref, xx_ref, carry_ref = refs
    else:
        x_ref, g_ref, mu_ref, w_ref, o_ref, hl_ref, h_ref, xx_ref, carry_ref = refs
        prev_ref = None
    i = pl.program_id(0)
    j = pl.program_id(1)

    @pl.when(j == 0)
    def _():
        h, xx = _norm_and_shift(x_ref, g_ref, prev_ref, carry_ref, i, tiles_per_seq)
        h_ref[...] = h
        xx_ref[...] = xx
        if per_row_prev:
            hl_ref[...] = h
        else:
            hl_ref[0] = h[h.shape[0] - 1:, :]

    xm = (h_ref[...] + xx_ref[...] * mu_ref[0]).astype(BF16)
    o_ref[...] = _dot(xm, w_ref[...])


def _rwkv_rkv(x, gain, prev, mu_rkv, w_rkv, tm, tiles_per_seq):
    m = x.shape[0]
    tn = 1024
    per_seq = D_MODEL // tn
    per_row_prev = prev is not None
    in_specs = [pl.BlockSpec((tm, D_MODEL), lambda i, j: (i, 0)),
                pl.BlockSpec((1, D_MODEL), lambda i, j: (0, 0))]
    args = [x, gain.reshape(1, D_MODEL)]
    if per_row_prev:
        in_specs.append(pl.BlockSpec((tm, D_MODEL), lambda i, j: (i, 0)))
        args.append(prev)
        hl_spec = pl.BlockSpec((tm, D_MODEL), lambda i, j: (i, 0))
        hl_shape = jax.ShapeDtypeStruct((m, D_MODEL), F32)
    else:
        nseq = m // (tm * tiles_per_seq)
        hl_spec = pl.BlockSpec((1, 1, D_MODEL), lambda i, j: (i // tiles_per_seq, 0, 0))
        hl_shape = jax.ShapeDtypeStruct((nseq, 1, D_MODEL), F32)
    in_specs += [pl.BlockSpec((1, 1, D_MODEL), lambda i, j: (j // per_seq, 0, 0)),
                 pl.BlockSpec((D_MODEL, tn), lambda i, j: (0, j))]
    args += [mu_rkv, w_rkv]
    return pl.pallas_call(
        functools.partial(_rwkv_rkv_kernel, per_row_prev=per_row_prev, tiles_per_seq=tiles_per_seq),
        grid=(m // tm, 3 * per_seq),
        in_specs=in_specs,
        out_specs=[pl.BlockSpec((tm, tn), lambda i, j: (i, j)), hl_spec],
        out_shape=[jax.ShapeDtypeStruct((m, 3 * D_MODEL), F32), hl_shape],
        scratch_shapes=[pltpu.VMEM((tm, D_MODEL), F32), pltpu.VMEM((tm, D_MODEL), F32),
                        pltpu.VMEM((8, D_MODEL), F32)],
        compiler_params=_cparams(("arbitrary", "arbitrary")),
        name="rwkv_rkv",
    )(*args)


def _rwkv_lora_kernel(*refs, per_row_prev, tiles_per_seq):
    if per_row_prev:
        (x_ref, g_ref, prev_ref, mu_ref, w0_ref, w1_ref, w2_ref, a0_ref, a1_ref, a2_ref, g1_ref, g2_ref,
         lw_ref, a_ref, go_ref, carry_ref) = refs
    else:
        (x_ref, g_ref, mu_ref, w0_ref, w1_ref, w2_ref, a0_ref, a1_ref, a2_ref, g1_ref, g2_ref,
         lw_ref, a_ref, go_ref, carry_ref) = refs
        prev_ref = None
    h, xx = _norm_and_shift(x_ref, g_ref, prev_ref, carry_ref, pl.program_id(0), tiles_per_seq)
    xw = (h + xx * mu_ref[0]).astype(BF16)
    xa = (h + xx * mu_ref[1]).astype(BF16)
    xg = (h + xx * mu_ref[2]).astype(BF16)
    zw = w0_ref[...] + _dot(jnp.tanh(_dot(xw, w1_ref[...])).astype(BF16), w2_ref[...])
    lw_ref[...] = -math.exp(-0.5) * _sigmoid(zw)
    a_ref[...] = _sigmoid(a0_ref[...] + _dot(_dot(xa, a1_ref[...]).astype(BF16), a2_ref[...]))
    go_ref[...] = _dot(_sigmoid(_dot(xg, g1_ref[...])).astype(BF16), g2_ref[...])


def _rwkv_lora(x, gain, prev, mu_wag, w0, w1, w2, a0, a1, a2, g1, g2, tm, tiles_per_seq):
    m = x.shape[0]
    per_row_prev = prev is not None
    full = lambda a: pl.BlockSpec(a.shape, lambda i: (0,) * a.ndim)
    row = pl.BlockSpec((tm, D_MODEL), lambda i: (i, 0))
    args = [x, gain.reshape(1, D_MODEL)]
    in_specs = [row, full(args[1])]
    if per_row_prev:
        in_specs.append(row)
        args.append(prev)
    rest = [mu_wag, w0.reshape(1, D_MODEL), w1, w2, a0.reshape(1, D_MODEL), a1, a2, g1, g2]
    in_specs += [full(a) for a in rest]
    args += rest
    return pl.pallas_call(
        functools.partial(_rwkv_lora_kernel, per_row_prev=per_row_prev, tiles_per_seq=tiles_per_seq),
        grid=(m // tm,),
        in_specs=in_specs,
        out_specs=[row, row, row],
        out_shape=[jax.ShapeDtypeStruct((m, D_MODEL), F32)] * 3,
        scratch_shapes=[pltpu.VMEM((8, D_MODEL), F32)],
        compiler_params=_cparams(("arbitrary",)),
        name="rwkv_lora",
    )(*args)


def _wkv_keys(k, a, kk_gain, ka_gain, head_sum):
    kk = k * kk_gain
    kk = kk / jnp.maximum(jnp.sqrt(head_sum(kk * kk)), 1e-12)
    return kk, kk * a, k * (1.0 + (a - 1.0) * ka_gain)


def _head_norm(o, head_sum):
    mean = head_sum(o) * (1.0 / NC)
    var = head_sum(jnp.square(o - mean)) * (1.0 / NC)
    return (o - mean) * lax.rsqrt(var + GN_EPS)


def _wkv_finish(on, r, k2, v, g, rk, ln_g, ln_b, head_sum):
    bonus = head_sum(r * k2 * rk) * v
    return (on * ln_g + ln_b + bonus) * g


def _pair_sum(x):
    lane = lax.broadcasted_iota(jnp.int32, x.shape, 1)
    s0 = jnp.sum(x[:, :NC], axis=-1, keepdims=True)
    s1 = jnp.sum(x[:, NC:], axis=-1, keepdims=True)
    return jnp.where(lane < NC, s0, s1)


def _bmm(a, b):
    return jnp.einsum("cij,cjk->cik", a, b, preferred_element_type=F32)


def _bmm_nt(a, b):
    return jnp.einsum("cik,cjk->cij", a, b, preferred_element_type=F32)


def _wkv_prompt_kernel(r_ref, k_ref, v_ref, lw_ref, a_ref, g_ref, kkg_ref, kag_ref, rk_ref, lng_ref, lnb_ref,
                       y_ref, so_ref, st_ref, *, tc):
    c = WKV_CHUNK
    nch = tc // c
    ti = pl.program_id(2)

    @pl.when(ti == 0)
    def _():
        st_ref[...] = jnp.zeros(st_ref.shape, F32)

    ri = lax.broadcasted_iota(jnp.int32, (c, c), 0)
    cj = lax.broadcasted_iota(jnp.int32, (c, c), 1)
    tri = (ri >= cj).astype(BF16)
    gi = lax.broadcasted_iota(jnp.int32, (2 * c, 2 * c), 0)
    gj = lax.broadcasted_iota(jnp.int32, (2 * c, 2 * c), 1) % c
    gmask = (gj < jnp.where(gi < c, gi, gi - c + 1))[None]

    r = r_ref[...]
    v = v_ref[...]
    kk, bv, k2 = _wkv_keys(k_ref[...], a_ref[...], kkg_ref[...], kag_ref[...], _pair_sum)
    to3 = lambda x: x.reshape(nch, c, 2 * NC)
    lw3 = to3(lw_ref[...])
    p0 = lw3.astype(BF16)
    r1 = lw3 - p0.astype(F32)
    p1 = r1.astype(BF16)
    p2 = (r1 - p1.astype(F32)).astype(BF16)
    trib = jnp.broadcast_to(tri[None], (nch, c, c))
    cum = _bmm(trib, p0) + (_bmm(trib, p1) + _bmm(trib, p2))
    cum_last = cum[:, c - 1:c, :]
    w_rest = jnp.exp(cum_last - cum)
    w_inv = jnp.exp(-cum)
    am = -to3(kk) * jnp.exp(cum - lw3)
    rm = to3(r) * jnp.exp(cum)
    bm = (to3(bv) * w_inv).astype(BF16)
    km = (to3(k2) * w_inv).astype(BF16)
    bt = (to3(bv) * w_rest).astype(BF16)
    kt = (to3(k2) * w_rest).astype(BF16)
    v3 = to3(v)
    w_end = jnp.exp(cum_last)

    heads = lambda x: jnp.concatenate([x[:, :, :NC], x[:, :, NC:]], axis=0)
    am_h, rm_h, v_h = heads(am), heads(rm), heads(v3)
    x1 = jnp.concatenate([am_h, rm_h], axis=1).astype(BF16)
    x2 = jnp.concatenate([heads(bm), heads(km)], axis=1)
    g4 = jnp.where(gmask, _bmm_nt(x1, x2), 0.0)
    lv = _bmm(g4[:, :, c:].astype(BF16), v_h.astype(BF16))
    x = jnp.concatenate([am_h, lv[:, :c]], axis=-1)
    lp = g4[:, :c, :c].astype(BF16)
    steps = int(math.log2(c))
    for s in range(steps - 1):
        prod = _bmm(lp, jnp.concatenate([x.astype(BF16), lp], axis=-1))
        x = x + prod[:, :, :2 * NC]
        lp = prod[:, :, 2 * NC:].astype(BF16)
    x = x + _bmm(lp, x.astype(BF16))
    gh =_bmm(g4[:, c:, :c].astype(BF16), x.astype(BF16)) + jnp.concatenate([rm_h, lv[:, c:]], axis=-1)
    btk = jnp.concatenate([heads(bt), heads(kt)], axis=1)
    zv = jnp.concatenate([jnp.zeros_like(v_h), v_h], axis=-1)
    pqv = jnp.concatenate([x, zv], axis=1).astype(BF16)
    eft = [_dot_tn(pqv[n], btk[n]) for n in range(2 * nch)]
    gb = gh[:, :, :NC].astype(BF16)
    ss = [st_ref[0], st_ref[1]]
    o_chunks = [[], []]
    lane_sum = lambda t: jnp.sum(t, axis=-1, keepdims=True)
    for ch in range(nch):
        for hh in range(2):
            n = hh * nch + ch
            sb = ss[hh].astype(BF16)
            o_chunks[hh].append(_head_norm(_dot_nt(gb[n], sb) + gh[n][:, NC:], lane_sum))
            ss[hh] = (ss[hh] * w_end[ch][:, hh * NC:(hh + 1) * NC]
                      + _dot(sb, eft[n][:NC].astype(BF16)) + eft[n][NC:])
    st_ref[0] = ss[0]
    st_ref[1] = ss[1]

    @pl.when(ti == pl.num_programs(2) - 1)
    def _():
        so_ref[0, 0] = ss[0]
        so_ref[0, 1] = ss[1]

    on = jnp.concatenate([jnp.concatenate(oc, axis=0) for oc in o_chunks], axis=-1)
    y_ref[...] = _wkv_finish(on, r, k2, v, g_ref[...], rk_ref[...], lng_ref[...], lnb_ref[...], _pair_sum)


def _wkv_prompt(rkv, lw, a, g, kk_gain, ka_gain, rk, ln_g, ln_b, batch, seq, tc):
    nt = seq // tc
    nhp = HC // 2
    tok = lambda off: pl.BlockSpec((tc, 2 * NC), lambda b, hp, t: (b * nt + t, off + hp))
    par = pl.BlockSpec((1, 2 * NC), lambda b, hp, t: (0, hp))
    return pl.pallas_call(
        functools.partial(_wkv_prompt_kernel, tc=tc),
        grid=(batch, nhp, nt),
        in_specs=[tok(0), tok(nhp), tok(2 * nhp), tok(0), tok(0), tok(0), par, par, par, par, par],
        out_specs=[
            pl.BlockSpec((tc, 2 * NC), lambda b, hp, t: (b * nt + t, hp)),
            pl.BlockSpec((1, 2, NC, NC), lambda b, hp, t: (b, hp, 0, 0)),
        ],
        out_shape=[
            jax.ShapeDtypeStruct((batch * seq, D_MODEL), F32),
            jax.ShapeDtypeStruct((batch, HC, NC, NC), F32),
        ],
        scratch_shapes=[pltpu.VMEM((2, NC, NC), F32)],
        compiler_params=_cparams(("arbitrary", "arbitrary", "arbitrary")),
        name="wkv_prompt",
    )(rkv, rkv, rkv, lw, a, g, kk_gain, ka_gain, rk, ln_g, ln_b)


def _wkv_sample_kernel(r_ref, k_ref, v_ref, lw_ref, a_ref, g_ref, kkg_ref, kag_ref, rk_ref, lng_ref, lnb_ref,
                       s_ref, y_ref, so_ref, o_ref):
    ys = []
    col_sum = lambda x: jnp.sum(x, axis=0, keepdims=True)
    for hh in range(2):
        sl = slice(hh * NC, (hh + 1) * NC)
        t = lambda ref: ref[...].T[sl, :]
        r, k, v, a, g = t(r_ref), t(k_ref), t(v_ref), t(a_ref), t(g_ref)
        w = jnp.exp(t(lw_ref))
        kk, bv, k2 = _wkv_keys(k, a, kkg_ref[sl, :], kag_ref[sl, :], col_sum)
        for i in range(NC):
            s_old = s_ref[hh, i]
            sk = jnp.sum(s_old * kk, axis=0, keepdims=True)
            s_new = s_old * w - sk * bv + v[i:i + 1, :] * k2
            so_ref[hh, i] = s_new
            o_ref[i:i + 1, :] = jnp.sum(s_new * r, axis=0, keepdims=True)
        ys.append(_wkv_finish(_head_norm(o_ref[...], col_sum), r, k2, v, g,rk_ref[sl, :], lng_ref[sl, :], lnb_ref[sl, :], col_sum))
    y_ref[...] = jnp.concatenate(ys, axis=0).T


def _wkv_sample(rkv, lw, a, g, kk_gain, ka_gain, rk, ln_g, ln_b, state_t):
    nb = lw.shape[0]
    nhp = HC // 2
    tok = lambda off: pl.BlockSpec((nb, 2 * NC), lambda hp: (0, off + hp))
    par = pl.BlockSpec((2 * NC, 1), lambda hp: (hp, 0))
    col = lambda p: p.reshape(D_MODEL, 1)
    st_spec = pl.BlockSpec((2, NC, NC, nb), lambda hp: (hp, 0, 0, 0))
    return pl.pallas_call(
        _wkv_sample_kernel,
        grid=(nhp,),
        in_specs=[tok(0), tok(nhp), tok(2 * nhp), tok(0), tok(0), tok(0), par, par, par, par, par, st_spec],
        out_specs=[pl.BlockSpec((nb, 2 * NC), lambda hp: (0, hp)), st_spec],
        out_shape=[
            jax.ShapeDtypeStruct((nb, D_MODEL), F32),
            jax.ShapeDtypeStruct(state_t.shape, F32),
        ],
        scratch_shapes=[pltpu.VMEM((NC, nb), F32)],
        compiler_params=_cparams(("arbitrary",)),
        name="wkv_sample",
    )(rkv, rkv, rkv, lw, a, g, col(kk_gain), col(ka_gain), col(rk), col(ln_g), col(ln_b), state_t)


def _row_tile(m, want):
    return min(m, want)


def _trunk(x, batch, seq, past, W):
    m = x.shape[0]
    prompt = past is None
    tm = _row_tile(m, 512)
    lams = (W["lambda_q1"], W["lambda_k1"], W["lambda_q2"], W["lambda_k2"])

    ka, va, z = _inproj0(x, W["norm0_mix"], W["w_in0"], tm)
    if prompt:
        o_a = _dattn_prompt(z, ka, va, lams, W["subln_gain"], batch, seq, min(seq, 512))
        o_b, ret_new = _ret_prompt(z, batch, seq, min(seq, 256))
    else:
        o_a = _dattn_sample(z, ka, va, past["cache_k"], past["cache_v"], past["page_table"], lams,
                            W["subln_gain"])
        o_b, ret_new = _ret_sample(z, past["state_ret"], past["pos"])
    x = _mm_resid([o_a, o_b], W["w_out0"], x, _row_tile(m, 256))
    x = _ffn(x, W["norm0_ffn"], W["w_gate0"], W["w_up0"], W["w_down0"], W["norm_final"], False, tm)

    tiles_per_seq = max(seq // tm, 1)
    prev = None if prompt else past["state_shift"]
    rkv, hlast = _rwkv_rkv(x, W["norm1_mix"], prev, W["mu_rkv"], W["w_rkv"], tm, tiles_per_seq)
    tl = _row_tile(m, 256)
    lw, a, g = _rwkv_lora(x, W["norm1_mix"], prev, W["mu_wag"], W["decay_w0"], W["decay_w1"], W["decay_w2"],
                          W["aaa_a0"], W["aaa_a1"], W["aaa_a2"], W["gate_g1"], W["gate_g2"],
                          tl, max(seq // tl, 1))
    head_params = (W["k_k"], W["k_a"], W["r_k"], W["lnx_gain"], W["lnx_bias"])
    if prompt:
        y, wkv_new = _wkv_prompt(rkv, lw, a, g, *head_params, batch, seq, min(seq, 512))
        shift_new = hlast.reshape(batch, D_MODEL)
    else:
        state_t = jnp.transpose(past["state_wkv"], (1, 2, 3, 0))
        y, wkv_t = _wkv_sample(rkv, lw, a, g, *head_params, state_t)
        wkv_new = jnp.transpose(wkv_t, (3, 0, 1, 2))
        shift_new = hlast
    x = _mm_resid([y], W["w_out1"], x, _row_tile(m, 256))
    x = _ffn(x, W["norm1_ffn"], W["w_gate1"], W["w_up1"], W["w_down1"], W["norm_final"], True, tm)
    return x, ka, va, ret_new, wkv_new, shift_new


def kernel(x_prompt, x_sample, cache_k, cache_v, page_table, state_ret, state_wkv, state_shift, norm0_mix, w_in0, lambda_q1, lambda_k1, lambda_q2, lambda_k2, subln_gain, w_out0, norm0_ffn, w_gate0, w_up0, w_down0, norm1_mix, mu1, w_r1, w_k1, w_v1, decay_w0, decay_w1, decay_w2, aaa_a0, aaa_a1, aaa_a2, gate_g1, gate_g2, k_k, k_a, r_k, lnx_gain, lnx_bias, w_out1, norm1_ffn, w_gate1, w_up1, w_down1, norm_final):
    bf = lambda w: w.astype(BF16)
    row = lambda p: p.reshape(1, D_MODEL)
    W = dict(
        norm0_mix=norm0_mix, w_in0=bf(w_in0), lambda_q1=lambda_q1, lambda_k1=lambda_k1, lambda_q2=lambda_q2,
        lambda_k2=lambda_k2, subln_gain=subln_gain, w_out0=bf(w_out0), norm0_ffn=norm0_ffn,
        w_gate0=bf(w_gate0), w_up0=bf(w_up0), w_down0=bf(w_down0),
        norm1_mix=norm1_mix,
        mu_rkv=jnp.stack([mu1[0], mu1[2], mu1[3]]).reshape(3, 1, D_MODEL),
        mu_wag=jnp.stack([mu1[1], mu1[4], mu1[5]]).reshape(3, 1, D_MODEL),
        w_rkv=jnp.concatenate([bf(w_r1), bf(w_k1), bf(w_v1)], axis=1),
        decay_w0=decay_w0, decay_w1=bf(decay_w1), decay_w2=bf(decay_w2),
        aaa_a0=aaa_a0, aaa_a1=bf(aaa_a1), aaa_a2=bf(aaa_a2), gate_g1=bf(gate_g1), gate_g2=bf(gate_g2),
        k_k=row(k_k), k_a=row(k_a), r_k=row(r_k), lnx_gain=row(lnx_gain), lnx_bias=row(lnx_bias),
        w_out1=bf(w_out1), norm1_ffn=norm1_ffn, w_gate1=bf(w_gate1), w_up1=bf(w_up1), w_down1=bf(w_down1),
        norm_final=norm_final,
    )
    bsz, seq = x_prompt.shape[:2]
    y_p, k_p, v_p, ret_p, wkv_p, shift_p = _trunk(x_prompt.reshape(bsz * seq, D_MODEL), bsz, seq, None, W)
    nb, dseq = x_sample.shape[:2]
    past_len = page_table.shape[1] * cache_k.shape[1]
    past = dict(cache_k=cache_k, cache_v=cache_v, page_table=page_table, state_ret=state_ret,
                state_wkv=state_wkv, state_shift=state_shift, pos=past_len)
    y_s, k_s, v_s, ret_s, wkv_s, shift_s = _trunk(x_sample.reshape(nb * dseq, D_MODEL), nb, dseq, past, W)
    return (y_p.reshape(bsz, seq, D_MODEL), y_s.reshape(nb, dseq, D_MODEL),
            k_p.reshape(bsz, seq, HA, 2 * DKA), v_p.reshape(bsz, seq, HA, DVA),
            k_s.reshape(nb, dseq, HA, 2 * DKA), v_s.reshape(nb, dseq, HA, DVA),
            ret_p, ret_s, wkv_p, wkv_s, shift_p, shift_s)
```

```python
import functools
import math

import jax
import jax.numpy as jnp
from jax import lax
from jax.experimental import pallas as pl
from jax.experimental.pallas import tpu as pltpu

F32 = jnp.float32
BF16 = jnp.bfloat16

D_MODEL = 2048
EPS = 1e-6
DKA = 64
DVA = 128
HA = 8
LAMBDA_INIT = 0.8 - 0.6 * math.exp(-0.3 * 0)
DKB = 64
DVB = 128
HB = 8
ROPE_BASE = 10000.0
NC = 64
HC = D_MODEL // NC
GN_EPS = 64e-5
WKV_CHUNK = 64
PAGES_PER_STEP = 8
VMEM_LIMIT = 48 * 1024 * 1024


def _cparams(sem):
    return pltpu.CompilerParams(dimension_semantics=sem, vmem_limit_bytes=VMEM_LIMIT)


def _rms(x, eps=EPS):
    return x * lax.rsqrt(jnp.mean(x * x, axis=-1, keepdims=True) + eps)


def _sigmoid(x):
    return 1.0 / (1.0 + jnp.exp(-x))


def _dot(a, b):
    return jnp.dot(a, b, preferred_element_type=F32)


def _dot_nt(a, b):
    return lax.dot_general(a, b, (((1,), (1,)), ((), ())), preferred_element_type=F32)


def _dot_tn(a, b):
    return lax.dot_general(a, b, (((0,), (0,)), ((), ())), preferred_element_type=F32)


def _split2(x):
    hi = x.astype(BF16)
    lo = (x - hi.astype(F32)).astype(BF16)
    return hi, lo


def _dot_f32(a, b, dot=_dot):
    ah, al = _split2(a)
    bh, bl = _split2(b)
    return dot(ah, bh) + (dot(ah, bl) + dot(al, bh))


def _inproj0_kernel(x_ref, g_ref, w_ref, ka_ref, va_ref, z_ref, xn_ref):
    j = pl.program_id(1)

    @pl.when(j == 0)
    def _():
        xn_ref[...] = (_rms(x_ref[...]) * g_ref[...]).astype(BF16)

    acc = _dot(xn_ref[...], w_ref[...])

    @pl.when(j == 1)
    def _():
        ka_ref[...] = acc

    @pl.when(j == 2)
    def _():
        va_ref[...] = acc

    @pl.when(jnp.logical_and(j != 1, j != 2))
    def _():
        z_ref[...] = acc


def _inproj0(x, gain, w_bf16, tm):
    m = x.shape[0]
    tn = 1024
    nj = w_bf16.shape[1] // tn
    return pl.pallas_call(
        _inproj0_kernel,
        grid=(m // tm, nj),
        in_specs=[
            pl.BlockSpec((tm, D_MODEL), lambda i, j: (i, 0)),
            pl.BlockSpec((1, D_MODEL), lambda i, j: (0, 0)),
            pl.BlockSpec((D_MODEL, tn), lambda i, j: (0, j)),
        ],
        out_specs=[
            pl.BlockSpec((tm, tn), lambda i, j: (i, 0)),
            pl.BlockSpec((tm, tn), lambda i, j: (i, 0)),
            pl.BlockSpec((tm, tn), lambda i, j: (i, jnp.maximum(j - 2, 0))),
        ],
        out_shape=[
            jax.ShapeDtypeStruct((m, tn), F32),
            jax.ShapeDtypeStruct((m, tn), F32),
            jax.ShapeDtypeStruct((m, 4 * tn), F32),
        ],
        scratch_shapes=[pltpu.VMEM((tm, D_MODEL), BF16)],
        compiler_params=_cparams(("arbitrary", "arbitrary")),
        name="inproj0",
    )(x, gain.reshape(1, D_MODEL), w_bf16)


def _lambda_full(lq1, lk1, lq2, lk2):
    s1 = jnp.sum(lq1 * lk1, axis=-1, keepdims=True)
    s2 = jnp.sum(lq2 * lk2, axis=-1, keepdims=True)
    return jnp.exp(s1) - jnp.exp(s2) + LAMBDA_INIT


def _alibi_slope(h_vec):
    slope = jnp.zeros(h_vec.shape, F32)
    for h in range(HA):
        slope = jnp.where(h_vec == h, 2.0 ** (-(h + 1)), slope)
    return slope


def _subln(o, gain):
    return _rms(o) * gain * (1.0 - LAMBDA_INIT)


def _dattn_prompt_kernel(ii_ref, jj_ref, q_ref, k_ref, v_ref, lq1, lk1, lq2, lk2, gain_ref, o_ref,
                         m_ref, l_ref, a_ref, *, t, hps):
    hg = pl.program_id(1)
    s = pl.program_id(2)
    i = ii_ref[s]
    j = jj_ref[s]

    @pl.when(j == 0)
    def _():
        m_ref[...] = jnp.full(m_ref.shape, -jnp.inf, F32)
        l_ref[...] = jnp.zeros(l_ref.shape, F32)
        a_ref[...] = jnp.zeros(a_ref.shape, F32)

    def step(diagonal):
        col = lax.broadcasted_iota(jnp.int32, (1, t), 1)
        rel = ((j - i) * t + col - (t - 1)).astype(F32)
        if diagonal:
            keep = (lax.broadcasted_iota(jnp.int32, (t, t), 1) <= lax.broadcasted_iota(jnp.int32, (t, t), 0))
        for hh in range(hps):
            hl = slice(hh * DVA, (hh + 1) * DVA)
            bias = _alibi_slope(jnp.full((1, 1), hg * hps + hh, jnp.int32)) * rel
            q = (q_ref[:, hl] * (DKA ** -0.5)).astype(BF16)
            k = k_ref[:, hl].astype(BF16)
            v = v_ref[:, hl].astype(BF16)
            for half in range(2):
                idx = 2 * hh + half
                sl = slice(half * DKA, (half + 1) * DKA)
                sc = _dot_nt(q[:, sl], k[:, sl]) + bias
                if diagonal:
                    sc = jnp.where(keep, sc, -1e30)
                m_prev = m_ref[idx]
                m_new = jnp.maximum(m_prev, jnp.max(sc, axis=-1, keepdims=True))
                alpha = jnp.exp(m_prev - m_new)
                p = jnp.exp(sc - jnp.concatenate([m_new] * (t // DVA), axis=-1))
                l_ref[idx] = alpha * l_ref[idx] + jnp.sum(p, axis=-1, keepdims=True)
                a_ref[idx] = alpha * a_ref[idx] + _dot(p.astype(BF16), v)
                m_ref[idx] = m_new

    @pl.when(j < i)
    def _():
        step(False)

    @pl.when(j == i)
    def _():
        step(True)
        lam = _lambda_full(lq1[...], lk1[...], lq2[...], lk2[...])
        for hh in range(hps):
            o = a_ref[2 * hh] / l_ref[2 * hh] - lam * (a_ref[2 * hh + 1] / l_ref[2 * hh + 1])
            o_ref[:, hh * DVA:(hh + 1) * DVA] = _subln(o, gain_ref[...])


def _dattn_prompt(z, ka, va, lams, subln_gain, batch, seq, t):
    nt = seq // t
    hps = 2
    pairs = [(i, j) for i in range(nt) for j in range(i + 1)]
    ii = jnp.asarray([p[0] for p in pairs], jnp.int32)
    jj = jnp.asarray([p[1] for p in pairs], jnp.int32)
    vec = pl.BlockSpec((1, DKA), lambda b, h, s, ii, jj: (0, 0))
    grid_spec = pltpu.PrefetchScalarGridSpec(
        num_scalar_prefetch=2,
        grid=(batch, HA // hps, len(pairs)),
        in_specs=[
            pl.BlockSpec((t, hps * DVA), lambda b, h, s, ii, jj: (b * nt + ii[s], h)),
            pl.BlockSpec((t, hps * DVA), lambda b, h, s, ii, jj: (b * nt + jj[s], h)),
            pl.BlockSpec((t, hps * DVA), lambda b, h, s, ii, jj: (b * nt + jj[s], h)),
            vec, vec, vec, vec,
            pl.BlockSpec((1, DVA), lambda b, h, s, ii, jj: (0, 0)),
        ],
        out_specs=pl.BlockSpec((t, hps * DVA), lambda b, h, s, ii, jj: (b * nt + ii[s], h)),
        scratch_shapes=[pltpu.VMEM((2 * hps, t, DVA), F32)] * 3,
    )
    return pl.pallas_call(
        functools.partial(_dattn_prompt_kernel, t=t, hps=hps),
        grid_spec=grid_spec,
        out_shape=jax.ShapeDtypeStruct((batch * seq, HA * DVA), F32),
        compiler_params=_cparams(("arbitrary", "arbitrary", "arbitrary")),
        name="dattn_prompt",
    )(ii, jj, z, ka, va, *[l.reshape(1, DKA) for l in lams], subln_gain.reshape(1, DVA))


def _rows_by_head(row, nrow):
    head = lax.broadcasted_iota(jnp.int32, (nrow, 1), 0) * HA // nrow
    out = jnp.zeros((nrow, DVA), F32)
    for h in range(HA):
        out = jnp.where(head == h, row[:, h * DVA:(h + 1) * DVA], out)
    return out


def _dattn_sample_kernel(pt_ref, q_ref, kn_ref, vn_ref, *rest, n_pages, page, pps):
    kp_refs = rest[:pps]
    vp_refs = rest[pps:2 * pps]
    lq1, lk1, lq2, lk2, gain_ref, o_ref, qm_ref, m_ref, l_ref, a_ref = rest[2 * pps:]
    p = pl.program_id(1)
    nrow = 2 * HA
    rid = lax.broadcasted_iota(jnp.int32, (nrow, 1), 0)
    slope = _alibi_slope(rid // 2)
    past = n_pages * page

    @pl.when(p == 0)
    def _():
        half = lax.broadcasted_iota(jnp.int32, (nrow, DVA), 1) // DKA
        qm_ref[...] = jnp.where(half == rid % 2, _rows_by_head(q_ref[0], nrow), 0.0)
        m_ref[...] = jnp.full(m_ref.shape, -jnp.inf, F32)
        l_ref[...] = jnp.zeros(l_ref.shape, F32)
        a_ref[...] = jnp.zeros(a_ref.shape, F32)

    qm = qm_ref[...]
    qmb = qm.astype(BF16)
    n = lax.broadcasted_iota(jnp.int32, (1, page * HA), 1)
    own_head = n % HA == rid // 2
    scores = []
    for r in range(pps):
        rel = (past - ((p * pps + r) * page + n // HA)).astype(F32)
        kf = kp_refs[r][...].reshape(page * HA, DVA).astype(BF16)
        sc = _dot_nt(qmb, kf) * (DKA ** -0.5) - slope * rel
        scores.append(jnp.where(own_head, sc, -1e30))
    m_new = m_ref[...]
    for sc in scores:
        m_new = jnp.maximum(m_new, jnp.max(sc, axis=-1, keepdims=True))
    alpha = jnp.exp(m_ref[...] - m_new)
    l_new = alpha * l_ref[...]
    a_new = alpha * a_ref[...]
    for r, sc in enumerate(scores):
        pr = jnp.exp(sc - m_new)
        l_new = l_new + jnp.sum(pr, axis=-1, keepdims=True)
        a_new = a_new + _dot(pr.astype(BF16), vp_refs[r][...].reshape(page * HA, DVA).astype(BF16))
    l_ref[...] = l_new
    a_ref[...] = a_new
    m_ref[...] = m_new

    @pl.when(p == n_pages // pps - 1)
    def _():
        sc_n = jnp.sum(qm * _rows_by_head(kn_ref[0], nrow), axis=-1, keepdims=True) * (DKA ** -0.5)
        m_f = jnp.maximum(m_ref[...], sc_n)
        al = jnp.exp(m_ref[...] - m_f)
        pn = jnp.exp(sc_n - m_f)
        l_f = al * l_ref[...] + pn
        a_f = (al * a_ref[...] + pn * _rows_by_head(vn_ref[0], nrow)) / l_f
        lam = _lambda_full(lq1[...], lk1[...], lq2[...], lk2[...])
        outs = []
        for h in range(HA):
            o = a_f[2 * h:2 * h + 1] - lam * a_f[2 * h + 1:2 * h + 2]
            outs.append(_subln(o, gain_ref[...]))
        o_ref[0] = jnp.concatenate(outs, axis=-1)


def _dattn_sample(z, ka, va, cache_k, cache_v, page_table, lams, subln_gain):
    nb, n_pages = page_table.shape
    page = cache_k.shape[1]
    w = HA * DVA
    row3 = lambda a: a.reshape(nb, 1, a.shape[-1])
    vec = pl.BlockSpec((1, DKA), lambda b, p, pt: (0, 0))
    pps = max(d for d in range(1, PAGES_PER_STEP + 1) if n_pages % d == 0)
    page_specs = [
        pl.BlockSpec((None, page, HA, DVA), lambda b, p, pt, r=r: (pt[b * n_pages + p * pps + r], 0, 0, 0))
        for r in range(pps)]
    grid_spec = pltpu.PrefetchScalarGridSpec(
        num_scalar_prefetch=1,
        grid=(nb, n_pages // pps),
        in_specs=[
            pl.BlockSpec((1, 1, w), lambda b, p, pt: (b, 0, 0)),
            pl.BlockSpec((1, 1, w), lambda b, p, pt: (b, 0, 0)),
            pl.BlockSpec((1, 1, w), lambda b, p, pt: (b, 0, 0)),
            *page_specs, *page_specs,
            vec, vec, vec, vec,
            pl.BlockSpec((1, DVA), lambda b, p, pt: (0, 0)),
        ],
        out_specs=pl.BlockSpec((1, 1, w), lambda b, p, pt: (b, 0, 0)),
        scratch_shapes=[pltpu.VMEM((2 * HA, DVA), F32), pltpu.VMEM((2 * HA, 1), F32),
                        pltpu.VMEM((2 * HA, 1), F32), pltpu.VMEM((2 * HA, DVA), F32)],
    )
    out = pl.pallas_call(
        functools.partial(_dattn_sample_kernel, n_pages=n_pages, page=page, pps=pps),
        grid_spec=grid_spec,
        out_shape=jax.ShapeDtypeStruct((nb, 1, w), F32),
        compiler_params=_cparams(("arbitrary", "arbitrary")),
        name="dattn_sample",
    )(page_table.reshape(-1), row3(z), row3(ka), row3(va), *([cache_k] * pps), *([cache_v] * pps),
      *[l.reshape(1, DKA) for l in lams], subln_gain.reshape(1, DVA))
    return out.reshape(nb, w)


def _ret_log_gamma(h):
    return math.log(1.0 - 2.0 ** (-5.0 - h))


def _ret_theta(width):
    theta = 1.0 / (ROPE_BASE ** jnp.linspace(0.0, 1.0, DKB // 2, dtype=F32))
    return jnp.tile(jnp.repeat(theta, 2), width // DKB).reshape(1, width)


def _rotate_pairs(x, cos, sin):
    n = x.shape[-1]
    lane = lax.broadcasted_iota(jnp.int32, x.shape, x.ndim - 1)
    nxt = pltpu.roll(x, n - 1, axis=x.ndim - 1)
    prv = pltpu.roll(x, 1, axis=x.ndim - 1)
    return x * cos + jnp.where(lane % 2 == 0, -nxt, prv) * sin


def _ret_prompt_kernel(q_ref, k_ref, v_ref, g_ref, th_ref, o_ref, s_ref, dec_ref, *, c):
    ci = pl.program_id(1)
    ri = lax.broadcasted_iota(jnp.int32, (c, c), 0)
    cj = lax.broadcasted_iota(jnp.int32, (c, c), 1)

    @pl.when(jnp.logical_and(pl.program_id(0) == 0, ci == 0))
    def _():
        diff = (ri - cj).astype(F32)
        for h in range(HB):
            dec_ref[h] = jnp.where(diff >= 0, jnp.exp(_ret_log_gamma(h) * jnp.maximum(diff, 0.0)), 0.0)

    @pl.when(ci == 0)
    def _():
        s_ref[...] = jnp.zeros(s_ref.shape, F32)

    idx = lax.broadcasted_iota(jnp.int32, (c, 1), 0)
    pos = (ci * c + idx).astype(F32)
    ang = pos * th_ref[...]
    cos = jnp.concatenate([jnp.cos(ang)] * (HB * DKB // 128), axis=-1)
    sin = jnp.concatenate([jnp.sin(ang)] * (HB * DKB // 128), axis=-1)
    qr = _rotate_pairs(q_ref[...], cos, sin)
    kr = _rotate_pairs(k_ref[...], cos, sin) * (DKB ** -0.5)
    idx_f = idx.astype(F32)
    outs = []
    for h in range(HB):
        lg = _ret_log_gamma(h)
        qh = qr[:, h * DKB:(h + 1) * DKB].astype(BF16)
        kh = kr[:, h * DKB:(h + 1) * DKB]
        vh = v_ref[:, h * DVB:(h + 1) * DVB].astype(BF16)
        s_old = s_ref[0, h]
        scores = _dot_nt(qh, kh.astype(BF16)) * dec_ref[h]
        o = _dot(scores.astype(BF16), vh)
        o = o + _dot(qh, s_old.astype(BF16)) * jnp.exp(lg * (idx_f + 1.0))
        kd = (kh * jnp.exp(lg * (c - 1.0 - idx_f))).astype(BF16)
        s_ref[0, h] = math.exp(lg * c) * s_old + _dot_tn(kd, vh)
        gate = g_ref[:, h * DVB:(h + 1) * DVB]
        outs.append(_rms(o) * (gate * _sigmoid(gate)))
    o_ref[...] = jnp.concatenate(outs, axis=-1)


def _ret_prompt(z, batch, seq, c):
    nc = seq // c
    wq = HB * DKB
    wv = HB * DVB
    return pl.pallas_call(
        functools.partial(_ret_prompt_kernel, c=c),
        grid=(batch, nc),
        in_specs=[
            pl.BlockSpec((c, wq), lambda b, i: (b * nc + i, 2)),
            pl.BlockSpec((c, wq), lambda b, i: (b * nc + i, 3)),
            pl.BlockSpec((c, wv), lambda b, i: (b * nc + i, 2)),
            pl.BlockSpec((c, wv), lambda b, i: (b * nc + i, 3)),
            pl.BlockSpec((1, 128), lambda b, i: (0, 0)),
        ],
        out_specs=[
            pl.BlockSpec((c, wv), lambda b, i: (b * nc + i, 0)),
            pl.BlockSpec((1, HB, DKB, DVB), lambda b, i: (b, 0, 0, 0)),
        ],
        out_shape=[
            jax.ShapeDtypeStruct((batch * seq, wv), F32),
            jax.ShapeDtypeStruct((batch, HB, DKB, DVB), F32),
        ],
        scratch_shapes=[pltpu.VMEM((HB, c, c), F32)],
        compiler_params=_cparams(("arbitrary", "arbitrary")),
        name="ret_prompt",
    )(z, z, z, z, _ret_theta(128))


def _ret_sample_kernel(qk_ref, v_ref, g_ref, th_ref, s_ref, o_ref, so_ref, *, pos):
    ang = float(pos) * th_ref[...]
    x = qk_ref[...]
    ncol = x.shape[1]
    d = lax.broadcasted_iota(jnp.int32, x.shape, 0)
    partner = jnp.where(d % 2 == 0, -pltpu.roll(x, DKB - 1, axis=0), pltpu.roll(x, 1, axis=0))
    qk = x * jnp.cos(ang) + partner * jnp.sin(ang)
    is_k = lax.broadcasted_iota(jnp.int32, (1, ncol), 1) % (2 * HB) >= HB
    qkb = (qk * jnp.where(is_k, DKB ** -0.5, 1.0)).astype(BF16)
    sel_row = lax.broadcasted_iota(jnp.int32, (ncol, HB * DVB), 0)
    sel_head = lax.broadcasted_iota(jnp.int32, (ncol, HB * DVB), 1) // DVB
    rows = []
    for s in range(v_ref.shape[0]):
        q_sel = jnp.where(sel_row == s * 2 * HB + sel_head, 1.0, 0.0)
        kv_sel = jnp.where(sel_row == s * 2 * HB + HB + sel_head, v_ref[s:s + 1, :], 0.0)
        res = _dot(qkb, jnp.concatenate([q_sel, kv_sel], axis=-1).astype(BF16))
        outs = []
        for h in range(HB):
            q_lanes = res[:, h * DVB:(h + 1) * DVB]
            outer = res[:, (HB + h) * DVB:(HB + h + 1) * DVB]
            s_new = math.exp(_ret_log_gamma(h)) * s_ref[s, h] + outer
            so_ref[s, h] = s_new
            o = jnp.sum(q_lanes * s_new, axis=0, keepdims=True)
            gate = g_ref[s:s + 1, h * DVB:(h + 1) * DVB]
            outs.append(_rms(o) * (gate * _sigmoid(gate)))
        rows.append(jnp.concatenate(outs, axis=-1))
    o_ref[...] = jnp.concatenate(rows, axis=0)


def _ret_sample(z, state, pos):
    nb = z.shape[0]
    wq = HB * DKB
    wv = HB * DVB
    sb = 8 if nb % 8 == 0 else nb
    qk_t = jnp.transpose(z[:, wv:wv + 2 * wq].reshape(nb, 2 * HB, DKB), (2, 0, 1)).reshape(DKB, nb * 2 * HB)
    theta_col = _ret_theta(DKB).reshape(DKB, 1)
    return pl.pallas_call(
        functools.partial(_ret_sample_kernel, pos=pos),
        grid=(nb // sb,),
        in_specs=[
            pl.BlockSpec((DKB, sb * 2 * HB), lambda b: (0, b)),
            pl.BlockSpec((sb, wv), lambda b: (b, 2)),
            pl.BlockSpec((sb, wv), lambda b: (b, 3)),
            pl.BlockSpec((DKB, 1), lambda b: (0, 0)),
            pl.BlockSpec((sb, HB, DKB, DVB), lambda b: (b, 0, 0, 0)),
        ],
        out_specs=[
            pl.BlockSpec((sb, wv), lambda b: (b, 0)),
            pl.BlockSpec((sb, HB, DKB, DVB), lambda b: (b, 0, 0, 0)),
        ],
        out_shape=[
            jax.ShapeDtypeStruct((nb, wv), F32),
            jax.ShapeDtypeStruct(state.shape, F32),
        ],
        compiler_params=_cparams(("arbitrary",)),
        name="ret_sample",
    )(qk_t, z, z, theta_col, state)


def _mm_resid_kernel(*refs, n_in):
    a_refs = refs[:n_in]
    w_refs = refs[n_in:2 * n_in]
    x_ref = refs[2 * n_in]
    o_ref = refs[2 * n_in + 1]
    acc = x_ref[...]
    for a_ref, w_ref in zip(a_refs, w_refs):
        acc = acc + _dot(a_ref[...].astype(BF16), w_ref[...])
    o_ref[...] = acc


def _mm_resid(a_list, w_bf16, resid, tm):
    m, n = resid.shape
    tn = n
    in_specs = []
    for a in a_list:
        in_specs.append(pl.BlockSpec((tm, a.shape[1]), lambda i, j: (i, 0)))
    row = 0
    for a in a_list:
        kb = a.shape[1]
        in_specs.append(pl.BlockSpec((kb, tn), lambda i, j, r=row // kb: (r, j)))
        row += kb
    in_specs.append(pl.BlockSpec((tm, tn), lambda i, j: (i, j)))
    return pl.pallas_call(
        functools.partial(_mm_resid_kernel, n_in=len(a_list)),
        grid=(m // tm, n // tn),
        in_specs=in_specs,
        out_specs=pl.BlockSpec((tm, tn), lambda i, j: (i, j)),
        out_shape=jax.ShapeDtypeStruct((m, n), F32),
        compiler_params=_cparams(("arbitrary", "arbitrary")),
        name="mm_resid",
    )(*a_list, *([w_bf16] * len(a_list)), resid)


def _ffn_kernel(x_ref, g_ref, wg_ref, wu_ref, wd_ref, fg_ref, o_ref, xn_ref, acc_ref, *, final_norm):
    j = pl.program_id(1)

    @pl.when(j == 0)
    def _():
        xn_ref[...] = (_rms(x_ref[...]) * g_ref[...]).astype(BF16)
        acc_ref[...] = jnp.zeros(acc_ref.shape, F32)

    xn = xn_ref[...]
    gate = _dot(xn, wg_ref[...])
    up = _dot(xn, wu_ref[...])
    hid = (gate * _sigmoid(gate) * up).astype(BF16)
    acc_ref[...] += _dot(hid, wd_ref[...])

    @pl.when(j == pl.num_programs(1) - 1)
    def _():
        y = x_ref[...] + acc_ref[...]
        if final_norm:
            y = _rms(y) * fg_ref[...]
        o_ref[...] = y


def _ffn(x, gain, wg, wu, wd, final_gain, final_norm, tm, tf=512):
    m = x.shape[0]
    dff = wg.shape[1]
    return pl.pallas_call(
        functools.partial(_ffn_kernel, final_norm=final_norm),
        grid=(m // tm, dff // tf),
        in_specs=[
            pl.BlockSpec((tm, D_MODEL), lambda i, j: (i, 0)),
            pl.BlockSpec((1, D_MODEL), lambda i, j: (0, 0)),
            pl.BlockSpec((D_MODEL, tf), lambda i, j: (0, j)),
            pl.BlockSpec((D_MODEL, tf), lambda i, j: (0, j)),
            pl.BlockSpec((tf, D_MODEL), lambda i, j: (j, 0)),
            pl.BlockSpec((1, D_MODEL), lambda i, j: (0, 0)),
        ],
        out_specs=pl.BlockSpec((tm, D_MODEL), lambda i, j: (i, 0)),
        out_shape=jax.ShapeDtypeStruct((m, D_MODEL), F32),
        scratch_shapes=[pltpu.VMEM((tm, D_MODEL), BF16), pltpu.VMEM((tm, D_MODEL), F32)],
        compiler_params=_cparams(("arbitrary", "arbitrary")),
        name="ffn",
    )(x, gain.reshape(1, D_MODEL), wg, wu, wd, final_gain.reshape(1, D_MODEL))


def _norm_and_shift(x_ref, g_ref, prev_ref, carry_ref, i, tiles_per_seq):
    h = _rms(x_ref[...]) * g_ref[...]
    if prev_ref is None:
        tm = h.shape[0]
        first = (i % tiles_per_seq) == 0
        prev_row = jnp.where(first, 0.0, carry_ref[0:1, :])
        rows = lax.broadcasted_iota(jnp.int32, (tm, 1), 0)
        shifted = jnp.where(rows == 0, prev_row, pltpu.roll(h, 1, axis=0))
        carry_ref[0:1, :] = h[tm - 1:tm, :]
    else:
        shifted = prev_ref[...]
    return h, shifted - h


def _rwkv_rkv_kernel(*refs, per_row_prev, tiles_per_seq):
    if per_row_prev:
        x_ref, g_ref, prev_ref, mu_ref, w_ref, o_ref, hl_ref, h_ref, xx_ref, carry_ref = refs
    else:
        x_ref, g_ref, mu_ref, w_ref, o_ref, hl_ref, h_ref, xx_ref, carry_ref = refs
        prev_ref = None
    i = pl.program_id(0)
    j = pl.program_id(1)

    @pl.when(j == 0)
    def _():
        h, xx = _norm_and_shift(x_ref, g_ref, prev_ref, carry_ref, i, tiles_per_seq)
        h_ref[...] = h
        xx_ref[...] = xx
        if per_row_prev:
            hl_ref[...] = h
        else:
            hl_ref[0] = h[h.shape[0] - 1:, :]

    xm = (h_ref[...] + xx_ref[...] * mu_ref[0]).astype(BF16)
    o_ref[...] = _dot(xm, w_ref[...])


def _rwkv_rkv(x, gain, prev, mu_rkv, w_rkv, tm, tiles_per_seq):
    m = x.shape[0]
    tn = 1024
    per_seq = D_MODEL // tn
    per_row_prev = prev is not None
    in_specs = [pl.BlockSpec((tm, D_MODEL), lambda i, j: (i, 0)),
                pl.BlockSpec((1, D_MODEL), lambda i, j: (0, 0))]
    args = [x, gain.reshape(1, D_MODEL)]
    if per_row_prev:
        in_specs.append(pl.BlockSpec((tm, D_MODEL), lambda i, j: (i, 0)))
        args.append(prev)
        hl_spec = pl.BlockSpec((tm, D_MODEL), lambda i, j: (i, 0))
        hl_shape = jax.ShapeDtypeStruct((m, D_MODEL), F32)
    else:
        nseq = m // (tm * tiles_per_seq)
        hl_spec = pl.BlockSpec((1, 1, D_MODEL), lambda i, j: (i // tiles_per_seq, 0, 0))
        hl_shape = jax.ShapeDtypeStruct((nseq, 1, D_MODEL), F32)
    in_specs += [pl.BlockSpec((1, 1, D_MODEL), lambda i, j: (j // per_seq, 0, 0)),
                 pl.BlockSpec((D_MODEL, tn), lambda i, j: (0, j))]
    args += [mu_rkv, w_rkv]
    return pl.pallas_call(
        functools.partial(_rwkv_rkv_kernel, per_row_prev=per_row_prev, tiles_per_seq=tiles_per_seq),
        grid=(m // tm, 3 * per_seq),
        in_specs=in_specs,
        out_specs=[pl.BlockSpec((tm, tn), lambda i, j: (i, j)), hl_spec],
        out_shape=[jax.ShapeDtypeStruct((m, 3 * D_MODEL), F32), hl_shape],
        scratch_shapes=[pltpu.VMEM((tm, D_MODEL), F32), pltpu.VMEM((tm, D_MODEL), F32),
                        pltpu.VMEM((8, D_MODEL), F32)],
        compiler_params=_cparams(("arbitrary", "arbitrary")),
        name="rwkv_rkv",
    )(*args)


def _rwkv_lora_kernel(*refs, per_row_prev, tiles_per_seq):
    if per_row_prev:
        (x_ref, g_ref, prev_ref, mu_ref, w0_ref, w1_ref, w2_ref, a0_ref, a1_ref, a2_ref, g1_ref, g2_ref,
         lw_ref, a_ref, go_ref, carry_ref) = refs
    else:
        (x_ref, g_ref, mu_ref, w0_ref, w1_ref, w2_ref, a0_ref, a1_ref, a2_ref, g1_ref, g2_ref,
         lw_ref, a_ref, go_ref, carry_ref) = refs
        prev_ref = None
    h, xx = _norm_and_shift(x_ref, g_ref, prev_ref, carry_ref, pl.program_id(0), tiles_per_seq)
    xw = (h + xx * mu_ref[0]).astype(BF16)
    xa = (h + xx * mu_ref[1]).astype(BF16)
    xg = (h + xx * mu_ref[2]).astype(BF16)
    zw = w0_ref[...] + _dot(jnp.tanh(_dot(xw, w1_ref[...])).astype(BF16), w2_ref[...])
    lw_ref[...] = -math.exp(-0.5) * _sigmoid(zw)
    a_ref[...] = _sigmoid(a0_ref[...] + _dot(_dot(xa, a1_ref[...]).astype(BF16), a2_ref[...]))
    go_ref[...] = _dot(_sigmoid(_dot(xg, g1_ref[...])).astype(BF16), g2_ref[...])


def _rwkv_lora(x, gain, prev, mu_wag, w0, w1, w2, a0, a1, a2, g1, g2, tm, tiles_per_seq):
    m = x.shape[0]
    per_row_prev = prev is not None
    full = lambda a: pl.BlockSpec(a.shape, lambda i: (0,) * a.ndim)
    row = pl.BlockSpec((tm, D_MODEL), lambda i: (i, 0))
    args = [x, gain.reshape(1, D_MODEL)]
    in_specs = [row, full(args[1])]
    if per_row_prev:
        in_specs.append(row)
        args.append(prev)
    rest = [mu_wag, w0.reshape(1, D_MODEL), w1, w2, a0.reshape(1, D_MODEL), a1, a2, g1, g2]
    in_specs += [full(a) for a in rest]
    args += rest
    return pl.pallas_call(
        functools.partial(_rwkv_lora_kernel, per_row_prev=per_row_prev, tiles_per_seq=tiles_per_seq),
        grid=(m // tm,),
        in_specs=in_specs,
        out_specs=[row, row, row],
        out_shape=[jax.ShapeDtypeStruct((m, D_MODEL), F32)] * 3,
        scratch_shapes=[pltpu.VMEM((8, D_MODEL), F32)],
        compiler_params=_cparams(("arbitrary",)),
        name="rwkv_lora",
    )(*args)


def _wkv_keys(k, a, kk_gain, ka_gain, head_sum):
    kk = k * kk_gain
    kk = kk / jnp.maximum(jnp.sqrt(head_sum(kk * kk)), 1e-12)
    return kk, kk * a, k * (1.0 + (a - 1.0) * ka_gain)


def _head_norm(o, head_sum):
    mean = head_sum(o) * (1.0 / NC)
    var = head_sum(jnp.square(o - mean)) * (1.0 / NC)
    return (o - mean) * lax.rsqrt(var + GN_EPS)


def _wkv_finish(on, r, k2, v, g, rk, ln_g, ln_b, head_sum):
    bonus = head_sum(r * k2 * rk) * v
    return (on * ln_g + ln_b + bonus) * g


def _pair_sum(x):
    lane = lax.broadcasted_iota(jnp.int32, x.shape, 1)
    s0 = jnp.sum(x[:, :NC], axis=-1, keepdims=True)
    s1 = jnp.sum(x[:, NC:], axis=-1, keepdims=True)
    return jnp.where(lane < NC, s0, s1)


def _bmm(a, b):
    return jnp.einsum("cij,cjk->cik", a, b, preferred_element_type=F32)


def _bmm_nt(a, b):
    return jnp.einsum("cik,cjk->cij", a, b, preferred_element_type=F32)


def _wkv_prompt_kernel(r_ref, k_ref, v_ref, lw_ref, a_ref, kkg_ref, kag_ref, rk_ref, g_ref, lng_ref, lnb_ref,
                       y_ref, so_ref, st_ref, gh_ref, eft_ref, wend_ref, bonus_ref, *, tc, nt):
    c = WKV_CHUNK
    nch = tc // c
    s = pl.program_id(0)

    @pl.when(s == 0)
    def _():
        for ref in (st_ref, gh_ref, eft_ref, wend_ref, bonus_ref):
            ref[...] = jnp.zeros(ref.shape, F32)

    prev = jnp.maximum(s - 1, 0)
    slot = prev % 2
    first = prev % nt == 0
    ss = [jnp.where(first, 0.0, st_ref[0]), jnp.where(first, 0.0, st_ref[1])]
    o_chunks = [[], []]
    lane_sum = lambda t: jnp.sum(t, axis=-1, keepdims=True)
    n_points = 2 + int(math.log2(c))
    done = [0]

    def advance_stage2(point):
        upto = nch * (point + 1) // n_points
        for ch in range(done[0], upto):
            w_end = wend_ref[slot, ch]
            for hh in range(2):
                n = hh * nch + ch
                gh = gh_ref[slot, n]
                eft = eft_ref[slot, n]
                sb = ss[hh].astype(BF16)
                o_chunks[hh].append(_head_norm(_dot_nt(gh[:, :NC].astype(BF16), sb) + gh[:, NC:], lane_sum))
                ss[hh] = (ss[hh] * w_end[:, hh * NC:(hh + 1) * NC]
                          + _dot(sb, eft[:NC].astype(BF16)) + eft[NC:])
        done[0] = upto

    ri = lax.broadcasted_iota(jnp.int32, (c, c), 0)
    cj = lax.broadcasted_iota(jnp.int32, (c, c), 1)
    tri = (ri >= cj).astype(BF16)
    gi = lax.broadcasted_iota(jnp.int32, (2 * c, 2 * c), 0)
    gj = lax.broadcasted_iota(jnp.int32, (2 * c, 2 * c), 1) % c
    gmask = (gj < jnp.where(gi < c, gi, gi - c + 1))[None]

    r = r_ref[...]
    v = v_ref[...]
    kk, bv, k2 = _wkv_keys(k_ref[...], a_ref[...], kkg_ref[...], kag_ref[...], _pair_sum)
    to3 = lambda x: x.reshape(nch, c, 2 * NC)
    lw3 = to3(lw_ref[...])
    p0 = lw3.astype(BF16)
    r1 = lw3 - p0.astype(F32)
    p1 = r1.astype(BF16)
    p2 = (r1 - p1.astype(F32)).astype(BF16)
    trib = jnp.broadcast_to(tri[None], (nch, c, c))
    cum = _bmm(trib, p0) + (_bmm(trib, p1) + _bmm(trib, p2))
    cum_last = cum[:, c - 1:c, :]
    w_rest = jnp.exp(cum_last - cum)
    w_inv = jnp.exp(-cum)
    am = -to3(kk) * jnp.exp(cum - lw3)
    rm = to3(r) * jnp.exp(cum)
    bm = (to3(bv) * w_inv).astype(BF16)
    km = (to3(k2) * w_inv).astype(BF16)
    bt = (to3(bv) * w_rest).astype(BF16)
    kt = (to3(k2) * w_rest).astype(BF16)
    v3 = to3(v)
    advance_stage2(0)

    heads = lambda x: jnp.concatenate([x[:, :, :NC], x[:, :, NC:]], axis=0)
    am_h, rm_h, v_h = heads(am), heads(rm), heads(v3)
    x1 = jnp.concatenate([am_h, rm_h], axis=1).astype(BF16)
    x2 = jnp.concatenate([heads(bm), heads(km)], axis=1)
    g4 = jnp.where(gmask, _bmm_nt(x1, x2), 0.0)
    lv = _bmm(g4[:, :, c:].astype(BF16), v_h.astype(BF16))
    advance_stage2(1)
    x = jnp.concatenate([am_h, lv[:, :c]], axis=-1)
    lp = g4[:, :c, :c].astype(BF16)
    steps = int(math.log2(c))
    for i in range(steps - 1):
        prod = _bmm(lp, jnp.concatenate([x.astype(BF16), lp], axis=-1))
        x = x + prod[:, :, :2 * NC]
        lp = prod[:, :, 2 * NC:].astype(BF16)
        advance_stage2(2 + i)
    x = x + _bmm(lp, x.astype(BF16))
    advance_stage2(n_points - 1)
    gh_new = _bmm(g4[:, c:, :c].astype(BF16), x.astype(BF16)) + jnp.concatenate([rm_h, lv[:, c:]], axis=-1)
    btk = jnp.concatenate([heads(bt), heads(kt)], axis=1)
    zv = jnp.concatenate([jnp.zeros_like(v_h), v_h], axis=-1)
    pqv = jnp.concatenate([x, zv], axis=1).astype(BF16)
    st_ref[0] = ss[0]
    st_ref[1] = ss[1]
    so_ref[0, 0] = ss[0]
    so_ref[0, 1] = ss[1]
    on = jnp.concatenate([jnp.concatenate(oc, axis=0) for oc in o_chunks], axis=-1)
    y_ref[...] = (on * lng_ref[...] + lnb_ref[...] + bonus_ref[slot]) * g_ref[...]

    nxt = s % 2
    gh_ref[nxt] = gh_new
    for n in range(2 * nch):
        eft_ref[nxt, n] = _dot_tn(pqv[n], btk[n])
    wend_ref[nxt] = jnp.exp(cum_last)
    bonus_ref[nxt] = _pair_sum(r * k2 * rk_ref[...]) * v


def _wkv_prompt(rkv, lw, a, g, kk_gain, ka_gain, rk, ln_g, ln_b, batch, seq, tc):
    nt = seq // tc
    nhp = HC // 2
    nblk = batch * nhp * nt
    nch = tc // WKV_CHUNK

    def rows_cols(blk):
        b = blk // (nhp * nt)
        return b * nt + blk % nt, (blk // nt) % nhp, b

    cur = lambda s: jnp.minimum(s, nblk - 1)
    prv = lambda s: jnp.maximum(s - 1, 0)
    tok = lambda off, which: pl.BlockSpec(
        (tc, 2 * NC), lambda s: (rows_cols(which(s))[0], off + rows_cols(which(s))[1]))
    par = lambda which: pl.BlockSpec((1, 2 * NC), lambda s: (0, rows_cols(which(s))[1]))
    return pl.pallas_call(
        functools.partial(_wkv_prompt_kernel, tc=tc, nt=nt),
        grid=(nblk + 1,),
        in_specs=[tok(0, cur), tok(nhp, cur), tok(2 * nhp, cur), tok(0, cur), tok(0, cur),
                  par(cur), par(cur), par(cur), tok(0, prv), par(prv), par(prv)],
        out_specs=[
            tok(0, prv),
            pl.BlockSpec((1, 2, NC, NC), lambda s: (rows_cols(prv(s))[2], rows_cols(prv(s))[1], 0, 0)),
        ],
        out_shape=[
            jax.ShapeDtypeStruct((batch * seq, D_MODEL), F32),
            jax.ShapeDtypeStruct((batch, HC, NC, NC), F32),
        ],
        scratch_shapes=[
            pltpu.VMEM((2, NC, NC), F32),
            pltpu.VMEM((2, 2 * nch, WKV_CHUNK, 2 * NC), F32),
            pltpu.VMEM((2, 2 * nch, 2 * NC, NC), F32),
            pltpu.VMEM((2, nch, 1, 2 * NC), F32),
            pltpu.VMEM((2, tc, 2 * NC), F32),
        ],
        compiler_params=_cparams(("arbitrary",)),
        name="wkv_prompt",
    )(rkv, rkv, rkv, lw, a, kk_gain, ka_gain, rk, g, ln_g, ln_b)


def _wkv_sample_kernel(r_ref, k_ref, v_ref, lw_ref, a_ref, g_ref, kkg_ref, kag_ref, rk_ref, lng_ref, lnb_ref,
                       s_ref, y_ref, so_ref, o_ref):
    ys = []
    col_sum = lambda x: jnp.sum(x, axis=0, keepdims=True)
    for hh in range(2):
        sl = slice(hh * NC, (hh + 1) * NC)
        t = lambda ref: ref[...].T[sl, :]
        r, k, v, a, g = t(r_ref), t(k_ref), t(v_ref), t(a_ref), t(g_ref)
        w = jnp.exp(t(lw_ref))
        kk, bv, k2 = _wkv_keys(k, a, kkg_ref[sl, :], kag_ref[sl, :], col_sum)
        for i in range(NC):
            s_old = s_ref[hh, i]
            sk = jnp.sum(s_old * kk, axis=0, keepdims=True)
            s_new = s_old * w - sk * bv + v[i:i + 1, :] * k2
            so_ref[hh, i] = s_new
            o_ref[i:i + 1, :] = jnp.sum(s_new * r, axis=0, keepdims=True)
        ys.append(_wkv_finish(_head_norm(o_ref[...], col_sum), r, k2, v, g,rk_ref[sl, :], lng_ref[sl, :], lnb_ref[sl, :], col_sum))
    y_ref[...] = jnp.concatenate(ys, axis=0).T


def _wkv_sample(rkv, lw, a, g, kk_gain, ka_gain, rk, ln_g, ln_b, state_t):
    nb = lw.shape[0]
    nhp = HC // 2
    tok = lambda off: pl.BlockSpec((nb, 2 * NC), lambda hp: (0, off + hp))
    par = pl.BlockSpec((2 * NC, 1), lambda hp: (hp, 0))
    col = lambda p: p.reshape(D_MODEL, 1)
    st_spec = pl.BlockSpec((2, NC, NC, nb), lambda hp: (hp, 0, 0, 0))
    return pl.pallas_call(
        _wkv_sample_kernel,
        grid=(nhp,),
        in_specs=[tok(0), tok(nhp), tok(2 * nhp), tok(0), tok(0), tok(0), par, par, par, par, par, st_spec],
        out_specs=[pl.BlockSpec((nb, 2 * NC), lambda hp: (0, hp)), st_spec],
        out_shape=[
            jax.ShapeDtypeStruct((nb, D_MODEL), F32),
            jax.ShapeDtypeStruct(state_t.shape, F32),
        ],
        scratch_shapes=[pltpu.VMEM((NC, nb), F32)],
        compiler_params=_cparams(("arbitrary",)),
        name="wkv_sample",
    )(rkv, rkv, rkv, lw, a, g, col(kk_gain), col(ka_gain), col(rk), col(ln_g), col(ln_b), state_t)


def _row_tile(m, want):
    return min(m, want)


def _trunk(x, batch, seq, past, W):
    m = x.shape[0]
    prompt = past is None
    tm = _row_tile(m, 512)
    lams = (W["lambda_q1"], W["lambda_k1"], W["lambda_q2"], W["lambda_k2"])

    ka, va, z = _inproj0(x, W["norm0_mix"], W["w_in0"], tm)
    if prompt:
        o_a = _dattn_prompt(z, ka, va, lams, W["subln_gain"], batch, seq, min(seq, 512))
        o_b, ret_new = _ret_prompt(z, batch, seq, min(seq, 256))
    else:
        o_a = _dattn_sample(z, ka, va, past["cache_k"], past["cache_v"], past["page_table"], lams,
                            W["subln_gain"])
        o_b, ret_new = _ret_sample(z, past["state_ret"], past["pos"])
    x = _mm_resid([o_a, o_b], W["w_out0"], x, _row_tile(m, 256))
    x = _ffn(x, W["norm0_ffn"], W["w_gate0"], W["w_up0"], W["w_down0"], W["norm_final"], False, tm)

    tiles_per_seq = max(seq // tm, 1)
    prev = None if prompt else past["state_shift"]
    rkv, hlast = _rwkv_rkv(x, W["norm1_mix"], prev, W["mu_rkv"], W["w_rkv"], tm, tiles_per_seq)
    tl = _row_tile(m, 256)
    lw, a, g = _rwkv_lora(x, W["norm1_mix"], prev, W["mu_wag"], W["decay_w0"], W["decay_w1"], W["decay_w2"],
                          W["aaa_a0"], W["aaa_a1"], W["aaa_a2"], W["gate_g1"], W["gate_g2"],
                          tl, max(seq // tl, 1))
    head_params = (W["k_k"], W["k_a"], W["r_k"], W["lnx_gain"], W["lnx_bias"])
    if prompt:
        y, wkv_new = _wkv_prompt(rkv, lw, a, g, *head_params, batch, seq, min(seq, 512))
        shift_new = hlast.reshape(batch, D_MODEL)
    else:
        state_t = jnp.transpose(past["state_wkv"], (1, 2, 3, 0))
        y, wkv_t = _wkv_sample(rkv, lw, a, g, *head_params, state_t)
        wkv_new = jnp.transpose(wkv_t, (3, 0, 1, 2))
        shift_new = hlast
    x = _mm_resid([y], W["w_out1"], x, _row_tile(m, 256))
    x = _ffn(x, W["norm1_ffn"], W["w_gate1"], W["w_up1"], W["w_down1"], W["norm_final"], True, tm)
    return x, ka, va, ret_new, wkv_new, shift_new


def kernel(x_prompt, x_sample, cache_k, cache_v, page_table, state_ret, state_wkv, state_shift, norm0_mix, w_in0, lambda_q1, lambda_k1, lambda_q2, lambda_k2, subln_gain, w_out0, norm0_ffn, w_gate0, w_up0, w_down0, norm1_mix, mu1, w_r1, w_k1, w_v1, decay_w0, decay_w1, decay_w2, aaa_a0, aaa_a1, aaa_a2, gate_g1, gate_g2, k_k, k_a, r_k, lnx_gain, lnx_bias, w_out1, norm1_ffn, w_gate1, w_up1, w_down1, norm_final):
    bf = lambda w: w.astype(BF16)
    row = lambda p: p.reshape(1, D_MODEL)
    W = dict(
        norm0_mix=norm0_mix, w_in0=bf(w_in0), lambda_q1=lambda_q1, lambda_k1=lambda_k1, lambda_q2=lambda_q2,
        lambda_k2=lambda_k2, subln_gain=subln_gain, w_out0=bf(w_out0), norm0_ffn=norm0_ffn,
        w_gate0=bf(w_gate0), w_up0=bf(w_up0), w_down0=bf(w_down0),
        norm1_mix=norm1_mix,
        mu_rkv=jnp.stack([mu1[0], mu1[2], mu1[3]]).reshape(3, 1, D_MODEL),
        mu_wag=jnp.stack([mu1[1], mu1[4], mu1[5]]).reshape(3, 1, D_MODEL),
        w_rkv=jnp.concatenate([bf(w_r1), bf(w_k1), bf(w_v1)], axis=1),
        decay_w0=decay_w0, decay_w1=bf(decay_w1), decay_w2=bf(decay_w2),
        aaa_a0=aaa_a0, aaa_a1=bf(aaa_a1), aaa_a2=bf(aaa_a2), gate_g1=bf(gate_g1), gate_g2=bf(gate_g2),
        k_k=row(k_k), k_a=row(k_a), r_k=row(r_k), lnx_gain=row(lnx_gain), lnx_bias=row(lnx_bias),
        w_out1=bf(w_out1), norm1_ffn=norm1_ffn, w_gate1=bf(w_gate1), w_up1=bf(w_up1), w_down1=bf(w_down1),
        norm_final=norm_final,
    )
    bsz, seq = x_prompt.shape[:2]
    y_p, k_p, v_p, ret_p, wkv_p, shift_p = _trunk(x_prompt.reshape(bsz * seq, D_MODEL), bsz, seq, None, W)
    nb, dseq = x_sample.shape[:2]
    past_len = page_table.shape[1] * cache_k.shape[1]
    past = dict(cache_k=cache_k, cache_v=cache_v, page_table=page_table, state_ret=state_ret,
                state_wkv=state_wkv, state_shift=state_shift, pos=past_len)
    y_s, k_s, v_s, ret_s, wkv_s, shift_s = _trunk(x_sample.reshape(nb * dseq, D_MODEL), nb, dseq, past, W)
    return (y_p.reshape(bsz, seq, D_MODEL), y_s.reshape(nb, dseq, D_MODEL),
            k_p.reshape(bsz, seq, HA, 2 * DKA), v_p.reshape(bsz, seq, HA, DVA),
            k_s.reshape(nb, dseq, HA, 2 * DKA), v_s.reshape(nb, dseq, HA, DVA),
            ret_p, ret_s, wkv_p, wkv_s, shift_p, shift_s)
```

```python
import functools
import math

import jax
import jax.numpy as jnp
from jax import lax
from jax.experimental import pallas as pl
from jax.experimental.pallas import tpu as pltpu

F32 = jnp.float32
BF16 = jnp.bfloat16

D_MODEL = 2048
EPS = 1e-6
DKA = 64
DVA = 128
HA = 8
LAMBDA_INIT = 0.8 - 0.6 * math.exp(-0.3 * 0)
DKB = 64
DVB = 128
HB = 8
ROPE_BASE = 10000.0
NC = 64
HC = D_MODEL // NC
GN_EPS = 64e-5
WKV_CHUNK = 64
PAGES_PER_STEP = 16
VMEM_LIMIT = 48 * 1024 * 1024


def _cparams(sem):
    return pltpu.CompilerParams(dimension_semantics=sem, vmem_limit_bytes=VMEM_LIMIT)


def _rms(x, eps=EPS):
    return x * lax.rsqrt(jnp.mean(x * x, axis=-1, keepdims=True) + eps)


def _sigmoid(x):
    return 1.0 / (1.0 + jnp.exp(-x))


def _dot(a, b):
    return jnp.dot(a, b, preferred_element_type=F32)


def _dot_nt(a, b):
    return lax.dot_general(a, b, (((1,), (1,)), ((), ())), preferred_element_type=F32)


def _dot_tn(a, b):
    return lax.dot_general(a, b, (((0,), (0,)), ((), ())), preferred_element_type=F32)


def _split2(x):
    hi = x.astype(BF16)
    lo = (x - hi.astype(F32)).astype(BF16)
    return hi, lo


def _dot_f32(a, b, dot=_dot):
    ah, al = _split2(a)
    bh, bl = _split2(b)
    return dot(ah, bh) + (dot(ah, bl) + dot(al, bh))


def _inproj0_kernel(x_ref, g_ref, w_ref, ka_ref, va_ref, z_ref, xn_ref):
    j = pl.program_id(1)

    @pl.when(j == 0)
    def _():
        xn_ref[...] = (_rms(x_ref[...]) * g_ref[...]).astype(BF16)

    acc = _dot(xn_ref[...], w_ref[...])

    @pl.when(j == 1)
    def _():
        ka_ref[...] = acc

    @pl.when(j == 2)
    def _():
        va_ref[...] = acc

    @pl.when(jnp.logical_and(j != 1, j != 2))
    def _():
        z_ref[...] = acc


def _inproj0(x, gain, w_bf16, tm):
    m = x.shape[0]
    tn = 1024
    nj = w_bf16.shape[1] // tn
    return pl.pallas_call(
        _inproj0_kernel,
        grid=(m // tm, nj),
        in_specs=[
            pl.BlockSpec((tm, D_MODEL), lambda i, j: (i, 0)),
            pl.BlockSpec((1, D_MODEL), lambda i, j: (0, 0)),
            pl.BlockSpec((D_MODEL, tn), lambda i, j: (0, j)),
        ],
        out_specs=[
            pl.BlockSpec((tm, tn), lambda i, j: (i, 0)),
            pl.BlockSpec((tm, tn), lambda i, j: (i, 0)),
            pl.BlockSpec((tm, tn), lambda i, j: (i, jnp.maximum(j - 2, 0))),
        ],
        out_shape=[
            jax.ShapeDtypeStruct((m, tn), F32),
            jax.ShapeDtypeStruct((m, tn), F32),
            jax.ShapeDtypeStruct((m, 4 * tn), F32),
        ],
        scratch_shapes=[pltpu.VMEM((tm, D_MODEL), BF16)],
        compiler_params=_cparams(("arbitrary", "arbitrary")),
        name="inproj0",
    )(x, gain.reshape(1, D_MODEL), w_bf16)


def _lambda_full(lq1, lk1, lq2, lk2):
    s1 = jnp.sum(lq1 * lk1, axis=-1, keepdims=True)
    s2 = jnp.sum(lq2 * lk2, axis=-1, keepdims=True)
    return jnp.exp(s1) - jnp.exp(s2) + LAMBDA_INIT


def _alibi_slope(h_vec):
    slope = jnp.zeros(h_vec.shape, F32)
    for h in range(HA):
        slope = jnp.where(h_vec == h, 2.0 ** (-(h + 1)), slope)
    return slope


def _subln(o, gain):
    return _rms(o) * gain * (1.0 - LAMBDA_INIT)


def _dattn_prompt_kernel(ii_ref, jj_ref, q_ref, k_ref, v_ref, lq1, lk1, lq2, lk2, gain_ref, o_ref,
                         m_ref, l_ref, a_ref, *, t, hps):
    hg = pl.program_id(1)
    s = pl.program_id(2)
    i = ii_ref[s]
    j = jj_ref[s]

    @pl.when(j == 0)
    def _():
        m_ref[...] = jnp.full(m_ref.shape, -jnp.inf, F32)
        l_ref[...] = jnp.zeros(l_ref.shape, F32)
        a_ref[...] = jnp.zeros(a_ref.shape, F32)

    def step(diagonal):
        col = lax.broadcasted_iota(jnp.int32, (1, t), 1)
        rel = ((j - i) * t + col - (t - 1)).astype(F32)
        if diagonal:
            keep = (lax.broadcasted_iota(jnp.int32, (t, t), 1) <= lax.broadcasted_iota(jnp.int32, (t, t), 0))
        for hh in range(hps):
            hl = slice(hh * DVA, (hh + 1) * DVA)
            bias = _alibi_slope(jnp.full((1, 1), hg * hps + hh, jnp.int32)) * rel
            q = (q_ref[:, hl] * (DKA ** -0.5)).astype(BF16)
            k = k_ref[:, hl].astype(BF16)
            v = v_ref[:, hl].astype(BF16)
            for half in range(2):
                idx = 2 * hh + half
                sl = slice(half * DKA, (half + 1) * DKA)
                sc = _dot_nt(q[:, sl], k[:, sl]) + bias
                if diagonal:
                    sc = jnp.where(keep, sc, -1e30)
                m_prev = m_ref[idx]
                m_new = jnp.maximum(m_prev, jnp.max(sc, axis=-1, keepdims=True))
                alpha = jnp.exp(m_prev - m_new)
                p = jnp.exp(sc - jnp.concatenate([m_new] * (t // DVA), axis=-1))
                l_ref[idx] = alpha * l_ref[idx] + jnp.sum(p, axis=-1, keepdims=True)
                a_ref[idx] = alpha * a_ref[idx] + _dot(p.astype(BF16), v)
                m_ref[idx] = m_new

    @pl.when(j < i)
    def _():
        step(False)

    @pl.when(j == i)
    def _():
        step(True)
        lam = _lambda_full(lq1[...], lk1[...], lq2[...], lk2[...])
        for hh in range(hps):
            o = a_ref[2 * hh] / l_ref[2 * hh] - lam * (a_ref[2 * hh + 1] / l_ref[2 * hh + 1])
            o_ref[:, hh * DVA:(hh + 1) * DVA] = _subln(o, gain_ref[...])


def _dattn_prompt(z, ka, va, lams, subln_gain, batch, seq, t):
    nt = seq // t
    hps = 2
    pairs = [(i, j) for i in range(nt) for j in range(i + 1)]
    ii = jnp.asarray([p[0] for p in pairs], jnp.int32)
    jj = jnp.asarray([p[1] for p in pairs], jnp.int32)
    vec = pl.BlockSpec((1, DKA), lambda b, h, s, ii, jj: (0, 0))
    grid_spec = pltpu.PrefetchScalarGridSpec(
        num_scalar_prefetch=2,
        grid=(batch, HA // hps, len(pairs)),
        in_specs=[
            pl.BlockSpec((t, hps * DVA), lambda b, h, s, ii, jj: (b * nt + ii[s], h)),
            pl.BlockSpec((t, hps * DVA), lambda b, h, s, ii, jj: (b * nt + jj[s], h)),
            pl.BlockSpec((t, hps * DVA), lambda b, h, s, ii, jj: (b * nt + jj[s], h)),
            vec, vec, vec, vec,
            pl.BlockSpec((1, DVA), lambda b, h, s, ii, jj: (0, 0)),
        ],
        out_specs=pl.BlockSpec((t, hps * DVA), lambda b, h, s, ii, jj: (b * nt + ii[s], h)),
        scratch_shapes=[pltpu.VMEM((2 * hps, t, DVA), F32)] * 3,
    )
    return pl.pallas_call(
        functools.partial(_dattn_prompt_kernel, t=t, hps=hps),
        grid_spec=grid_spec,
        out_shape=jax.ShapeDtypeStruct((batch * seq, HA * DVA), F32),
        compiler_params=_cparams(("arbitrary", "arbitrary", "arbitrary")),
        name="dattn_prompt",
    )(ii, jj, z, ka, va, *[l.reshape(1, DKA) for l in lams], subln_gain.reshape(1, DVA))


def _rows_by_head(row, nrow):
    head = lax.broadcasted_iota(jnp.int32, (nrow, 1), 0) * HA // nrow
    out = jnp.zeros((nrow, DVA), F32)
    for h in range(HA):
        out = jnp.where(head == h, row[:, h * DVA:(h + 1) * DVA], out)
    return out


def _dattn_sample_kernel(pt_ref, q_ref, kn_ref, vn_ref, *rest, n_pages, page, pps):
    kp_refs = rest[:pps]
    vp_refs = rest[pps:2 * pps]
    lq1, lk1, lq2, lk2, gain_ref, o_ref, qm_ref, m_ref, l_ref, a_ref = rest[2 * pps:]
    p = pl.program_id(1)
    nrow = 2 * HA
    rid = lax.broadcasted_iota(jnp.int32, (nrow, 1), 0)
    slope = _alibi_slope(rid // 2)
    past = n_pages * page

    @pl.when(p == 0)
    def _():
        half = lax.broadcasted_iota(jnp.int32, (nrow, DVA), 1) // DKA
        qm_ref[...] = jnp.where(half == rid % 2, _rows_by_head(q_ref[0], nrow), 0.0)
        m_ref[...] = jnp.full(m_ref.shape, -jnp.inf, F32)
        l_ref[...] = jnp.zeros(l_ref.shape, F32)
        a_ref[...] = jnp.zeros(a_ref.shape, F32)

    qm = qm_ref[...]
    qmb = qm.astype(BF16)
    n = lax.broadcasted_iota(jnp.int32, (1, page * HA), 1)
    own_head = n % HA == rid // 2
    scores = []
    for r in range(pps):
        rel = (past - ((p * pps + r) * page + n // HA)).astype(F32)
        kf = kp_refs[r][...].reshape(page * HA, DVA).astype(BF16)
        sc = _dot_nt(qmb, kf) * (DKA ** -0.5) - slope * rel
        scores.append(jnp.where(own_head, sc, -1e30))
    m_new = m_ref[...]
    for sc in scores:
        m_new = jnp.maximum(m_new, jnp.max(sc, axis=-1, keepdims=True))
    alpha = jnp.exp(m_ref[...] - m_new)
    l_new = alpha * l_ref[...]
    a_new = alpha * a_ref[...]
    for r, sc in enumerate(scores):
        pr = jnp.exp(sc - m_new)
        l_new = l_new + jnp.sum(pr, axis=-1, keepdims=True)
        a_new = a_new + _dot(pr.astype(BF16), vp_refs[r][...].reshape(page * HA, DVA).astype(BF16))
    l_ref[...] = l_new
    a_ref[...] = a_new
    m_ref[...] = m_new

    @pl.when(p == n_pages // pps - 1)
    def _():
        sc_n = jnp.sum(qm * _rows_by_head(kn_ref[0], nrow), axis=-1, keepdims=True) * (DKA ** -0.5)
        m_f = jnp.maximum(m_ref[...], sc_n)
        al = jnp.exp(m_ref[...] - m_f)
        pn = jnp.exp(sc_n - m_f)
        l_f = al * l_ref[...] + pn
        a_f = (al * a_ref[...] + pn * _rows_by_head(vn_ref[0], nrow)) / l_f
        lam = _lambda_full(lq1[...], lk1[...], lq2[...], lk2[...])
        outs = []
        for h in range(HA):
            o = a_f[2 * h:2 * h + 1] - lam * a_f[2 * h + 1:2 * h + 2]
            outs.append(_subln(o, gain_ref[...]))
        o_ref[0] = jnp.concatenate(outs, axis=-1)


def _dattn_sample(z, ka, va, cache_k, cache_v, page_table, lams, subln_gain):
    nb, n_pages = page_table.shape
    page = cache_k.shape[1]
    w = HA * DVA
    row3 = lambda a: a.reshape(nb, 1, a.shape[-1])
    vec = pl.BlockSpec((1, DKA), lambda b, p, pt: (0, 0))
    pps = max(d for d in range(1, PAGES_PER_STEP + 1) if n_pages % d == 0)
    page_specs = [
        pl.BlockSpec((None, page, HA, DVA), lambda b, p, pt, r=r: (pt[b * n_pages + p * pps + r], 0, 0, 0))
        for r in range(pps)]
    grid_spec = pltpu.PrefetchScalarGridSpec(
        num_scalar_prefetch=1,
        grid=(nb, n_pages // pps),
        in_specs=[
            pl.BlockSpec((1, 1, w), lambda b, p, pt: (b, 0, 0)),
            pl.BlockSpec((1, 1, w), lambda b, p, pt: (b, 0, 0)),
            pl.BlockSpec((1, 1, w), lambda b, p, pt: (b, 0, 0)),
            *page_specs, *page_specs,
            vec, vec, vec, vec,
            pl.BlockSpec((1, DVA), lambda b, p, pt: (0, 0)),
        ],
        out_specs=pl.BlockSpec((1, 1, w), lambda b, p, pt: (b, 0, 0)),
        scratch_shapes=[pltpu.VMEM((2 * HA, DVA), F32), pltpu.VMEM((2 * HA, 1), F32),
                        pltpu.VMEM((2 * HA, 1), F32), pltpu.VMEM((2 * HA, DVA), F32)],
    )
    out = pl.pallas_call(
        functools.partial(_dattn_sample_kernel, n_pages=n_pages, page=page, pps=pps),
        grid_spec=grid_spec,
        out_shape=jax.ShapeDtypeStruct((nb, 1, w), F32),
        compiler_params=_cparams(("arbitrary", "arbitrary")),
        name="dattn_sample",
    )(page_table.reshape(-1), row3(z), row3(ka), row3(va), *([cache_k] * pps), *([cache_v] * pps),
      *[l.reshape(1, DKA) for l in lams], subln_gain.reshape(1, DVA))
    return out.reshape(nb, w)


def _ret_log_gamma(h):
    return math.log(1.0 - 2.0 ** (-5.0 - h))


def _ret_theta(width):
    theta = 1.0 / (ROPE_BASE ** jnp.linspace(0.0, 1.0, DKB // 2, dtype=F32))
    return jnp.tile(jnp.repeat(theta, 2), width // DKB).reshape(1, width)


def _rotate_pairs(x, cos, sin):
    n = x.shape[-1]
    lane = lax.broadcasted_iota(jnp.int32, x.shape, x.ndim - 1)
    nxt = pltpu.roll(x, n - 1, axis=x.ndim - 1)
    prv = pltpu.roll(x, 1, axis=x.ndim - 1)
    return x * cos + jnp.where(lane % 2 == 0, -nxt, prv) * sin


def _ret_prompt_kernel(q_ref, k_ref, v_ref, g_ref, th_ref, o_ref, s_ref, dec_ref, *, c):
    ci = pl.program_id(1)
    ri = lax.broadcasted_iota(jnp.int32, (c, c), 0)
    cj = lax.broadcasted_iota(jnp.int32, (c, c), 1)

    @pl.when(jnp.logical_and(pl.program_id(0) == 0, ci == 0))
    def _():
        diff = (ri - cj).astype(F32)
        for h in range(HB):
            dec_ref[h] = jnp.where(diff >= 0, jnp.exp(_ret_log_gamma(h) * jnp.maximum(diff, 0.0)), 0.0)

    @pl.when(ci == 0)
    def _():
        s_ref[...] = jnp.zeros(s_ref.shape, F32)

    idx = lax.broadcasted_iota(jnp.int32, (c, 1), 0)
    pos = (ci * c + idx).astype(F32)
    ang = pos * th_ref[...]
    cos = jnp.concatenate([jnp.cos(ang)] * (HB * DKB // 128), axis=-1)
    sin = jnp.concatenate([jnp.sin(ang)] * (HB * DKB // 128), axis=-1)
    qr = _rotate_pairs(q_ref[...], cos, sin)
    kr = _rotate_pairs(k_ref[...], cos, sin) * (DKB ** -0.5)
    idx_f = idx.astype(F32)
    outs = []
    for h in range(HB):
        lg = _ret_log_gamma(h)
        qh = qr[:, h * DKB:(h + 1) * DKB].astype(BF16)
        kh = kr[:, h * DKB:(h + 1) * DKB]
        vh = v_ref[:, h * DVB:(h + 1) * DVB].astype(BF16)
        s_old = s_ref[0, h]
        scores = _dot_nt(qh, kh.astype(BF16)) * dec_ref[h]
        o = _dot(scores.astype(BF16), vh)
        o = o + _dot(qh, s_old.astype(BF16)) * jnp.exp(lg * (idx_f + 1.0))
        kd = (kh * jnp.exp(lg * (c - 1.0 - idx_f))).astype(BF16)
        s_ref[0, h] = math.exp(lg * c) * s_old + _dot_tn(kd, vh)
        gate = g_ref[:, h * DVB:(h + 1) * DVB]
        outs.append(_rms(o) * (gate * _sigmoid(gate)))
    o_ref[...] = jnp.concatenate(outs, axis=-1)


def _ret_prompt(z, batch, seq, c):
    nc = seq // c
    wq = HB * DKB
    wv = HB * DVB
    return pl.pallas_call(
        functools.partial(_ret_prompt_kernel, c=c),
        grid=(batch, nc),
        in_specs=[
            pl.BlockSpec((c, wq), lambda b, i: (b * nc + i, 2)),
            pl.BlockSpec((c, wq), lambda b, i: (b * nc + i, 3)),
            pl.BlockSpec((c, wv), lambda b, i: (b * nc + i, 2)),
            pl.BlockSpec((c, wv), lambda b, i: (b * nc + i, 3)),
            pl.BlockSpec((1, 128), lambda b, i: (0, 0)),
        ],
        out_specs=[
            pl.BlockSpec((c, wv), lambda b, i: (b * nc + i, 0)),
            pl.BlockSpec((1, HB, DKB, DVB), lambda b, i: (b, 0, 0, 0)),
        ],
        out_shape=[
            jax.ShapeDtypeStruct((batch * seq, wv), F32),
            jax.ShapeDtypeStruct((batch, HB, DKB, DVB), F32),
        ],
        scratch_shapes=[pltpu.VMEM((HB, c, c), F32)],
        compiler_params=_cparams(("arbitrary", "arbitrary")),
        name="ret_prompt",
    )(z, z, z, z, _ret_theta(128))


def _ret_sample_kernel(qk_ref, v_ref, g_ref, th_ref, s_ref, o_ref, so_ref, *, pos):
    ang = float(pos) * th_ref[...]
    x = qk_ref[...]
    ncol = x.shape[1]
    d = lax.broadcasted_iota(jnp.int32, x.shape, 0)
    partner = jnp.where(d % 2 == 0, -pltpu.roll(x, DKB - 1, axis=0), pltpu.roll(x, 1, axis=0))
    qk = x * jnp.cos(ang) + partner * jnp.sin(ang)
    is_k = lax.broadcasted_iota(jnp.int32, (1, ncol), 1) % (2 * HB) >= HB
    qkb = (qk * jnp.where(is_k, DKB ** -0.5, 1.0)).astype(BF16)
    sel_row = lax.broadcasted_iota(jnp.int32, (ncol, HB * DVB), 0)
    sel_head = lax.broadcasted_iota(jnp.int32, (ncol, HB * DVB), 1) // DVB
    rows = []
    for s in range(v_ref.shape[0]):
        q_sel = jnp.where(sel_row == s * 2 * HB + sel_head, 1.0, 0.0)
        kv_sel = jnp.where(sel_row == s * 2 * HB + HB + sel_head, v_ref[s:s + 1, :], 0.0)
        res = _dot(qkb, jnp.concatenate([q_sel, kv_sel], axis=-1).astype(BF16))
        outs = []
        for h in range(HB):
            q_lanes = res[:, h * DVB:(h + 1) * DVB]
            outer = res[:, (HB + h) * DVB:(HB + h + 1) * DVB]
            s_new = math.exp(_ret_log_gamma(h)) * s_ref[s, h] + outer
            so_ref[s, h] = s_new
            o = jnp.sum(q_lanes * s_new, axis=0, keepdims=True)
            gate = g_ref[s:s + 1, h * DVB:(h + 1) * DVB]
            outs.append(_rms(o) * (gate * _sigmoid(gate)))
        rows.append(jnp.concatenate(outs, axis=-1))
    o_ref[...] = jnp.concatenate(rows, axis=0)


def _ret_sample(z, state, pos):
    nb = z.shape[0]
    wq = HB * DKB
    wv = HB * DVB
    sb = 8 if nb % 8 == 0 else nb
    qk_t = jnp.transpose(z[:, wv:wv + 2 * wq].reshape(nb, 2 * HB, DKB), (2, 0, 1)).reshape(DKB, nb * 2 * HB)
    theta_col = _ret_theta(DKB).reshape(DKB, 1)
    return pl.pallas_call(
        functools.partial(_ret_sample_kernel, pos=pos),
        grid=(nb // sb,),
        in_specs=[
            pl.BlockSpec((DKB, sb * 2 * HB), lambda b: (0, b)),
            pl.BlockSpec((sb, wv), lambda b: (b, 2)),
            pl.BlockSpec((sb, wv), lambda b: (b, 3)),
            pl.BlockSpec((DKB, 1), lambda b: (0, 0)),
            pl.BlockSpec((sb, HB, DKB, DVB), lambda b: (b, 0, 0, 0)),
        ],
        out_specs=[
            pl.BlockSpec((sb, wv), lambda b: (b, 0)),
            pl.BlockSpec((sb, HB, DKB, DVB), lambda b: (b, 0, 0, 0)),
        ],
        out_shape=[
            jax.ShapeDtypeStruct((nb, wv), F32),
            jax.ShapeDtypeStruct(state.shape, F32),
        ],
        compiler_params=_cparams(("arbitrary",)),
        name="ret_sample",
    )(qk_t, z, z, theta_col, state)


def _mm_resid_kernel(*refs, n_in):
    a_refs = refs[:n_in]
    w_refs = refs[n_in:2 * n_in]
    x_ref = refs[2 * n_in]
    o_ref = refs[2 * n_in + 1]
    acc = x_ref[...]
    for a_ref, w_ref in zip(a_refs, w_refs):
        acc = acc + _dot(a_ref[...].astype(BF16), w_ref[...])
    o_ref[...] = acc


def _mm_resid(a_list, w_bf16, resid, tm):
    m, n = resid.shape
    tn = n
    in_specs = []
    for a in a_list:
        in_specs.append(pl.BlockSpec((tm, a.shape[1]), lambda i, j: (i, 0)))
    row = 0
    for a in a_list:
        kb = a.shape[1]
        in_specs.append(pl.BlockSpec((kb, tn), lambda i, j, r=row // kb: (r, j)))
        row += kb
    in_specs.append(pl.BlockSpec((tm, tn), lambda i, j: (i, j)))
    return pl.pallas_call(
        functools.partial(_mm_resid_kernel, n_in=len(a_list)),
        grid=(m // tm, n // tn),
        in_specs=in_specs,
        out_specs=pl.BlockSpec((tm, tn), lambda i, j: (i, j)),
        out_shape=jax.ShapeDtypeStruct((m, n), F32),
        compiler_params=_cparams(("arbitrary", "arbitrary")),
        name="mm_resid",
    )(*a_list, *([w_bf16] * len(a_list)), resid)


def _ffn_kernel(*refs, final_norm, emit_bf16, aliased):
    x_ref, g_ref, wg_ref, wu_ref, wd_ref, fg_ref = refs[:6]
    refs = refs[6 + aliased:]
    if emit_bf16:
        o_ref, wgb_ref, wub_ref, wdb_ref, xn_ref, acc_ref = refs
    else:
        o_ref, xn_ref, acc_ref = refs
    j = pl.program_id(1)

    @pl.when(j == 0)
    def _():
        xn_ref[...] = (_rms(x_ref[...]) * g_ref[...]).astype(BF16)
        acc_ref[...] = jnp.zeros(acc_ref.shape, F32)

    wg, wu, wd = wg_ref[...].astype(BF16), wu_ref[...].astype(BF16), wd_ref[...].astype(BF16)
    if emit_bf16:
        wgb_ref[...], wub_ref[...], wdb_ref[...] = wg, wu, wd
    xn = xn_ref[...]
    gate = _dot(xn, wg)
    up = _dot(xn, wu)
    hid = (gate * _sigmoid(gate) * up).astype(BF16)
    acc_ref[...] += _dot(hid, wd)

    @pl.when(j == pl.num_programs(1) - 1)
    def _():
        y = x_ref[...] + acc_ref[...]
        if final_norm:
            y = _rms(y) * fg_ref[...]
        o_ref[...] = y


def _ffn_call(x, gain, wg, wu, wd, final_gain, final_norm, tm, tf, row_tiles, done=None):
    m = x.shape[0]
    dff = wg.shape[1]
    t0, t1 = row_tiles
    emit_bf16 = wg.dtype == F32
    w_specs = [pl.BlockSpec((D_MODEL, tf), lambda i, j: (0, j)),
               pl.BlockSpec((D_MODEL, tf), lambda i, j: (0, j)),
               pl.BlockSpec((tf, D_MODEL), lambda i, j: (j, 0))]
    in_specs = [pl.BlockSpec((tm, D_MODEL), lambda i, j: (i + t0, 0)),
                pl.BlockSpec((1, D_MODEL), lambda i, j: (0, 0)),
                *w_specs,
                pl.BlockSpec((1, D_MODEL), lambda i, j: (0, 0))]
    args = [x, gain.reshape(1, D_MODEL), wg, wu, wd, final_gain.reshape(1, D_MODEL)]
    out_specs = [pl.BlockSpec((tm, D_MODEL), lambda i, j: (i + t0, 0))]
    out_shape = [jax.ShapeDtypeStruct((m, D_MODEL), F32)]
    aliases = {}
    if done is not None:
        in_specs.append(pl.BlockSpec(memory_space=pl.ANY))
        args.append(done)
        aliases = {len(args) - 1: 0}
    if emit_bf16:
        out_specs += w_specs
        out_shape += [jax.ShapeDtypeStruct(w.shape, BF16) for w in (wg, wu, wd)]
    res = pl.pallas_call(
        functools.partial(_ffn_kernel, final_norm=final_norm, emit_bf16=emit_bf16, aliased=done is not None),
        grid=(t1 - t0, dff // tf),
        in_specs=in_specs,
        out_specs=out_specs,
        out_shape=out_shape,
        input_output_aliases=aliases,
        scratch_shapes=[pltpu.VMEM((tm, D_MODEL), BF16), pltpu.VMEM((tm, D_MODEL), F32)],
        compiler_params=_cparams(("arbitrary", "arbitrary")),
        name="ffn",
    )(*args)
    return res if emit_bf16 else res[0]


def _ffn(x, gain, weights, final_gain, final_norm, tm, tf=512):
    n_tiles = x.shape[0] // tm
    if weights[0].dtype != F32:
        return _ffn_call(x, gain, *weights, final_gain, final_norm, tm, tf, (0, n_tiles)), weights
    y, *weights_bf16 = _ffn_call(x, gain, *weights, final_gain, final_norm, tm, tf // 2, (0, 1))
    if n_tiles > 1:
        y = _ffn_call(x, gain, *weights_bf16, final_gain, final_norm, tm, tf, (1, n_tiles), done=y)
    return y, tuple(weights_bf16)


def _norm_and_shift(x_ref, g_ref, prev_ref, carry_ref, i, tiles_per_seq):
    h = _rms(x_ref[...]) * g_ref[...]
    if prev_ref is None:
        tm = h.shape[0]
        first = (i % tiles_per_seq) == 0
        prev_row = jnp.where(first, 0.0, carry_ref[0:1, :])
        rows = lax.broadcasted_iota(jnp.int32, (tm, 1), 0)
        shifted = jnp.where(rows == 0, prev_row, pltpu.roll(h, 1, axis=0))
        carry_ref[0:1, :] = h[tm - 1:tm, :]
    else:
        shifted = prev_ref[...]
    return h, shifted - h


def _rwkv_rkv_kernel(*refs, per_row_prev, tiles_per_seq):
    if per_row_prev:
        x_ref, g_ref, prev_ref, mu_ref, w_ref, o_ref, hl_ref, h_ref, xx_ref, carry_ref = refs
    else:
        x_ref, g_ref, mu_ref, w_ref, o_ref, hl_ref, h_ref, xx_ref, carry_ref = refs
        prev_ref = None
    i = pl.program_id(0)
    j = pl.program_id(1)

    @pl.when(j == 0)
    def _():
        h, xx = _norm_and_shift(x_ref, g_ref, prev_ref, carry_ref, i, tiles_per_seq)
        h_ref[...] = h
        xx_ref[...] = xx
        if per_row_prev:
            hl_ref[...] = h
        else:
            hl_ref[0] = h[h.shape[0] - 1:, :]

    xm = (h_ref[...] + xx_ref[...] * mu_ref[0]).astype(BF16)
    o_ref[...] = _dot(xm, w_ref[...])


def _rwkv_rkv(x, gain, prev, mu_rkv, w_rkv, tm, tiles_per_seq):
    m = x.shape[0]
    tn = 1024
    per_seq = D_MODEL // tn
    per_row_prev = prev is not None
    in_specs = [pl.BlockSpec((tm, D_MODEL), lambda i, j: (i, 0)),
                pl.BlockSpec((1, D_MODEL), lambda i, j: (0, 0))]
    args = [x, gain.reshape(1, D_MODEL)]
    if per_row_prev:
        in_specs.append(pl.BlockSpec((tm, D_MODEL), lambda i, j: (i, 0)))
        args.append(prev)
        hl_spec = pl.BlockSpec((tm, D_MODEL), lambda i, j: (i, 0))
        hl_shape = jax.ShapeDtypeStruct((m, D_MODEL), F32)
    else:
        nseq = m // (tm * tiles_per_seq)
        hl_spec = pl.BlockSpec((1, 1, D_MODEL), lambda i, j: (i // tiles_per_seq, 0, 0))
        hl_shape = jax.ShapeDtypeStruct((nseq, 1, D_MODEL), F32)
    in_specs += [pl.BlockSpec((1, 1, D_MODEL), lambda i, j: (j // per_seq, 0, 0)),
                 pl.BlockSpec((D_MODEL, tn), lambda i, j: (0, j))]
    args += [mu_rkv, w_rkv]
    return pl.pallas_call(
        functools.partial(_rwkv_rkv_kernel, per_row_prev=per_row_prev, tiles_per_seq=tiles_per_seq),
        grid=(m // tm, 3 * per_seq),
        in_specs=in_specs,
        out_specs=[pl.BlockSpec((tm, tn), lambda i, j: (i, j)), hl_spec],
        out_shape=[jax.ShapeDtypeStruct((m, 3 * D_MODEL), F32), hl_shape],
        scratch_shapes=[pltpu.VMEM((tm, D_MODEL), F32), pltpu.VMEM((tm, D_MODEL), F32),
                        pltpu.VMEM((8, D_MODEL), F32)],
        compiler_params=_cparams(("arbitrary", "arbitrary")),
        name="rwkv_rkv",
    )(*args)


def _rwkv_lora_kernel(*refs, per_row_prev, tiles_per_seq):
    if per_row_prev:
        (x_ref, g_ref, prev_ref, mu_ref, w0_ref, w1_ref, w2_ref, a0_ref, a1_ref, a2_ref, g1_ref, g2_ref,
         lw_ref, a_ref, go_ref, carry_ref) = refs
    else:
        (x_ref, g_ref, mu_ref, w0_ref, w1_ref, w2_ref, a0_ref, a1_ref, a2_ref, g1_ref, g2_ref,
         lw_ref, a_ref, go_ref, carry_ref) = refs
        prev_ref = None
    h, xx = _norm_and_shift(x_ref, g_ref, prev_ref, carry_ref, pl.program_id(0), tiles_per_seq)
    xw = (h + xx * mu_ref[0]).astype(BF16)
    xa = (h + xx * mu_ref[1]).astype(BF16)
    xg = (h + xx * mu_ref[2]).astype(BF16)
    zw = w0_ref[...] + _dot(jnp.tanh(_dot(xw, w1_ref[...])).astype(BF16), w2_ref[...])
    lw_ref[...] = -math.exp(-0.5) * _sigmoid(zw)
    a_ref[...] = _sigmoid(a0_ref[...] + _dot(_dot(xa, a1_ref[...]).astype(BF16), a2_ref[...]))
    go_ref[...] = _dot(_sigmoid(_dot(xg, g1_ref[...])).astype(BF16), g2_ref[...])


def _rwkv_lora(x, gain, prev, mu_wag, w0, w1, w2, a0, a1, a2, g1, g2, tm, tiles_per_seq):
    m = x.shape[0]
    per_row_prev = prev is not None
    full = lambda a: pl.BlockSpec(a.shape, lambda i: (0,) * a.ndim)
    row = pl.BlockSpec((tm, D_MODEL), lambda i: (i, 0))
    args = [x, gain.reshape(1, D_MODEL)]
    in_specs = [row, full(args[1])]
    if per_row_prev:
        in_specs.append(row)
        args.append(prev)
    rest = [mu_wag, w0.reshape(1, D_MODEL), w1, w2, a0.reshape(1, D_MODEL), a1, a2, g1, g2]
    in_specs += [full(a) for a in rest]
    args += rest
    return pl.pallas_call(
        functools.partial(_rwkv_lora_kernel, per_row_prev=per_row_prev, tiles_per_seq=tiles_per_seq),
        grid=(m // tm,),
        in_specs=in_specs,
        out_specs=[row, row, row],
        out_shape=[jax.ShapeDtypeStruct((m, D_MODEL), F32)] * 3,
        scratch_shapes=[pltpu.VMEM((8, D_MODEL), F32)],
        compiler_params=_cparams(("arbitrary",)),
        name="rwkv_lora",
    )(*args)


def _wkv_keys(k, a, kk_gain, ka_gain, head_sum):
    kk = k * kk_gain
    kk = kk / jnp.maximum(jnp.sqrt(head_sum(kk * kk)), 1e-12)
    return kk, kk * a, k * (1.0 + (a - 1.0) * ka_gain)


def _head_norm(o, head_sum):
    mean = head_sum(o) * (1.0 / NC)
    var = head_sum(jnp.square(o - mean)) * (1.0 / NC)
    return (o - mean) * lax.rsqrt(var + GN_EPS)


def _wkv_finish(on, r, k2, v, g, rk, ln_g, ln_b, head_sum):
    bonus = head_sum(r * k2 * rk) * v
    return (on * ln_g + ln_b + bonus) * g


def _pair_sum(x):
    lane = lax.broadcasted_iota(jnp.int32, x.shape, 1)
    s0 = jnp.sum(x[:, :NC], axis=-1, keepdims=True)
    s1 = jnp.sum(x[:, NC:], axis=-1, keepdims=True)
    return jnp.where(lane < NC, s0, s1)


def _bmm(a, b):
    return jnp.einsum("cij,cjk->cik", a, b, preferred_element_type=F32)


def _bmm_nt(a, b):
    return jnp.einsum("cik,cjk->cij", a, b, preferred_element_type=F32)


def _wkv_prompt_kernel(r_ref, k_ref, v_ref, lw_ref, a_ref, kkg_ref, kag_ref, rk_ref, g_ref, lng_ref, lnb_ref,
                       y_ref, so_ref, st_ref, gh_ref, eft_ref, wend_ref, bonus_ref, *, tc, nt):
    c = WKV_CHUNK
    nch = tc // c
    s = pl.program_id(0)

    @pl.when(s == 0)
    def _():
        for ref in (st_ref, gh_ref, eft_ref, wend_ref, bonus_ref):
            ref[...] = jnp.zeros(ref.shape, F32)

    prev = jnp.maximum(s - 1, 0)
    slot = prev % 2
    first = prev % nt == 0
    ss = [jnp.where(first, 0.0, st_ref[0]), jnp.where(first, 0.0, st_ref[1])]
    o_chunks = [[], []]
    lane_sum = lambda t: jnp.sum(t, axis=-1, keepdims=True)
    n_points = 2 + int(math.log2(c))
    done = [0]

    def advance_stage2(point):
        upto = nch * (point + 1) // n_points
        for ch in range(done[0], upto):
            w_end = wend_ref[slot, ch]
            for hh in range(2):
                n = hh * nch + ch
                gh = gh_ref[slot, n]
                eft = eft_ref[slot, n]
                sb = ss[hh].astype(BF16)
                o_chunks[hh].append(_head_norm(_dot_nt(gh[:, :NC].astype(BF16), sb) + gh[:, NC:], lane_sum))
                ss[hh] = (ss[hh] * w_end[:, hh * NC:(hh + 1) * NC]
                          + _dot(sb, eft[:NC].astype(BF16)) + eft[NC:])
        done[0] = upto

    ri = lax.broadcasted_iota(jnp.int32, (c, c), 0)
    cj = lax.broadcasted_iota(jnp.int32, (c, c), 1)
    tri = (ri >= cj).astype(BF16)
    gi = lax.broadcasted_iota(jnp.int32, (2 * c, 2 * c), 0)
    gj = lax.broadcasted_iota(jnp.int32, (2 * c, 2 * c), 1) % c
    gmask = (gj < jnp.where(gi < c, gi, gi - c + 1))[None]

    r = r_ref[...]
    v = v_ref[...]
    kk, bv, k2 = _wkv_keys(k_ref[...], a_ref[...], kkg_ref[...], kag_ref[...], _pair_sum)
    to3 = lambda x: x.reshape(nch, c, 2 * NC)
    lw3 = to3(lw_ref[...])
    p0 = lw3.astype(BF16)
    r1 = lw3 - p0.astype(F32)
    p1 = r1.astype(BF16)
    p2 = (r1 - p1.astype(F32)).astype(BF16)
    trib = jnp.broadcast_to(tri[None], (nch, c, c))
    cum = _bmm(trib, p0) + (_bmm(trib, p1) + _bmm(trib, p2))
    cum_last = cum[:, c - 1:c, :]
    w_rest = jnp.exp(cum_last - cum)
    w_inv = jnp.exp(-cum)
    am = -to3(kk) * jnp.exp(cum - lw3)
    rm = to3(r) * jnp.exp(cum)
    bm = (to3(bv) * w_inv).astype(BF16)
    km = (to3(k2) * w_inv).astype(BF16)
    bt = (to3(bv) * w_rest).astype(BF16)
    kt = (to3(k2) * w_rest).astype(BF16)
    v3 = to3(v)
    advance_stage2(0)

    heads = lambda x: jnp.concatenate([x[:, :, :NC], x[:, :, NC:]], axis=0)
    am_h, rm_h, v_h = heads(am), heads(rm), heads(v3)
    x1 = jnp.concatenate([am_h, rm_h], axis=1).astype(BF16)
    x2 = jnp.concatenate([heads(bm), heads(km)], axis=1)
    g4 = jnp.where(gmask, _bmm_nt(x1, x2), 0.0)
    lv = _bmm(g4[:, :, c:].astype(BF16), v_h.astype(BF16))
    advance_stage2(1)
    x = jnp.concatenate([am_h, lv[:, :c]], axis=-1)
    lp = g4[:, :c, :c].astype(BF16)
    steps = int(math.log2(c))
    for i in range(steps - 1):
        prod = _bmm(lp, jnp.concatenate([x.astype(BF16), lp], axis=-1))
        x = x + prod[:, :, :2 * NC]
        lp = prod[:, :, 2 * NC:].astype(BF16)
        advance_stage2(2 + i)
    x = x + _bmm(lp, x.astype(BF16))
    advance_stage2(n_points - 1)
    gh_new = _bmm(g4[:, c:, :c].astype(BF16), x.astype(BF16)) + jnp.concatenate([rm_h, lv[:, c:]], axis=-1)
    btk = jnp.concatenate([heads(bt), heads(kt)], axis=1)
    zv = jnp.concatenate([jnp.zeros_like(v_h), v_h], axis=-1)
    pqv = jnp.concatenate([x, zv], axis=1).astype(BF16)
    st_ref[0] = ss[0]
    st_ref[1] = ss[1]
    so_ref[0, 0] = ss[0]
    so_ref[0, 1] = ss[1]
    on = jnp.concatenate([jnp.concatenate(oc, axis=0) for oc in o_chunks], axis=-1)
    y_ref[...] = (on * lng_ref[...] + lnb_ref[...] + bonus_ref[slot]) * g_ref[...]

    nxt = s % 2
    gh_ref[nxt] = gh_new
    for n in range(2 * nch):
        eft_ref[nxt, n] = _dot_tn(pqv[n], btk[n])
    wend_ref[nxt] = jnp.exp(cum_last)
    bonus_ref[nxt] = _pair_sum(r * k2 * rk_ref[...]) * v


def _wkv_prompt(rkv, lw, a, g, kk_gain, ka_gain, rk, ln_g, ln_b, batch, seq, tc):
    nt = seq // tc
    nhp = HC // 2
    nblk = batch * nhp * nt
    nch = tc // WKV_CHUNK

    def rows_cols(blk):
        b = blk // (nhp * nt)
        return b * nt + blk % nt, (blk // nt) % nhp, b

    cur = lambda s: jnp.minimum(s, nblk - 1)
    prv = lambda s: jnp.maximum(s - 1, 0)
    tok = lambda off, which: pl.BlockSpec(
        (tc, 2 * NC), lambda s: (rows_cols(which(s))[0], off + rows_cols(which(s))[1]))
    par = lambda which: pl.BlockSpec((1, 2 * NC), lambda s: (0, rows_cols(which(s))[1]))
    return pl.pallas_call(
        functools.partial(_wkv_prompt_kernel, tc=tc, nt=nt),
        grid=(nblk + 1,),
        in_specs=[tok(0, cur), tok(nhp, cur), tok(2 * nhp, cur), tok(0, cur), tok(0, cur),
                  par(cur), par(cur), par(cur), tok(0, prv), par(prv), par(prv)],
        out_specs=[
            tok(0, prv),
            pl.BlockSpec((1, 2, NC, NC), lambda s: (rows_cols(prv(s))[2], rows_cols(prv(s))[1], 0, 0)),
        ],
        out_shape=[
            jax.ShapeDtypeStruct((batch * seq, D_MODEL), F32),
            jax.ShapeDtypeStruct((batch, HC, NC, NC), F32),
        ],
        scratch_shapes=[
            pltpu.VMEM((2, NC, NC), F32),
            pltpu.VMEM((2, 2 * nch, WKV_CHUNK, 2 * NC), F32),
            pltpu.VMEM((2, 2 * nch, 2 * NC, NC), F32),
            pltpu.VMEM((2, nch, 1, 2 * NC), F32),
            pltpu.VMEM((2, tc, 2 * NC), F32),
        ],
        compiler_params=_cparams(("arbitrary",)),
        name="wkv_prompt",
    )(rkv, rkv, rkv, lw, a, kk_gain, ka_gain, rk, g, ln_g, ln_b)


def _wkv_sample_kernel(r_ref, k_ref, v_ref, lw_ref, a_ref, g_ref, kkg_ref, kag_ref, rk_ref, lng_ref, lnb_ref,
                       s_ref, y_ref, so_ref, o_ref):
    ys = []
    col_sum = lambda x: jnp.sum(x, axis=0, keepdims=True)
    for hh in range(2):
        sl = slice(hh * NC, (hh + 1) * NC)
        t = lambda ref: ref[...].T[sl, :]
        r, k, v, a, g = t(r_ref), t(k_ref), t(v_ref), t(a_ref), t(g_ref)
        w = jnp.exp(t(lw_ref))
        kk, bv, k2 = _wkv_keys(k, a, kkg_ref[sl, :], kag_ref[sl, :], col_sum)
        for i in range(NC):
            s_old = s_ref[hh, i]
            sk = jnp.sum(s_old * kk, axis=0, keepdims=True)
            s_new = s_old * w - sk * bv + v[i:i + 1, :] * k2
            so_ref[hh, i] = s_new
            o_ref[i:i + 1, :] = jnp.sum(s_new * r, axis=0, keepdims=True)
        ys.append(_wkv_finish(_head_norm(o_ref[...], col_sum), r, k2, v, g,rk_ref[sl, :], lng_ref[sl, :], lnb_ref[sl, :], col_sum))
    y_ref[...] = jnp.concatenate(ys, axis=0).T


def _wkv_sample(rkv, lw, a, g, kk_gain, ka_gain, rk, ln_g, ln_b, state_t):
    nb = lw.shape[0]
    nhp = HC // 2
    tok = lambda off: pl.BlockSpec((nb, 2 * NC), lambda hp: (0, off + hp))
    par = pl.BlockSpec((2 * NC, 1), lambda hp: (hp, 0))
    col = lambda p: p.reshape(D_MODEL, 1)
    st_spec = pl.BlockSpec((2, NC, NC, nb), lambda hp: (hp, 0, 0, 0))
    return pl.pallas_call(
        _wkv_sample_kernel,
        grid=(nhp,),
        in_specs=[tok(0), tok(nhp), tok(2 * nhp), tok(0), tok(0), tok(0), par, par, par, par, par, st_spec],
        out_specs=[pl.BlockSpec((nb, 2 * NC), lambda hp: (0, hp)), st_spec],
        out_shape=[
            jax.ShapeDtypeStruct((nb, D_MODEL), F32),
            jax.ShapeDtypeStruct(state_t.shape, F32),
        ],
        scratch_shapes=[pltpu.VMEM((NC, nb), F32)],
        compiler_params=_cparams(("arbitrary",)),
        name="wkv_sample",
    )(rkv, rkv, rkv, lw, a, g, col(kk_gain), col(ka_gain), col(rk), col(ln_g), col(ln_b), state_t)


def _row_tile(m, want):
    return min(m, want)


def _trunk(x, batch, seq, past, W):
    m = x.shape[0]
    prompt = past is None
    tm = _row_tile(m, 512)
    lams = (W["lambda_q1"], W["lambda_k1"], W["lambda_q2"], W["lambda_k2"])

    ka, va, z = _inproj0(x, W["norm0_mix"], W["w_in0"], tm)
    if prompt:
        o_a = _dattn_prompt(z, ka, va, lams, W["subln_gain"], batch, seq, min(seq, 512))
        o_b, ret_new = _ret_prompt(z, batch, seq, min(seq, 256))
    else:
        o_a = _dattn_sample(z, ka, va, past["cache_k"], past["cache_v"], past["page_table"], lams,
                            W["subln_gain"])
        o_b, ret_new = _ret_sample(z, past["state_ret"], past["pos"])
    x = _mm_resid([o_a, o_b], W["w_out0"], x, _row_tile(m, 256))
    x, W["ffn0"] = _ffn(x, W["norm0_ffn"], W["ffn0"], W["norm_final"], False, tm)

    tiles_per_seq = max(seq // tm, 1)
    prev = None if prompt else past["state_shift"]
    rkv, hlast = _rwkv_rkv(x, W["norm1_mix"], prev, W["mu_rkv"], W["w_rkv"], tm, tiles_per_seq)
    tl = _row_tile(m, 256)
    lw, a, g = _rwkv_lora(x, W["norm1_mix"], prev, W["mu_wag"], W["decay_w0"], W["decay_w1"], W["decay_w2"],
                          W["aaa_a0"], W["aaa_a1"], W["aaa_a2"], W["gate_g1"], W["gate_g2"],
                          tl, max(seq // tl, 1))
    head_params = (W["k_k"], W["k_a"], W["r_k"], W["lnx_gain"], W["lnx_bias"])
    if prompt:
        y, wkv_new = _wkv_prompt(rkv, lw, a, g, *head_params, batch, seq, min(seq, 512))
        shift_new = hlast.reshape(batch, D_MODEL)
    else:
        state_t = jnp.transpose(past["state_wkv"], (1, 2, 3, 0))
        y, wkv_t = _wkv_sample(rkv, lw, a, g, *head_params, state_t)
        wkv_new = jnp.transpose(wkv_t, (3, 0, 1, 2))
        shift_new = hlast
    x = _mm_resid([y], W["w_out1"], x, _row_tile(m, 256))
    x, W["ffn1"] = _ffn(x, W["norm1_ffn"], W["ffn1"], W["norm_final"], True, tm)
    return x, ka, va, ret_new, wkv_new, shift_new


def kernel(x_prompt, x_sample, cache_k, cache_v, page_table, state_ret, state_wkv, state_shift, norm0_mix, w_in0, lambda_q1, lambda_k1, lambda_q2, lambda_k2, subln_gain, w_out0, norm0_ffn, w_gate0, w_up0, w_down0, norm1_mix, mu1, w_r1, w_k1, w_v1, decay_w0, decay_w1, decay_w2, aaa_a0, aaa_a1, aaa_a2, gate_g1, gate_g2, k_k, k_a, r_k, lnx_gain, lnx_bias, w_out1, norm1_ffn, w_gate1, w_up1, w_down1, norm_final):
    bf = lambda w: w.astype(BF16)
    row = lambda p: p.reshape(1, D_MODEL)
    W = dict(
        norm0_mix=norm0_mix, w_in0=bf(w_in0), lambda_q1=lambda_q1, lambda_k1=lambda_k1, lambda_q2=lambda_q2,
        lambda_k2=lambda_k2, subln_gain=subln_gain, w_out0=bf(w_out0), norm0_ffn=norm0_ffn,
        ffn0=(w_gate0, w_up0, w_down0),
        norm1_mix=norm1_mix,
        mu_rkv=jnp.stack([mu1[0], mu1[2], mu1[3]]).reshape(3, 1, D_MODEL),
        mu_wag=jnp.stack([mu1[1], mu1[4], mu1[5]]).reshape(3, 1, D_MODEL),
        w_rkv=jnp.concatenate([bf(w_r1), bf(w_k1), bf(w_v1)], axis=1),
        decay_w0=decay_w0, decay_w1=bf(decay_w1), decay_w2=bf(decay_w2),
        aaa_a0=aaa_a0, aaa_a1=bf(aaa_a1), aaa_a2=bf(aaa_a2), gate_g1=bf(gate_g1), gate_g2=bf(gate_g2),
        k_k=row(k_k), k_a=row(k_a), r_k=row(r_k), lnx_gain=row(lnx_gain), lnx_bias=row(lnx_bias),
        w_out1=bf(w_out1), norm1_ffn=norm1_ffn, ffn1=(w_gate1, w_up1, w_down1),
        norm_final=norm_final,
    )
    bsz, seq = x_prompt.shape[:2]
    y_p, k_p, v_p, ret_p, wkv_p, shift_p = _trunk(x_prompt.reshape(bsz * seq, D_MODEL), bsz, seq, None, W)
    nb, dseq = x_sample.shape[:2]
    past_len = page_table.shape[1] * cache_k.shape[1]
    past = dict(cache_k=cache_k, cache_v=cache_v, page_table=page_table, state_ret=state_ret,
                state_wkv=state_wkv, state_shift=state_shift, pos=past_len)
    y_s, k_s, v_s, ret_s, wkv_s, shift_s = _trunk(x_sample.reshape(nb * dseq, D_MODEL), nb, dseq, past, W)
    return (y_p.reshape(bsz, seq, D_MODEL), y_s.reshape(nb, dseq, D_MODEL),
            k_p.reshape(bsz, seq, HA, 2 * DKA), v_p.reshape(bsz, seq, HA, DVA),
            k_s.reshape(nb, dseq, HA, 2 * DKA), v_s.reshape(nb, dseq, HA, DVA),
            ret_p, ret_s, wkv_p, wkv_s, shift_p, shift_s)
```

```python
import functools
import math

import jax
import jax.numpy as jnp
from jax import lax
from jax.experimental import pallas as pl
from jax.experimental.pallas import tpu as pltpu

F32 = jnp.float32
BF16 = jnp.bfloat16

D_MODEL = 2048
EPS = 1e-6
DKA = 64
DVA = 128
HA = 8
LAMBDA_INIT = 0.8 - 0.6 * math.exp(-0.3 * 0)
DKB = 64
DVB = 128
HB = 8
ROPE_BASE = 10000.0
NC = 64
HC = D_MODEL // NC
GN_EPS = 64e-5
WKV_CHUNK = 64
PAGES_PER_STEP = 16
VMEM_LIMIT = 48 * 1024 * 1024


def _cparams(sem):
    return pltpu.CompilerParams(dimension_semantics=sem, vmem_limit_bytes=VMEM_LIMIT)


def _rms(x, eps=EPS):
    return x * lax.rsqrt(jnp.mean(x * x, axis=-1, keepdims=True) + eps)


def _sigmoid(x):
    return 1.0 / (1.0 + jnp.exp(-x))


def _dot(a, b):
    return jnp.dot(a, b, preferred_element_type=F32)


def _dot_nt(a, b):
    return lax.dot_general(a, b, (((1,), (1,)), ((), ())), preferred_element_type=F32)


def _dot_tn(a, b):
    return lax.dot_general(a, b, (((0,), (0,)), ((), ())), preferred_element_type=F32)


def _split2(x):
    hi = x.astype(BF16)
    lo = (x - hi.astype(F32)).astype(BF16)
    return hi, lo


def _dot_f32(a, b, dot=_dot):
    ah, al = _split2(a)
    bh, bl = _split2(b)
    return dot(ah, bh) + (dot(ah, bl) + dot(al, bh))


def _inproj0_kernel(x_ref, g_ref, w_ref, ka_ref, va_ref, z_ref, xn_ref):
    j = pl.program_id(1)

    @pl.when(j == 0)
    def _():
        xn_ref[...] = (_rms(x_ref[...]) * g_ref[...]).astype(BF16)

    acc = _dot(xn_ref[...], w_ref[...])

    @pl.when(j == 1)
    def _():
        ka_ref[...] = acc

    @pl.when(j == 2)
    def _():
        va_ref[...] = acc

    @pl.when(jnp.logical_and(j != 1, j != 2))
    def _():
        z_ref[...] = acc


def _inproj0(x, gain, w_bf16, tm):
    m = x.shape[0]
    tn = 1024
    nj = w_bf16.shape[1] // tn
    return pl.pallas_call(
        _inproj0_kernel,
        grid=(m // tm, nj),
        in_specs=[
            pl.BlockSpec((tm, D_MODEL), lambda i, j: (i, 0)),
            pl.BlockSpec((1, D_MODEL), lambda i, j: (0, 0)),
            pl.BlockSpec((D_MODEL, tn), lambda i, j: (0, j)),
        ],
        out_specs=[
            pl.BlockSpec((tm, tn), lambda i, j: (i, 0)),
            pl.BlockSpec((tm, tn), lambda i, j: (i, 0)),
            pl.BlockSpec((tm, tn), lambda i, j: (i, jnp.maximum(j - 2, 0))),
        ],
        out_shape=[
            jax.ShapeDtypeStruct((m, tn), F32),
            jax.ShapeDtypeStruct((m, tn), F32),
            jax.ShapeDtypeStruct((m, 4 * tn), F32),
        ],
        scratch_shapes=[pltpu.VMEM((tm, D_MODEL), BF16)],
        compiler_params=_cparams(("arbitrary", "arbitrary")),
        name="inproj0",
    )(x, gain.reshape(1, D_MODEL), w_bf16)


def _lambda_full(lq1, lk1, lq2, lk2):
    s1 = jnp.sum(lq1 * lk1, axis=-1, keepdims=True)
    s2 = jnp.sum(lq2 * lk2, axis=-1, keepdims=True)
    return jnp.exp(s1) - jnp.exp(s2) + LAMBDA_INIT


def _alibi_slope(h_vec):
    slope = jnp.zeros(h_vec.shape, F32)
    for h in range(HA):
        slope = jnp.where(h_vec == h, 2.0 ** (-(h + 1)), slope)
    return slope


def _subln(o, gain):
    return _rms(o) * gain * (1.0 - LAMBDA_INIT)


def _dattn_prompt_kernel(ii_ref, jj_ref, q_ref, k_ref, v_ref, lq1, lk1, lq2, lk2, gain_ref, o_ref,
                         m_ref, l_ref, a_ref, *, t, hps):
    hg = pl.program_id(1)
    s = pl.program_id(2)
    i = ii_ref[s]
    j = jj_ref[s]

    @pl.when(j == 0)
    def _():
        m_ref[...] = jnp.full(m_ref.shape, -jnp.inf, F32)
        l_ref[...] = jnp.zeros(l_ref.shape, F32)
        a_ref[...] = jnp.zeros(a_ref.shape, F32)

    def step(diagonal):
        col = lax.broadcasted_iota(jnp.int32, (1, t), 1)
        rel = ((j - i) * t + col - (t - 1)).astype(F32)
        if diagonal:
            keep = (lax.broadcasted_iota(jnp.int32, (t, t), 1) <= lax.broadcasted_iota(jnp.int32, (t, t), 0))
        for hh in range(hps):
            hl = slice(hh * DVA, (hh + 1) * DVA)
            bias = _alibi_slope(jnp.full((1, 1), hg * hps + hh, jnp.int32)) * rel
            q = (q_ref[:, hl] * (DKA ** -0.5)).astype(BF16)
            k = k_ref[:, hl].astype(BF16)
            v = v_ref[:, hl].astype(BF16)
            for half in range(2):
                idx = 2 * hh + half
                sl = slice(half * DKA, (half + 1) * DKA)
                sc = _dot_nt(q[:, sl], k[:, sl]) + bias
                if diagonal:
                    sc = jnp.where(keep, sc, -1e30)
                m_prev = m_ref[idx]
                m_new = jnp.maximum(m_prev, jnp.max(sc, axis=-1, keepdims=True))
                alpha = jnp.exp(m_prev - m_new)
                p = jnp.exp(sc - jnp.concatenate([m_new] * (t // DVA), axis=-1))
                l_ref[idx] = alpha * l_ref[idx] + jnp.sum(p, axis=-1, keepdims=True)
                a_ref[idx] = alpha * a_ref[idx] + _dot(p.astype(BF16), v)
                m_ref[idx] = m_new

    @pl.when(j < i)
    def _():
        step(False)

    @pl.when(j == i)
    def _():
        step(True)
        lam = _lambda_full(lq1[...], lk1[...], lq2[...], lk2[...])
        for hh in range(hps):
            o = a_ref[2 * hh] / l_ref[2 * hh] - lam * (a_ref[2 * hh + 1] / l_ref[2 * hh + 1])
            o_ref[:, hh * DVA:(hh + 1) * DVA] = _subln(o, gain_ref[...])


def _dattn_prompt(z, ka, va, lams, subln_gain, batch, seq, t):
    nt = seq // t
    hps = 2
    pairs = [(i, j) for i in range(nt) for j in range(i + 1)]
    ii = jnp.asarray([p[0] for p in pairs], jnp.int32)
    jj = jnp.asarray([p[1] for p in pairs], jnp.int32)
    vec = pl.BlockSpec((1, DKA), lambda b, h, s, ii, jj: (0, 0))
    grid_spec = pltpu.PrefetchScalarGridSpec(
        num_scalar_prefetch=2,
        grid=(batch, HA // hps, len(pairs)),
        in_specs=[
            pl.BlockSpec((t, hps * DVA), lambda b, h, s, ii, jj: (b * nt + ii[s], h)),
            pl.BlockSpec((t, hps * DVA), lambda b, h, s, ii, jj: (b * nt + jj[s], h)),
            pl.BlockSpec((t, hps * DVA), lambda b, h, s, ii, jj: (b * nt + jj[s], h)),
            vec, vec, vec, vec,
            pl.BlockSpec((1, DVA), lambda b, h, s, ii, jj: (0, 0)),
        ],
        out_specs=pl.BlockSpec((t, hps * DVA), lambda b, h, s, ii, jj: (b * nt + ii[s], h)),
        scratch_shapes=[pltpu.VMEM((2 * hps, t, DVA), F32)] * 3,
    )
    return pl.pallas_call(
        functools.partial(_dattn_prompt_kernel, t=t, hps=hps),
        grid_spec=grid_spec,
        out_shape=jax.ShapeDtypeStruct((batch * seq, HA * DVA), F32),
        compiler_params=_cparams(("arbitrary", "arbitrary", "arbitrary")),
        name="dattn_prompt",
    )(ii, jj, z, ka, va, *[l.reshape(1, DKA) for l in lams], subln_gain.reshape(1, DVA))


def _rows_by_head(row, nrow):
    head = lax.broadcasted_iota(jnp.int32, (nrow, 1), 0) * HA // nrow
    out = jnp.zeros((nrow, DVA), F32)
    for h in range(HA):
        out = jnp.where(head == h, row[:, h * DVA:(h + 1) * DVA], out)
    return out


def _dattn_sample_kernel(pt_ref, q_ref, kn_ref, vn_ref, *rest, n_pages, page, pps):
    kp_refs = rest[:pps]
    vp_refs = rest[pps:2 * pps]
    lq1, lk1, lq2, lk2, gain_ref, o_ref, qm_ref, m_ref, l_ref, a_ref = rest[2 * pps:]
    p = pl.program_id(1)
    nrow = 2 * HA
    rid = lax.broadcasted_iota(jnp.int32, (nrow, 1), 0)
    slope = _alibi_slope(rid // 2)
    past = n_pages * page

    @pl.when(p == 0)
    def _():
        half = lax.broadcasted_iota(jnp.int32, (nrow, DVA), 1) // DKA
        qm_ref[...] = jnp.where(half == rid % 2, _rows_by_head(q_ref[0], nrow), 0.0)
        m_ref[...] = jnp.full(m_ref.shape, -jnp.inf, F32)
        l_ref[...] = jnp.zeros(l_ref.shape, F32)
        a_ref[...] = jnp.zeros(a_ref.shape, F32)

    qm = qm_ref[...]
    qmb = qm.astype(BF16)
    n = lax.broadcasted_iota(jnp.int32, (1, page * HA), 1)
    own_head = n % HA == rid // 2
    scores = []
    for r in range(pps):
        rel = (past - ((p * pps + r) * page + n // HA)).astype(F32)
        kf = kp_refs[r][...].reshape(page * HA, DVA).astype(BF16)
        sc = _dot_nt(qmb, kf) * (DKA ** -0.5) - slope * rel
        scores.append(jnp.where(own_head, sc, -1e30))
    m_new = m_ref[...]
    for sc in scores:
        m_new = jnp.maximum(m_new, jnp.max(sc, axis=-1, keepdims=True))
    alpha = jnp.exp(m_ref[...] - m_new)
    l_new = alpha * l_ref[...]
    a_new = alpha * a_ref[...]
    for r, sc in enumerate(scores):
        pr = jnp.exp(sc - m_new)
        l_new = l_new + jnp.sum(pr, axis=-1, keepdims=True)
        a_new = a_new + _dot(pr.astype(BF16), vp_refs[r][...].reshape(page * HA, DVA).astype(BF16))
    l_ref[...] = l_new
    a_ref[...] = a_new
    m_ref[...] = m_new

    @pl.when(p == n_pages // pps - 1)
    def _():
        sc_n = jnp.sum(qm * _rows_by_head(kn_ref[0], nrow), axis=-1, keepdims=True) * (DKA ** -0.5)
        m_f = jnp.maximum(m_ref[...], sc_n)
        al = jnp.exp(m_ref[...] - m_f)
        pn = jnp.exp(sc_n - m_f)
        l_f = al * l_ref[...] + pn
        a_f = (al * a_ref[...] + pn * _rows_by_head(vn_ref[0], nrow)) / l_f
        lam = _lambda_full(lq1[...], lk1[...], lq2[...], lk2[...])
        outs = []
        for h in range(HA):
            o = a_f[2 * h:2 * h + 1] - lam * a_f[2 * h + 1:2 * h + 2]
            outs.append(_subln(o, gain_ref[...]))
        o_ref[0] = jnp.concatenate(outs, axis=-1)


def _dattn_sample(z, ka, va, cache_k, cache_v, page_table, lams, subln_gain):
    nb, n_pages = page_table.shape
    page = cache_k.shape[1]
    w = HA * DVA
    row3 = lambda a: a.reshape(nb, 1, a.shape[-1])
    vec = pl.BlockSpec((1, DKA), lambda b, p, pt: (0, 0))
    pps = max(d for d in range(1, PAGES_PER_STEP + 1) if n_pages % d == 0)
    page_specs = [
        pl.BlockSpec((None, page, HA, DVA), lambda b, p, pt, r=r: (pt[b * n_pages + p * pps + r], 0, 0, 0))
        for r in range(pps)]
    grid_spec = pltpu.PrefetchScalarGridSpec(
        num_scalar_prefetch=1,
        grid=(nb, n_pages // pps),
        in_specs=[
            pl.BlockSpec((1, 1, w), lambda b, p, pt: (b, 0, 0)),
            pl.BlockSpec((1, 1, w), lambda b, p, pt: (b, 0, 0)),
            pl.BlockSpec((1, 1, w), lambda b, p, pt: (b, 0, 0)),
            *page_specs, *page_specs,
            vec, vec, vec, vec,
            pl.BlockSpec((1, DVA), lambda b, p, pt: (0, 0)),
        ],
        out_specs=pl.BlockSpec((1, 1, w), lambda b, p, pt: (b, 0, 0)),
        scratch_shapes=[pltpu.VMEM((2 * HA, DVA), F32), pltpu.VMEM((2 * HA, 1), F32),
                        pltpu.VMEM((2 * HA, 1), F32), pltpu.VMEM((2 * HA, DVA), F32)],
    )
    out = pl.pallas_call(
        functools.partial(_dattn_sample_kernel, n_pages=n_pages, page=page, pps=pps),
        grid_spec=grid_spec,
        out_shape=jax.ShapeDtypeStruct((nb, 1, w), F32),
        compiler_params=_cparams(("arbitrary", "arbitrary")),
        name="dattn_sample",
    )(page_table.reshape(-1), row3(z), row3(ka), row3(va), *([cache_k] * pps), *([cache_v] * pps),
      *[l.reshape(1, DKA) for l in lams], subln_gain.reshape(1, DVA))
    return out.reshape(nb, w)


def _ret_log_gamma(h):
    return math.log(1.0 - 2.0 ** (-5.0 - h))


def _ret_theta(width):
    theta = 1.0 / (ROPE_BASE ** jnp.linspace(0.0, 1.0, DKB // 2, dtype=F32))
    return jnp.tile(jnp.repeat(theta, 2), width // DKB).reshape(1, width)


def _rotate_pairs(x, cos, sin):
    n = x.shape[-1]
    lane = lax.broadcasted_iota(jnp.int32, x.shape, x.ndim - 1)
    nxt = pltpu.roll(x, n - 1, axis=x.ndim - 1)
    prv = pltpu.roll(x, 1, axis=x.ndim - 1)
    return x * cos + jnp.where(lane % 2 == 0, -nxt, prv) * sin


def _ret_prompt_kernel(q_ref, k_ref, v_ref, g_ref, th_ref, o_ref, s_ref, dec_ref, *, c):
    ci = pl.program_id(1)
    ri = lax.broadcasted_iota(jnp.int32, (c, c), 0)
    cj = lax.broadcasted_iota(jnp.int32, (c, c), 1)

    @pl.when(jnp.logical_and(pl.program_id(0) == 0, ci == 0))
    def _():
        diff = (ri - cj).astype(F32)
        for h in range(HB):
            dec_ref[h] = jnp.where(diff >= 0, jnp.exp(_ret_log_gamma(h) * jnp.maximum(diff, 0.0)), 0.0)

    @pl.when(ci == 0)
    def _():
        s_ref[...] = jnp.zeros(s_ref.shape, F32)

    idx = lax.broadcasted_iota(jnp.int32, (c, 1), 0)
    pos = (ci * c + idx).astype(F32)
    ang = pos * th_ref[...]
    cos = jnp.concatenate([jnp.cos(ang)] * (HB * DKB // 128), axis=-1)
    sin = jnp.concatenate([jnp.sin(ang)] * (HB * DKB // 128), axis=-1)
    qr = _rotate_pairs(q_ref[...], cos, sin)
    kr = _rotate_pairs(k_ref[...], cos, sin) * (DKB ** -0.5)
    idx_f = idx.astype(F32)
    outs = []
    for h in range(HB):
        lg = _ret_log_gamma(h)
        qh = qr[:, h * DKB:(h + 1) * DKB].astype(BF16)
        kh = kr[:, h * DKB:(h + 1) * DKB]
        vh = v_ref[:, h * DVB:(h + 1) * DVB].astype(BF16)
        s_old = s_ref[0, h]
        scores = _dot_nt(qh, kh.astype(BF16)) * dec_ref[h]
        o = _dot(scores.astype(BF16), vh)
        o = o + _dot(qh, s_old.astype(BF16)) * jnp.exp(lg * (idx_f + 1.0))
        kd = (kh * jnp.exp(lg * (c - 1.0 - idx_f))).astype(BF16)
        s_ref[0, h] = math.exp(lg * c) * s_old + _dot_tn(kd, vh)
        gate = g_ref[:, h * DVB:(h + 1) * DVB]
        outs.append(_rms(o) * (gate * _sigmoid(gate)))
    o_ref[...] = jnp.concatenate(outs, axis=-1)


def _ret_prompt(z, batch, seq, c):
    nc = seq // c
    wq = HB * DKB
    wv = HB * DVB
    return pl.pallas_call(
        functools.partial(_ret_prompt_kernel, c=c),
        grid=(batch, nc),
        in_specs=[
            pl.BlockSpec((c, wq), lambda b, i: (b * nc + i, 2)),
            pl.BlockSpec((c, wq), lambda b, i: (b * nc + i, 3)),
            pl.BlockSpec((c, wv), lambda b, i: (b * nc + i, 2)),
            pl.BlockSpec((c, wv), lambda b, i: (b * nc + i, 3)),
            pl.BlockSpec((1, 128), lambda b, i: (0, 0)),
        ],
        out_specs=[
            pl.BlockSpec((c, wv), lambda b, i: (b * nc + i, 0)),
            pl.BlockSpec((1, HB, DKB, DVB), lambda b, i: (b, 0, 0, 0)),
        ],
        out_shape=[
            jax.ShapeDtypeStruct((batch * seq, wv), F32),
            jax.ShapeDtypeStruct((batch, HB, DKB, DVB), F32),
        ],
        scratch_shapes=[pltpu.VMEM((HB, c, c), F32)],
        compiler_params=_cparams(("arbitrary", "arbitrary")),
        name="ret_prompt",
    )(z, z, z, z, _ret_theta(128))


def _ret_sample_kernel(qk_ref, v_ref, g_ref, th_ref, s_ref, o_ref, so_ref, *, pos):
    ang = float(pos) * th_ref[...]
    x = qk_ref[...]
    ncol = x.shape[1]
    d = lax.broadcasted_iota(jnp.int32, x.shape, 0)
    partner = jnp.where(d % 2 == 0, -pltpu.roll(x, DKB - 1, axis=0), pltpu.roll(x, 1, axis=0))
    qk = x * jnp.cos(ang) + partner * jnp.sin(ang)
    is_k = lax.broadcasted_iota(jnp.int32, (1, ncol), 1) % (2 * HB) >= HB
    qkb = (qk * jnp.where(is_k, DKB ** -0.5, 1.0)).astype(BF16)
    sel_row = lax.broadcasted_iota(jnp.int32, (ncol, HB * DVB), 0)
    sel_head = lax.broadcasted_iota(jnp.int32, (ncol, HB * DVB), 1) // DVB
    rows = []
    for s in range(v_ref.shape[0]):
        q_sel = jnp.where(sel_row == s * 2 * HB + sel_head, 1.0, 0.0)
        kv_sel = jnp.where(sel_row == s * 2 * HB + HB + sel_head, v_ref[s:s + 1, :], 0.0)
        res = _dot(qkb, jnp.concatenate([q_sel, kv_sel], axis=-1).astype(BF16))
        outs = []
        for h in range(HB):
            q_lanes = res[:, h * DVB:(h + 1) * DVB]
            outer = res[:, (HB + h) * DVB:(HB + h + 1) * DVB]
            s_new = math.exp(_ret_log_gamma(h)) * s_ref[s, h] + outer
            so_ref[s, h] = s_new
            o = jnp.sum(q_lanes * s_new, axis=0, keepdims=True)
            gate = g_ref[s:s + 1, h * DVB:(h + 1) * DVB]
            outs.append(_rms(o) * (gate * _sigmoid(gate)))
        rows.append(jnp.concatenate(outs, axis=-1))
    o_ref[...] = jnp.concatenate(rows, axis=0)


def _ret_sample(z, state, pos):
    nb = z.shape[0]
    wq = HB * DKB
    wv = HB * DVB
    sb = 8 if nb % 8 == 0 else nb
    qk_t = jnp.transpose(z[:, wv:wv + 2 * wq].reshape(nb, 2 * HB, DKB), (2, 0, 1)).reshape(DKB, nb * 2 * HB)
    theta_col = _ret_theta(DKB).reshape(DKB, 1)
    return pl.pallas_call(
        functools.partial(_ret_sample_kernel, pos=pos),
        grid=(nb // sb,),
        in_specs=[
            pl.BlockSpec((DKB, sb * 2 * HB), lambda b: (0, b)),
            pl.BlockSpec((sb, wv), lambda b: (b, 2)),
            pl.BlockSpec((sb, wv), lambda b: (b, 3)),
            pl.BlockSpec((DKB, 1), lambda b: (0, 0)),
            pl.BlockSpec((sb, HB, DKB, DVB), lambda b: (b, 0, 0, 0)),
        ],
        out_specs=[
            pl.BlockSpec((sb, wv), lambda b: (b, 0)),
            pl.BlockSpec((sb, HB, DKB, DVB), lambda b: (b, 0, 0, 0)),
        ],
        out_shape=[
            jax.ShapeDtypeStruct((nb, wv), F32),
            jax.ShapeDtypeStruct(state.shape, F32),
        ],
        compiler_params=_cparams(("arbitrary",)),
        name="ret_sample",
    )(qk_t, z, z, theta_col, state)


def _mm_resid_kernel(*refs, n_in):
    a_refs = refs[:n_in]
    w_refs = refs[n_in:2 * n_in]
    x_ref = refs[2 * n_in]
    o_ref = refs[2 * n_in + 1]
    acc = x_ref[...]
    for a_ref, w_ref in zip(a_refs, w_refs):
        acc = acc + _dot(a_ref[...].astype(BF16), w_ref[...])
    o_ref[...] = acc


def _mm_resid(a_list, w_bf16, resid, tm):
    m, n = resid.shape
    tn = n
    in_specs = []
    for a in a_list:
        in_specs.append(pl.BlockSpec((tm, a.shape[1]), lambda i, j: (i, 0)))
    row = 0
    for a in a_list:
        kb = a.shape[1]
        in_specs.append(pl.BlockSpec((kb, tn), lambda i, j, r=row // kb: (r, j)))
        row += kb
    in_specs.append(pl.BlockSpec((tm, tn), lambda i, j: (i, j)))
    return pl.pallas_call(
        functools.partial(_mm_resid_kernel, n_in=len(a_list)),
        grid=(m // tm, n // tn),
        in_specs=in_specs,
        out_specs=pl.BlockSpec((tm, tn), lambda i, j: (i, j)),
        out_shape=jax.ShapeDtypeStruct((m, n), F32),
        compiler_params=_cparams(("arbitrary", "arbitrary")),
        name="mm_resid",
    )(*a_list, *([w_bf16] * len(a_list)), resid)


def _ffn_kernel(*refs, final_norm, emit_bf16, aliased):
    x_ref, g_ref, wg_ref, wu_ref, wd_ref, fg_ref = refs[:6]
    refs = refs[6 + aliased:]
    if emit_bf16:
        o_ref, wgb_ref, wub_ref, wdb_ref, xn_ref, acc_ref = refs
    else:
        o_ref, xn_ref, acc_ref = refs
    j = pl.program_id(1)

    @pl.when(j == 0)
    def _():
        xn_ref[...] = (_rms(x_ref[...]) * g_ref[...]).astype(BF16)
        acc_ref[...] = jnp.zeros(acc_ref.shape, F32)

    wg, wu, wd = wg_ref[...].astype(BF16), wu_ref[...].astype(BF16), wd_ref[...].astype(BF16)
    if emit_bf16:
        wgb_ref[...], wub_ref[...], wdb_ref[...] = wg, wu, wd
    xn = xn_ref[...]
    gate = _dot(xn, wg)
    up = _dot(xn, wu)
    hid = (gate * _sigmoid(gate) * up).astype(BF16)
    acc_ref[...] += _dot(hid, wd)

    @pl.when(j == pl.num_programs(1) - 1)
    def _():
        y = x_ref[...] + acc_ref[...]
        if final_norm:
            y = _rms(y) * fg_ref[...]
        o_ref[...] = y


def _ffn_call(x, gain, wg, wu, wd, final_gain, final_norm, tm, tf, row_tiles, done=None):
    m = x.shape[0]
    dff = wg.shape[1]
    t0, t1 = row_tiles
    emit_bf16 = wg.dtype == F32
    w_specs = [pl.BlockSpec((D_MODEL, tf), lambda i, j: (0, j)),
               pl.BlockSpec((D_MODEL, tf), lambda i, j: (0, j)),
               pl.BlockSpec((tf, D_MODEL), lambda i, j: (j, 0))]
    in_specs = [pl.BlockSpec((tm, D_MODEL), lambda i, j: (i + t0, 0)),
                pl.BlockSpec((1, D_MODEL), lambda i, j: (0, 0)),
                *w_specs,
                pl.BlockSpec((1, D_MODEL), lambda i, j: (0, 0))]
    args = [x, gain.reshape(1, D_MODEL), wg, wu, wd, final_gain.reshape(1, D_MODEL)]
    out_specs = [pl.BlockSpec((tm, D_MODEL), lambda i, j: (i + t0, 0))]
    out_shape = [jax.ShapeDtypeStruct((m, D_MODEL), F32)]
    aliases = {}
    if done is not None:
        in_specs.append(pl.BlockSpec(memory_space=pl.ANY))
        args.append(done)
        aliases = {len(args) - 1: 0}
    if emit_bf16:
        out_specs += w_specs
        out_shape += [jax.ShapeDtypeStruct(w.shape, BF16) for w in (wg, wu, wd)]
    res = pl.pallas_call(
        functools.partial(_ffn_kernel, final_norm=final_norm, emit_bf16=emit_bf16, aliased=done is not None),
        grid=(t1 - t0, dff // tf),
        in_specs=in_specs,
        out_specs=out_specs,
        out_shape=out_shape,
        input_output_aliases=aliases,
        scratch_shapes=[pltpu.VMEM((tm, D_MODEL), BF16), pltpu.VMEM((tm, D_MODEL), F32)],
        compiler_params=_cparams(("arbitrary", "arbitrary")),
        name="ffn",
    )(*args)
    return res if emit_bf16 else res[0]


def _ffn(x, gain, weights, final_gain, final_norm, tm, tf=512):
    n_tiles = x.shape[0] // tm
    if weights[0].dtype != F32:
        return _ffn_call(x, gain, *weights, final_gain, final_norm, tm, tf, (0, n_tiles)), weights
    y, *weights_bf16 = _ffn_call(x, gain, *weights, final_gain, final_norm, tm, tf // 2, (0, 1))
    if n_tiles > 1:
        y = _ffn_call(x, gain, *weights_bf16, final_gain, final_norm, tm, tf, (1, n_tiles), done=y)
    return y, tuple(weights_bf16)


def _norm_and_shift(x_ref, g_ref, prev_ref, carry_ref, i, tiles_per_seq):
    h = _rms(x_ref[...]) * g_ref[...]
    if prev_ref is None:
        tm = h.shape[0]
        first = (i % tiles_per_seq) == 0
        prev_row = jnp.where(first, 0.0, carry_ref[0:1, :])
        rows = lax.broadcasted_iota(jnp.int32, (tm, 1), 0)
        shifted = jnp.where(rows == 0, prev_row, pltpu.roll(h, 1, axis=0))
        carry_ref[0:1, :] = h[tm - 1:tm, :]
    else:
        shifted = prev_ref[...]
    return h, shifted - h


def _rwkv_rkv_kernel(*refs, per_row_prev, tiles_per_seq):
    if per_row_prev:
        x_ref, g_ref, prev_ref, mu_ref, w_ref, o_ref, hl_ref, h_ref, xx_ref, carry_ref = refs
    else:
        x_ref, g_ref, mu_ref, w_ref, o_ref, hl_ref, h_ref, xx_ref, carry_ref = refs
        prev_ref = None
    i = pl.program_id(0)
    j = pl.program_id(1)

    @pl.when(j == 0)
    def _():
        h, xx = _norm_and_shift(x_ref, g_ref, prev_ref, carry_ref, i, tiles_per_seq)
        h_ref[...] = h
        xx_ref[...] = xx
        if per_row_prev:
            hl_ref[...] = h
        else:
            hl_ref[0] = h[h.shape[0] - 1:, :]

    xm = (h_ref[...] + xx_ref[...] * mu_ref[0]).astype(BF16)
    o_ref[...] = _dot(xm, w_ref[...])


def _rwkv_rkv(x, gain, prev, mu_rkv, w_rkv, tm, tiles_per_seq):
    m = x.shape[0]
    tn = 1024
    per_seq = D_MODEL // tn
    per_row_prev = prev is not None
    in_specs = [pl.BlockSpec((tm, D_MODEL), lambda i, j: (i, 0)),
                pl.BlockSpec((1, D_MODEL), lambda i, j: (0, 0))]
    args = [x, gain.reshape(1, D_MODEL)]
    if per_row_prev:
        in_specs.append(pl.BlockSpec((tm, D_MODEL), lambda i, j: (i, 0)))
        args.append(prev)
        hl_spec = pl.BlockSpec((tm, D_MODEL), lambda i, j: (i, 0))
        hl_shape = jax.ShapeDtypeStruct((m, D_MODEL), F32)
    else:
        nseq = m // (tm * tiles_per_seq)
        hl_spec = pl.BlockSpec((1, 1, D_MODEL), lambda i, j: (i // tiles_per_seq, 0, 0))
        hl_shape = jax.ShapeDtypeStruct((nseq, 1, D_MODEL), F32)
    in_specs += [pl.BlockSpec((1, 1, D_MODEL), lambda i, j: (j // per_seq, 0, 0)),
                 pl.BlockSpec((None, D_MODEL, tn), lambda i, j: (j // per_seq, 0, j % per_seq))]
    args += [mu_rkv, w_rkv]
    return pl.pallas_call(
        functools.partial(_rwkv_rkv_kernel, per_row_prev=per_row_prev, tiles_per_seq=tiles_per_seq),
        grid=(m // tm, 3 * per_seq),
        in_specs=in_specs,
        out_specs=[pl.BlockSpec((tm, tn), lambda i, j: (i, j)), hl_spec],
        out_shape=[jax.ShapeDtypeStruct((m, 3 * D_MODEL), F32), hl_shape],
        scratch_shapes=[pltpu.VMEM((tm, D_MODEL), F32), pltpu.VMEM((tm, D_MODEL), F32),
                        pltpu.VMEM((8, D_MODEL), F32)],
        compiler_params=_cparams(("arbitrary", "arbitrary")),
        name="rwkv_rkv",
    )(*args)


def _rwkv_lora_kernel(*refs, per_row_prev, tiles_per_seq):
    if per_row_prev:
        (x_ref, g_ref, prev_ref, mu_ref, w0_ref, w1_ref, w2_ref, a0_ref, a1_ref, a2_ref, g1_ref, g2_ref,
         lw_ref, a_ref, go_ref, carry_ref) = refs
    else:
        (x_ref, g_ref, mu_ref, w0_ref, w1_ref, w2_ref, a0_ref, a1_ref, a2_ref, g1_ref, g2_ref,
         lw_ref, a_ref, go_ref, carry_ref) = refs
        prev_ref = None
    h, xx = _norm_and_shift(x_ref, g_ref, prev_ref, carry_ref, pl.program_id(0), tiles_per_seq)
    xw = (h + xx * mu_ref[0]).astype(BF16)
    xa = (h + xx * mu_ref[1]).astype(BF16)
    xg = (h + xx * mu_ref[2]).astype(BF16)
    zw = w0_ref[...] + _dot(jnp.tanh(_dot(xw, w1_ref[...])).astype(BF16), w2_ref[...])
    lw_ref[...] = -math.exp(-0.5) * _sigmoid(zw)
    a_ref[...] = _sigmoid(a0_ref[...] + _dot(_dot(xa, a1_ref[...]).astype(BF16), a2_ref[...]))
    go_ref[...] = _dot(_sigmoid(_dot(xg, g1_ref[...])).astype(BF16), g2_ref[...])


def _rwkv_lora(x, gain, prev, mu_wag, w0, w1, w2, a0, a1, a2, g1, g2, tm, tiles_per_seq):
    m = x.shape[0]
    per_row_prev = prev is not None
    full = lambda a: pl.BlockSpec(a.shape, lambda i: (0,) * a.ndim)
    row = pl.BlockSpec((tm, D_MODEL), lambda i: (i, 0))
    args = [x, gain.reshape(1, D_MODEL)]
    in_specs = [row, full(args[1])]
    if per_row_prev:
        in_specs.append(row)
        args.append(prev)
    rest = [mu_wag, w0.reshape(1, D_MODEL), w1, w2, a0.reshape(1, D_MODEL), a1, a2, g1, g2]
    in_specs += [full(a) for a in rest]
    args += rest
    return pl.pallas_call(
        functools.partial(_rwkv_lora_kernel, per_row_prev=per_row_prev, tiles_per_seq=tiles_per_seq),
        grid=(m // tm,),
        in_specs=in_specs,
        out_specs=[row, row, row],
        out_shape=[jax.ShapeDtypeStruct((m, D_MODEL), F32)] * 3,
        scratch_shapes=[pltpu.VMEM((8, D_MODEL), F32)],
        compiler_params=_cparams(("arbitrary",)),
        name="rwkv_lora",
    )(*args)


def _wkv_keys(k, a, kk_gain, ka_gain, head_sum):
    kk = k * kk_gain
    kk = kk / jnp.maximum(jnp.sqrt(head_sum(kk * kk)), 1e-12)
    return kk, kk * a, k * (1.0 + (a - 1.0) * ka_gain)


def _head_norm(o, head_sum):
    mean = head_sum(o) * (1.0 / NC)
    var = head_sum(jnp.square(o - mean)) * (1.0 / NC)
    return (o - mean) * lax.rsqrt(var + GN_EPS)


def _wkv_finish(on, r, k2, v, g, rk, ln_g, ln_b, head_sum):
    bonus = head_sum(r * k2 * rk) * v
    return (on * ln_g + ln_b + bonus) * g


def _pair_sum(x):
    lane = lax.broadcasted_iota(jnp.int32, x.shape, 1)
    s0 = jnp.sum(x[:, :NC], axis=-1, keepdims=True)
    s1 = jnp.sum(x[:, NC:], axis=-1, keepdims=True)
    return jnp.where(lane < NC, s0, s1)


def _bmm(a, b):
    return jnp.einsum("cij,cjk->cik", a, b, preferred_element_type=F32)


def _bmm_nt(a, b):
    return jnp.einsum("cik,cjk->cij", a, b, preferred_element_type=F32)


def _wkv_prompt_kernel(r_ref, k_ref, v_ref, lw_ref, a_ref, kkg_ref, kag_ref, rk_ref, g_ref, lng_ref, lnb_ref,
                       y_ref, so_ref, st_ref, gh_ref, ghb_ref, e0t_ref, ft_ref, wend_ref, bonus_ref, *, tc, nt):
    c = WKV_CHUNK
    nch = tc // c
    s = pl.program_id(0)

    @pl.when(s == 0)
    def _():
        for ref in (st_ref, gh_ref, ghb_ref, e0t_ref, ft_ref, wend_ref, bonus_ref):
            ref[...] = jnp.zeros(ref.shape, ref.dtype)

    prev = jnp.maximum(s - 1, 0)
    slot = prev % 2
    first = prev % nt == 0
    ss = [jnp.where(first, 0.0, st_ref[0]), jnp.where(first, 0.0, st_ref[1])]
    o_chunks = [[], []]
    lane_sum = lambda t: jnp.sum(t, axis=-1, keepdims=True)
    n_points = 2 + int(math.log2(c))
    done = [0]

    def advance_stage2(point):
        upto = nch * (point + 1) // n_points
        for ch in range(done[0], upto):
            w_end = wend_ref[slot, ch]
            for hh in range(2):
                n = hh * nch + ch
                other = slice((1 - hh) * NC, (2 - hh) * NC)
                sb = ss[hh].astype(BF16)
                o = _dot_nt(ghb_ref[slot, n], sb) + gh_ref[slot, n][:, other]
                o_chunks[hh].append(_head_norm(o, lane_sum))
                ss[hh] = ss[hh] * w_end + _dot(sb, e0t_ref[slot, n]) + ft_ref[slot, n]
        done[0] = upto

    ri = lax.broadcasted_iota(jnp.int32, (c, c), 0)
    cj = lax.broadcasted_iota(jnp.int32, (c, c), 1)
    tri = (ri >= cj).astype(BF16)
    gi = lax.broadcasted_iota(jnp.int32, (2 * c, 2 * c), 0)
    gj = lax.broadcasted_iota(jnp.int32, (2 * c, 2 * c), 1) % c
    gmask = (gj < jnp.where(gi < c, gi, gi - c + 1))[None]

    r = r_ref[...]
    v = v_ref[...]
    kk, bv, k2 = _wkv_keys(k_ref[...], a_ref[...], kkg_ref[...], kag_ref[...], _pair_sum)
    to3 = lambda x: x.reshape(nch, c, 2 * NC)
    lw3 = to3(lw_ref[...])
    p0 = lw3.astype(BF16)
    r1 = lw3 - p0.astype(F32)
    p1 = r1.astype(BF16)
    p2 = (r1 - p1.astype(F32)).astype(BF16)
    trib = jnp.broadcast_to(tri[None], (nch, c, c))
    cum = _bmm(trib, p0) + (_bmm(trib, p1) + _bmm(trib, p2))
    cum_last = cum[:, c - 1:c, :]
    w_rest = jnp.exp(cum_last - cum)
    w_inv = jnp.exp(-cum)
    am = -to3(kk) * jnp.exp(cum - lw3)
    rm = to3(r) * jnp.exp(cum)
    bm = (to3(bv) * w_inv).astype(BF16)
    km = (to3(k2) * w_inv).astype(BF16)
    bt = (to3(bv) * w_rest).astype(BF16)
    kt = (to3(k2) * w_rest).astype(BF16)
    v3 = to3(v)
    advance_stage2(0)

    both = lambda x: jnp.concatenate([x, x], axis=0)
    lane = lax.broadcasted_iota(jnp.int32, (2 * nch, 1, 2 * NC), 2)
    head = lax.broadcasted_iota(jnp.int32, (2 * nch, 1, 2 * NC), 0) // nch
    own = (lane // NC) == head
    v_sw = both(pltpu.roll(v, NC, axis=1).reshape(nch, c, 2 * NC))
    x1 = both(jnp.concatenate([am, rm], axis=1).astype(BF16))
    x2 = jnp.where(own, both(jnp.concatenate([bm, km], axis=1)), 0.0)
    g4 = jnp.where(gmask, _bmm_nt(x1, x2), 0.0)
    lv = _bmm(g4[:, :, c:].astype(BF16), v_sw.astype(BF16))
    advance_stage2(1)
    x = jnp.where(own, both(am), lv[:, :c])
    lp = g4[:, :c, :c].astype(BF16)
    steps = int(math.log2(c))
    for i in range(steps - 1):
        prod = _bmm(lp, jnp.concatenate([x.astype(BF16), lp], axis=-1))
        x = x + prod[:, :, :2 * NC]
        lp = prod[:, :, 2 * NC:].astype(BF16)
        advance_stage2(2 + i)
    x = x + _bmm(lp, x.astype(BF16))
    advance_stage2(n_points - 1)
    gh_new = _bmm(g4[:, c:, :c].astype(BF16), x.astype(BF16)) + jnp.where(own, both(rm), lv[:, c:])
    btk = jnp.where(own, both(jnp.concatenate([bt, kt], axis=1)), 0.0)
    pqv = jnp.concatenate([x, jnp.where(own, 0.0, v_sw)], axis=1).astype(BF16)
    st_ref[0] = ss[0]
    st_ref[1] = ss[1]
    so_ref[0, 0] = ss[0][:, :NC]
    so_ref[0, 1] = ss[1][:, NC:]
    on = jnp.concatenate([jnp.concatenate(oc, axis=0) for oc in o_chunks], axis=-1)
    y_ref[...] = (on * lng_ref[...] + lnb_ref[...] + bonus_ref[slot]) * g_ref[...]

    nxt = s % 2
    gh_ref[nxt] = gh_new
    ghb_ref[nxt] = gh_new.astype(BF16)
    for n in range(2 * nch):
        eft = _dot_tn(pqv[n], btk[n])
        e0t_ref[nxt, n] = eft.astype(BF16)
        ft_ref[nxt, n] = eft[NC:] if n < nch else eft[:NC]
    wend_ref[nxt] = jnp.exp(cum_last)
    bonus_ref[nxt] = _pair_sum(r * k2 * rk_ref[...]) * v


def _wkv_prompt(rkv, lw, a, g, kk_gain, ka_gain, rk, ln_g, ln_b, batch, seq, tc):
    nt = seq // tc
    nhp = HC // 2
    nblk = batch * nhp * nt
    nch = tc // WKV_CHUNK

    def rows_cols(blk):
        b = blk // (nhp * nt)
        return b * nt + blk % nt, (blk // nt) % nhp, b

    cur = lambda s: jnp.minimum(s, nblk - 1)
    prv = lambda s: jnp.maximum(s - 1, 0)
    tok = lambda off, which: pl.BlockSpec(
        (tc, 2 * NC), lambda s: (rows_cols(which(s))[0], off + rows_cols(which(s))[1]))
    par = lambda which: pl.BlockSpec((1, 2 * NC), lambda s: (0, rows_cols(which(s))[1]))
    return pl.pallas_call(
        functools.partial(_wkv_prompt_kernel, tc=tc, nt=nt),
        grid=(nblk + 1,),
        in_specs=[tok(0, cur), tok(nhp, cur), tok(2 * nhp, cur), tok(0, cur), tok(0, cur),
                  par(cur), par(cur), par(cur), tok(0, prv), par(prv), par(prv)],
        out_specs=[
            tok(0, prv),
            pl.BlockSpec((1, 2, NC, NC), lambda s: (rows_cols(prv(s))[2], rows_cols(prv(s))[1], 0, 0)),
        ],
        out_shape=[
            jax.ShapeDtypeStruct((batch * seq, D_MODEL), F32),
            jax.ShapeDtypeStruct((batch, HC, NC, NC), F32),
        ],
        scratch_shapes=[
            pltpu.VMEM((2, NC, 2 * NC), F32),
            pltpu.VMEM((2, 2 * nch, WKV_CHUNK, 2 * NC), F32),
            pltpu.VMEM((2, 2 * nch, WKV_CHUNK, 2 * NC), BF16),
            pltpu.VMEM((2, 2 * nch, 2 * NC, 2 * NC), BF16),
            pltpu.VMEM((2, 2 * nch, NC, 2 * NC), F32),
            pltpu.VMEM((2, nch, 1, 2 * NC), F32),
            pltpu.VMEM((2, tc, 2 * NC), F32),
        ],
        compiler_params=_cparams(("arbitrary",)),
        name="wkv_prompt",
    )(rkv, rkv, rkv, lw, a, kk_gain, ka_gain, rk, g, ln_g, ln_b)


def _wkv_sample_kernel(r_ref, k_ref, v_ref, lw_ref, a_ref, g_ref, kkg_ref, kag_ref, rk_ref, lng_ref, lnb_ref,
                       s_ref, y_ref, so_ref, o_ref):
    ys = []
    col_sum = lambda x: jnp.sum(x, axis=0, keepdims=True)
    for hh in range(2):
        sl = slice(hh * NC, (hh + 1) * NC)
        t = lambda ref: ref[...].T[sl, :]
        r, k, v, a, g = t(r_ref), t(k_ref), t(v_ref), t(a_ref), t(g_ref)
        w = jnp.exp(t(lw_ref))
        kk, bv, k2 = _wkv_keys(k, a, kkg_ref[sl, :], kag_ref[sl, :], col_sum)
        for i in range(NC):
            s_old = s_ref[hh, i]
            sk = jnp.sum(s_old * kk, axis=0, keepdims=True)
            s_new = s_old * w - sk * bv + v[i:i + 1, :] * k2
            so_ref[hh, i] = s_new
            o_ref[i:i + 1, :] = jnp.sum(s_new * r, axis=0, keepdims=True)
        ys.append(_wkv_finish(_head_norm(o_ref[...], col_sum), r, k2, v, g,rk_ref[sl, :], lng_ref[sl, :], lnb_ref[sl, :], col_sum))
    y_ref[...] = jnp.concatenate(ys, axis=0).T


def _wkv_sample(rkv, lw, a, g, kk_gain, ka_gain, rk, ln_g, ln_b, state_t):
    nb = lw.shape[0]
    nhp = HC // 2
    tok = lambda off: pl.BlockSpec((nb, 2 * NC), lambda hp: (0, off + hp))
    par = pl.BlockSpec((2 * NC, 1), lambda hp: (hp, 0))
    col = lambda p: p.reshape(D_MODEL, 1)
    st_spec = pl.BlockSpec((2, NC, NC, nb), lambda hp: (hp, 0, 0, 0))
    return pl.pallas_call(
        _wkv_sample_kernel,
        grid=(nhp,),
        in_specs=[tok(0), tok(nhp), tok(2 * nhp), tok(0), tok(0), tok(0), par, par, par, par, par, st_spec],
        out_specs=[pl.BlockSpec((nb, 2 * NC), lambda hp: (0, hp)), st_spec],
        out_shape=[
            jax.ShapeDtypeStruct((nb, D_MODEL), F32),
            jax.ShapeDtypeStruct(state_t.shape, F32),
        ],
        scratch_shapes=[pltpu.VMEM((NC, nb), F32)],
        compiler_params=_cparams(("arbitrary",)),
        name="wkv_sample",
    )(rkv, rkv, rkv, lw, a, g, col(kk_gain), col(ka_gain), col(rk), col(ln_g), col(ln_b), state_t)


def _row_tile(m, want):
    return min(m, want)


def _trunk(x, batch, seq, past, W):
    m = x.shape[0]
    prompt = past is None
    tm = _row_tile(m, 512)
    lams = (W["lambda_q1"], W["lambda_k1"], W["lambda_q2"], W["lambda_k2"])

    ka, va, z = _inproj0(x, W["norm0_mix"], W["w_in0"], tm)
    if prompt:
        o_a = _dattn_prompt(z, ka, va, lams, W["subln_gain"], batch, seq, min(seq, 512))
        o_b, ret_new = _ret_prompt(z, batch, seq, min(seq, 256))
    else:
        o_a = _dattn_sample(z, ka, va, past["cache_k"], past["cache_v"], past["page_table"], lams,
                            W["subln_gain"])
        o_b, ret_new = _ret_sample(z, past["state_ret"], past["pos"])
    x = _mm_resid([o_a, o_b], W["w_out0"], x, _row_tile(m, 256))
    x, W["ffn0"] = _ffn(x, W["norm0_ffn"], W["ffn0"], W["norm_final"], False, tm)

    tiles_per_seq = max(seq // tm, 1)
    prev = None if prompt else past["state_shift"]
    rkv, hlast = _rwkv_rkv(x, W["norm1_mix"], prev, W["mu_rkv"], W["w_rkv"], tm, tiles_per_seq)
    tl = _row_tile(m, 256)
    lw, a, g = _rwkv_lora(x, W["norm1_mix"], prev, W["mu_wag"], W["decay_w0"], W["decay_w1"], W["decay_w2"],
                          W["aaa_a0"], W["aaa_a1"], W["aaa_a2"], W["gate_g1"], W["gate_g2"],
                          tl, max(seq // tl, 1))
    head_params = (W["k_k"], W["k_a"], W["r_k"], W["lnx_gain"], W["lnx_bias"])
    if prompt:
        y, wkv_new = _wkv_prompt(rkv, lw, a, g, *head_params, batch, seq, min(seq, 512))
        shift_new = hlast.reshape(batch, D_MODEL)
    else:
        state_t = jnp.transpose(past["state_wkv"], (1, 2, 3, 0))
        y, wkv_t = _wkv_sample(rkv, lw, a, g, *head_params, state_t)
        wkv_new = jnp.transpose(wkv_t, (3, 0, 1, 2))
        shift_new = hlast
    x = _mm_resid([y], W["w_out1"], x, _row_tile(m, 256))
    x, W["ffn1"] = _ffn(x, W["norm1_ffn"], W["ffn1"], W["norm_final"], True, tm)
    return x, ka, va, ret_new, wkv_new, shift_new


def kernel(x_prompt, x_sample, cache_k, cache_v, page_table, state_ret, state_wkv, state_shift, norm0_mix, w_in0, lambda_q1, lambda_k1, lambda_q2, lambda_k2, subln_gain, w_out0, norm0_ffn, w_gate0, w_up0, w_down0, norm1_mix, mu1, w_r1, w_k1, w_v1, decay_w0, decay_w1, decay_w2, aaa_a0, aaa_a1, aaa_a2, gate_g1, gate_g2, k_k, k_a, r_k, lnx_gain, lnx_bias, w_out1, norm1_ffn, w_gate1, w_up1, w_down1, norm_final):
    bf = lambda w: w.astype(BF16)
    row = lambda p: p.reshape(1, D_MODEL)
    W = dict(
        norm0_mix=norm0_mix, w_in0=bf(w_in0), lambda_q1=lambda_q1, lambda_k1=lambda_k1, lambda_q2=lambda_q2,
        lambda_k2=lambda_k2, subln_gain=subln_gain, w_out0=bf(w_out0), norm0_ffn=norm0_ffn,
        ffn0=(w_gate0, w_up0, w_down0),
        norm1_mix=norm1_mix,
        mu_rkv=jnp.stack([mu1[0], mu1[2], mu1[3]]).reshape(3, 1, D_MODEL),
        mu_wag=jnp.stack([mu1[1], mu1[4], mu1[5]]).reshape(3, 1, D_MODEL),
        w_rkv=jnp.stack([bf(w_r1), bf(w_k1), bf(w_v1)]),
        decay_w0=decay_w0, decay_w1=bf(decay_w1), decay_w2=bf(decay_w2),
        aaa_a0=aaa_a0, aaa_a1=bf(aaa_a1), aaa_a2=bf(aaa_a2), gate_g1=bf(gate_g1), gate_g2=bf(gate_g2),
        k_k=row(k_k), k_a=row(k_a), r_k=row(r_k), lnx_gain=row(lnx_gain), lnx_bias=row(lnx_bias),
        w_out1=bf(w_out1), norm1_ffn=norm1_ffn, ffn1=(w_gate1, w_up1, w_down1),
        norm_final=norm_final,
    )
    bsz, seq = x_prompt.shape[:2]
    y_p, k_p, v_p, ret_p, wkv_p, shift_p = _trunk(x_prompt.reshape(bsz * seq, D_MODEL), bsz, seq, None, W)
    nb, dseq = x_sample.shape[:2]
    past_len = page_table.shape[1] * cache_k.shape[1]
    past = dict(cache_k=cache_k, cache_v=cache_v, page_table=page_table, state_ret=state_ret,
                state_wkv=state_wkv, state_shift=state_shift, pos=past_len)
    y_s, k_s, v_s, ret_s, wkv_s, shift_s = _trunk(x_sample.reshape(nb * dseq, D_MODEL), nb, dseq, past, W)
    return (y_p.reshape(bsz, seq, D_MODEL), y_s.reshape(nb, dseq, D_MODEL),
            k_p.reshape(bsz, seq, HA, 2 * DKA), v_p.reshape(bsz, seq, HA, DVA),
            k_s.reshape(nb, dseq, HA, 2 * DKA), v_s.reshape(nb, dseq, HA, DVA),
            ret_p, ret_s, wkv_p, wkv_s, shift_p, shift_s)
```

```python
import functools
import math

import jax
import jax.numpy as jnp
from jax import lax
from jax.experimental import pallas as pl
from jax.experimental.pallas import tpu as pltpu

F32 = jnp.float32
BF16 = jnp.bfloat16

D_MODEL = 2048
EPS = 1e-6
DKA = 64
DVA = 128
HA = 8
LAMBDA_INIT = 0.8 - 0.6 * math.exp(-0.3 * 0)
DKB = 64
DVB = 128
HB = 8
ROPE_BASE = 10000.0
NC = 64
HC = D_MODEL // NC
GN_EPS = 64e-5
WKV_CHUNK = 64
PAGES_PER_STEP = 16
VMEM_LIMIT = 48 * 1024 * 1024


def _cparams(sem):
    return pltpu.CompilerParams(dimension_semantics=sem, vmem_limit_bytes=VMEM_LIMIT)


def _rms(x, eps=EPS):
    return x * lax.rsqrt(jnp.mean(x * x, axis=-1, keepdims=True) + eps)


def _sigmoid(x):
    return 1.0 / (1.0 + jnp.exp(-x))


def _dot(a, b):
    return jnp.dot(a, b, preferred_element_type=F32)


def _dot_nt(a, b):
    return lax.dot_general(a, b, (((1,), (1,)), ((), ())), preferred_element_type=F32)


def _dot_tn(a, b):
    return lax.dot_general(a, b, (((0,), (0,)), ((), ())), preferred_element_type=F32)


def _split2(x):
    hi = x.astype(BF16)
    lo = (x - hi.astype(F32)).astype(BF16)
    return hi, lo


def _dot_f32(a, b, dot=_dot):
    ah, al = _split2(a)
    bh, bl = _split2(b)
    return dot(ah, bh) + (dot(ah, bl) + dot(al, bh))


def _inproj0_kernel(x_ref, g_ref, w_ref, ka_ref, va_ref, z_ref, xn_ref):
    j = pl.program_id(1)

    @pl.when(j == 0)
    def _():
        xn_ref[...] = (_rms(x_ref[...]) * g_ref[...]).astype(BF16)

    acc = _dot(xn_ref[...], w_ref[...])

    @pl.when(j == 1)
    def _():
        ka_ref[...] = acc

    @pl.when(j == 2)
    def _():
        va_ref[...] = acc

    @pl.when(jnp.logical_and(j != 1, j != 2))
    def _():
        z_ref[...] = acc


def _inproj0(x, gain, w_bf16, tm):
    m = x.shape[0]
    tn = 1024
    nj = w_bf16.shape[1] // tn
    return pl.pallas_call(
        _inproj0_kernel,
        grid=(m // tm, nj),
        in_specs=[
            pl.BlockSpec((tm, D_MODEL), lambda i, j: (i, 0)),
            pl.BlockSpec((1, D_MODEL), lambda i, j: (0, 0)),
            pl.BlockSpec((D_MODEL, tn), lambda i, j: (0, j)),
        ],
        out_specs=[
            pl.BlockSpec((tm, tn), lambda i, j: (i, 0)),
            pl.BlockSpec((tm, tn), lambda i, j: (i, 0)),
            pl.BlockSpec((tm, tn), lambda i, j: (i, jnp.maximum(j - 2, 0))),
        ],
        out_shape=[
            jax.ShapeDtypeStruct((m, tn), F32),
            jax.ShapeDtypeStruct((m, tn), F32),
            jax.ShapeDtypeStruct((m, 4 * tn), F32),
        ],
        scratch_shapes=[pltpu.VMEM((tm, D_MODEL), BF16)],
        compiler_params=_cparams(("arbitrary", "arbitrary")),
        name="inproj0",
    )(x, gain.reshape(1, D_MODEL), w_bf16)


def _lambda_full(lq1, lk1, lq2, lk2):
    s1 = jnp.sum(lq1 * lk1, axis=-1, keepdims=True)
    s2 = jnp.sum(lq2 * lk2, axis=-1, keepdims=True)
    return jnp.exp(s1) - jnp.exp(s2) + LAMBDA_INIT


def _alibi_slope(h_vec):
    slope = jnp.zeros(h_vec.shape, F32)
    for h in range(HA):
        slope = jnp.where(h_vec == h, 2.0 ** (-(h + 1)), slope)
    return slope


def _subln(o, gain):
    return _rms(o) * gain * (1.0 - LAMBDA_INIT)


def _dattn_prompt_kernel(ii_ref, jj_ref, q_ref, k_ref, v_ref, lq1, lk1, lq2, lk2, gain_ref, o_ref,
                         m_ref, l_ref, a_ref, *, t, hps):
    hg = pl.program_id(1)
    s = pl.program_id(2)
    i = ii_ref[s]
    j = jj_ref[s]

    @pl.when(j == 0)
    def _():
        m_ref[...] = jnp.full(m_ref.shape, -jnp.inf, F32)
        l_ref[...] = jnp.zeros(l_ref.shape, F32)
        a_ref[...] = jnp.zeros(a_ref.shape, F32)

    def step(diagonal):
        col = lax.broadcasted_iota(jnp.int32, (1, t), 1)
        rel = ((j - i) * t + col - (t - 1)).astype(F32)
        if diagonal:
            keep = (lax.broadcasted_iota(jnp.int32, (t, t), 1) <= lax.broadcasted_iota(jnp.int32, (t, t), 0))
        for hh in range(hps):
            hl = slice(hh * DVA, (hh + 1) * DVA)
            bias = _alibi_slope(jnp.full((1, 1), hg * hps + hh, jnp.int32)) * rel
            q = (q_ref[:, hl] * (DKA ** -0.5)).astype(BF16)
            k = k_ref[:, hl].astype(BF16)
            v = v_ref[:, hl].astype(BF16)
            for half in range(2):
                idx = 2 * hh + half
                sl = slice(half * DKA, (half + 1) * DKA)
                sc = _dot_nt(q[:, sl], k[:, sl]) + bias
                if diagonal:
                    sc = jnp.where(keep, sc, -1e30)
                m_prev = m_ref[idx]
                m_new = jnp.maximum(m_prev, jnp.max(sc, axis=-1, keepdims=True))
                alpha = jnp.exp(m_prev - m_new)
                p = jnp.exp(sc - jnp.concatenate([m_new] * (t // DVA), axis=-1))
                l_ref[idx] = alpha * l_ref[idx] + jnp.sum(p, axis=-1, keepdims=True)
                a_ref[idx] = alpha * a_ref[idx] + _dot(p.astype(BF16), v)
                m_ref[idx] = m_new

    @pl.when(j < i)
    def _():
        step(False)

    @pl.when(j == i)
    def _():
        step(True)
        lam = _lambda_full(lq1[...], lk1[...], lq2[...], lk2[...])
        for hh in range(hps):
            o = a_ref[2 * hh] / l_ref[2 * hh] - lam * (a_ref[2 * hh + 1] / l_ref[2 * hh + 1])
            o_ref[:, hh * DVA:(hh + 1) * DVA] = _subln(o, gain_ref[...])


def _dattn_prompt(z, ka, va, lams, subln_gain, batch, seq, t):
    nt = seq // t
    hps = 2
    pairs = [(i, j) for i in range(nt) for j in range(i + 1)]
    ii = jnp.asarray([p[0] for p in pairs], jnp.int32)
    jj = jnp.asarray([p[1] for p in pairs], jnp.int32)
    vec = pl.BlockSpec((1, DKA), lambda b, h, s, ii, jj: (0, 0))
    grid_spec = pltpu.PrefetchScalarGridSpec(
        num_scalar_prefetch=2,
        grid=(batch, HA // hps, len(pairs)),
        in_specs=[
            pl.BlockSpec((t, hps * DVA), lambda b, h, s, ii, jj: (b * nt + ii[s], h)),
            pl.BlockSpec((t, hps * DVA), lambda b, h, s, ii, jj: (b * nt + jj[s], h)),
            pl.BlockSpec((t, hps * DVA), lambda b, h, s, ii, jj: (b * nt + jj[s], h)),
            vec, vec, vec, vec,
            pl.BlockSpec((1, DVA), lambda b, h, s, ii, jj: (0, 0)),
        ],
        out_specs=pl.BlockSpec((t, hps * DVA), lambda b, h, s, ii, jj: (b * nt + ii[s], h)),
        scratch_shapes=[pltpu.VMEM((2 * hps, t, DVA), F32)] * 3,
    )
    return pl.pallas_call(
        functools.partial(_dattn_prompt_kernel, t=t, hps=hps),
        grid_spec=grid_spec,
        out_shape=jax.ShapeDtypeStruct((batch * seq, HA * DVA), F32),
        compiler_params=_cparams(("arbitrary", "arbitrary", "arbitrary")),
        name="dattn_prompt",
    )(ii, jj, z, ka, va, *[l.reshape(1, DKA) for l in lams], subln_gain.reshape(1, DVA))


def _rows_by_head(row, nrow):
    head = lax.broadcasted_iota(jnp.int32, (nrow, 1), 0) * HA // nrow
    out = jnp.zeros((nrow, DVA), F32)
    for h in range(HA):
        out = jnp.where(head == h, row[:, h * DVA:(h + 1) * DVA], out)
    return out


def _dattn_sample_kernel(pt_ref, q_ref, kn_ref, vn_ref, *rest, n_pages, page, pps):
    kp_refs = rest[:pps]
    vp_refs = rest[pps:2 * pps]
    lq1, lk1, lq2, lk2, gain_ref, o_ref, qm_ref, m_ref, l_ref, a_ref = rest[2 * pps:]
    p = pl.program_id(1)
    nrow = 2 * HA
    rid = lax.broadcasted_iota(jnp.int32, (nrow, 1), 0)
    slope = _alibi_slope(rid // 2)
    past = n_pages * page

    @pl.when(p == 0)
    def _():
        half = lax.broadcasted_iota(jnp.int32, (nrow, DVA), 1) // DKA
        qm_ref[...] = jnp.where(half == rid % 2, _rows_by_head(q_ref[0], nrow), 0.0)
        m_ref[...] = jnp.full(m_ref.shape, -jnp.inf, F32)
        l_ref[...] = jnp.zeros(l_ref.shape, F32)
        a_ref[...] = jnp.zeros(a_ref.shape, F32)

    qm = qm_ref[...]
    qmb = qm.astype(BF16)
    n = lax.broadcasted_iota(jnp.int32, (1, page * HA), 1)
    own_head = n % HA == rid // 2
    scores = []
    for r in range(pps):
        rel = (past - ((p * pps + r) * page + n // HA)).astype(F32)
        kf = kp_refs[r][...].reshape(page * HA, DVA).astype(BF16)
        sc = _dot_nt(qmb, kf) * (DKA ** -0.5) - slope * rel
        scores.append(jnp.where(own_head, sc, -1e30))
    m_new = m_ref[...]
    for sc in scores:
        m_new = jnp.maximum(m_new, jnp.max(sc, axis=-1, keepdims=True))
    alpha = jnp.exp(m_ref[...] - m_new)
    l_new = alpha * l_ref[...]
    a_new = alpha * a_ref[...]
    for r, sc in enumerate(scores):
        pr = jnp.exp(sc - m_new)
        l_new = l_new + jnp.sum(pr, axis=-1, keepdims=True)
        a_new = a_new + _dot(pr.astype(BF16), vp_refs[r][...].reshape(page * HA, DVA).astype(BF16))
    l_ref[...] = l_new
    a_ref[...] = a_new
    m_ref[...] = m_new

    @pl.when(p == n_pages // pps - 1)
    def _():
        sc_n = jnp.sum(qm * _rows_by_head(kn_ref[0], nrow), axis=-1, keepdims=True) * (DKA ** -0.5)
        m_f = jnp.maximum(m_ref[...], sc_n)
        al = jnp.exp(m_ref[...] - m_f)
        pn = jnp.exp(sc_n - m_f)
        l_f = al * l_ref[...] + pn
        a_f = (al * a_ref[...] + pn * _rows_by_head(vn_ref[0], nrow)) / l_f
        lam = _lambda_full(lq1[...], lk1[...], lq2[...], lk2[...])
        outs = []
        for h in range(HA):
            o = a_f[2 * h:2 * h + 1] - lam * a_f[2 * h + 1:2 * h + 2]
            outs.append(_subln(o, gain_ref[...]))
        o_ref[0] = jnp.concatenate(outs, axis=-1)


def _dattn_sample(z, ka, va, cache_k, cache_v, page_table, lams, subln_gain):
    nb, n_pages = page_table.shape
    page = cache_k.shape[1]
    w = HA * DVA
    row3 = lambda a: a.reshape(nb, 1, a.shape[-1])
    vec = pl.BlockSpec((1, DKA), lambda b, p, pt: (0, 0))
    pps = max(d for d in range(1, PAGES_PER_STEP + 1) if n_pages % d == 0)
    page_specs = [
        pl.BlockSpec((None, page, HA, DVA), lambda b, p, pt, r=r: (pt[b * n_pages + p * pps + r], 0, 0, 0))
        for r in range(pps)]
    grid_spec = pltpu.PrefetchScalarGridSpec(
        num_scalar_prefetch=1,
        grid=(nb, n_pages // pps),
        in_specs=[
            pl.BlockSpec((1, 1, w), lambda b, p, pt: (b, 0, 0)),
            pl.BlockSpec((1, 1, w), lambda b, p, pt: (b, 0, 0)),
            pl.BlockSpec((1, 1, w), lambda b, p, pt: (b, 0, 0)),
            *page_specs, *page_specs,
            vec, vec, vec, vec,
            pl.BlockSpec((1, DVA), lambda b, p, pt: (0, 0)),
        ],
        out_specs=pl.BlockSpec((1, 1, w), lambda b, p, pt: (b, 0, 0)),
        scratch_shapes=[pltpu.VMEM((2 * HA, DVA), F32), pltpu.VMEM((2 * HA, 1), F32),
                        pltpu.VMEM((2 * HA, 1), F32), pltpu.VMEM((2 * HA, DVA), F32)],
    )
    out = pl.pallas_call(
        functools.partial(_dattn_sample_kernel, n_pages=n_pages, page=page, pps=pps),
        grid_spec=grid_spec,
        out_shape=jax.ShapeDtypeStruct((nb, 1, w), F32),
        compiler_params=_cparams(("arbitrary", "arbitrary")),
        name="dattn_sample",
    )(page_table.reshape(-1), row3(z), row3(ka), row3(va), *([cache_k] * pps), *([cache_v] * pps),
      *[l.reshape(1, DKA) for l in lams], subln_gain.reshape(1, DVA))
    return out.reshape(nb, w)


def _ret_log_gamma(h):
    return math.log(1.0 - 2.0 ** (-5.0 - h))


def _ret_theta(width):
    theta = 1.0 / (ROPE_BASE ** jnp.linspace(0.0, 1.0, DKB // 2, dtype=F32))
    return jnp.tile(jnp.repeat(theta, 2), width // DKB).reshape(1, width)


def _rotate_pairs(x, cos, sin):
    n = x.shape[-1]
    lane = lax.broadcasted_iota(jnp.int32, x.shape, x.ndim - 1)
    nxt = pltpu.roll(x, n - 1, axis=x.ndim - 1)
    prv = pltpu.roll(x, 1, axis=x.ndim - 1)
    return x * cos + jnp.where(lane % 2 == 0, -nxt, prv) * sin


def _ret_prompt_kernel(q_ref, k_ref, v_ref, g_ref, th_ref, o_ref, s_ref, dec_ref, cos_ref, sin_ref, *, c):
    ci = pl.program_id(1)
    ri = lax.broadcasted_iota(jnp.int32, (c, c), 0)
    cj = lax.broadcasted_iota(jnp.int32, (c, c), 1)
    idx = lax.broadcasted_iota(jnp.int32, (c, 1), 0)

    @pl.when(jnp.logical_and(pl.program_id(0) == 0, ci == 0))
    def _():
        diff = (ri - cj).astype(F32)
        for h in range(HB):
            dec_ref[h] = jnp.where(diff >= 0, jnp.exp(_ret_log_gamma(h) * jnp.maximum(diff, 0.0)), 0.0)
        within = idx.astype(F32) * th_ref[...]
        cos_ref[...] = jnp.cos(within)
        sin_ref[...] = jnp.sin(within)

    @pl.when(ci == 0)
    def _():
        s_ref[...] = jnp.zeros(s_ref.shape, F32)

    start = (ci * c).astype(F32) * th_ref[...]
    c0, s0 = jnp.cos(start), jnp.sin(start)
    cos = jnp.concatenate([cos_ref[...] * c0 - sin_ref[...] * s0] * (HB * DKB // 128), axis=-1)
    sin = jnp.concatenate([sin_ref[...] * c0 + cos_ref[...] * s0] * (HB * DKB // 128), axis=-1)
    qr = _rotate_pairs(q_ref[...], cos, sin)
    kr = _rotate_pairs(k_ref[...], cos, sin) * (DKB ** -0.5)
    idx_f = idx.astype(F32)
    outs = []
    for h in range(HB):
        lg = _ret_log_gamma(h)
        qh = qr[:, h * DKB:(h + 1) * DKB].astype(BF16)
        kh = kr[:, h * DKB:(h + 1) * DKB]
        vh = v_ref[:, h * DVB:(h + 1) * DVB].astype(BF16)
        s_old = s_ref[0, h]
        scores = _dot_nt(qh, kh.astype(BF16)) * dec_ref[h]
        o = _dot(scores.astype(BF16), vh)
        o = o + _dot(qh, s_old.astype(BF16)) * jnp.exp(lg * (idx_f + 1.0))
        kd = (kh * jnp.exp(lg * (c - 1.0 - idx_f))).astype(BF16)
        s_ref[0, h] = math.exp(lg * c) * s_old + _dot_tn(kd, vh)
        gate = g_ref[:, h * DVB:(h + 1) * DVB]
        outs.append(_rms(o) * (gate * _sigmoid(gate)))
    o_ref[...] = jnp.concatenate(outs, axis=-1)


def _ret_prompt(z, batch, seq, c):
    nc = seq // c
    wq = HB * DKB
    wv = HB * DVB
    return pl.pallas_call(
        functools.partial(_ret_prompt_kernel, c=c),
        grid=(batch, nc),
        in_specs=[
            pl.BlockSpec((c, wq), lambda b, i: (b * nc + i, 2)),
            pl.BlockSpec((c, wq), lambda b, i: (b * nc + i, 3)),
            pl.BlockSpec((c, wv), lambda b, i: (b * nc + i, 2)),
            pl.BlockSpec((c, wv), lambda b, i: (b * nc + i, 3)),
            pl.BlockSpec((1, 128), lambda b, i: (0, 0)),
        ],
        out_specs=[
            pl.BlockSpec((c, wv), lambda b, i: (b * nc + i, 0)),
            pl.BlockSpec((1, HB, DKB, DVB), lambda b, i: (b, 0, 0, 0)),
        ],
        out_shape=[
            jax.ShapeDtypeStruct((batch * seq, wv), F32),
            jax.ShapeDtypeStruct((batch, HB, DKB, DVB), F32),
        ],
        scratch_shapes=[pltpu.VMEM((HB, c, c), F32), pltpu.VMEM((c, 128), F32), pltpu.VMEM((c, 128), F32)],
        compiler_params=_cparams(("arbitrary", "arbitrary")),
        name="ret_prompt",
    )(z, z, z, z, _ret_theta(128))


def _ret_sample_kernel(qk_ref, v_ref, g_ref, th_ref, s_ref, o_ref, so_ref, *, pos):
    ang = float(pos) * th_ref[...]
    x = qk_ref[...]
    ncol = x.shape[1]
    d = lax.broadcasted_iota(jnp.int32, x.shape, 0)
    partner = jnp.where(d % 2 == 0, -pltpu.roll(x, DKB - 1, axis=0), pltpu.roll(x, 1, axis=0))
    qk = x * jnp.cos(ang) + partner * jnp.sin(ang)
    is_k = lax.broadcasted_iota(jnp.int32, (1, ncol), 1) % (2 * HB) >= HB
    qkb = (qk * jnp.where(is_k, DKB ** -0.5, 1.0)).astype(BF16)
    sel_row = lax.broadcasted_iota(jnp.int32, (ncol, HB * DVB), 0)
    sel_head = lax.broadcasted_iota(jnp.int32, (ncol, HB * DVB), 1) // DVB
    rows = []
    for s in range(v_ref.shape[0]):
        q_sel = jnp.where(sel_row == s * 2 * HB + sel_head, 1.0, 0.0)
        kv_sel = jnp.where(sel_row == s * 2 * HB + HB + sel_head, v_ref[s:s + 1, :], 0.0)
        res = _dot(qkb, jnp.concatenate([q_sel, kv_sel], axis=-1).astype(BF16))
        outs = []
        for h in range(HB):
            q_lanes = res[:, h * DVB:(h + 1) * DVB]
            outer = res[:, (HB + h) * DVB:(HB + h + 1) * DVB]
            s_new = math.exp(_ret_log_gamma(h)) * s_ref[s, h] + outer
            so_ref[s, h] = s_new
            o = jnp.sum(q_lanes * s_new, axis=0, keepdims=True)
            gate = g_ref[s:s + 1, h * DVB:(h + 1) * DVB]
            outs.append(_rms(o) * (gate * _sigmoid(gate)))
        rows.append(jnp.concatenate(outs, axis=-1))
    o_ref[...] = jnp.concatenate(rows, axis=0)


def _ret_sample(z, state, pos):
    nb = z.shape[0]
    wq = HB * DKB
    wv = HB * DVB
    sb = 8 if nb % 8 == 0 else nb
    qk_t = jnp.transpose(z[:, wv:wv + 2 * wq].reshape(nb, 2 * HB, DKB), (2, 0, 1)).reshape(DKB, nb * 2 * HB)
    theta_col = _ret_theta(DKB).reshape(DKB, 1)
    return pl.pallas_call(
        functools.partial(_ret_sample_kernel, pos=pos),
        grid=(nb // sb,),
        in_specs=[
            pl.BlockSpec((DKB, sb * 2 * HB), lambda b: (0, b)),
            pl.BlockSpec((sb, wv), lambda b: (b, 2)),
            pl.BlockSpec((sb, wv), lambda b: (b, 3)),
            pl.BlockSpec((DKB, 1), lambda b: (0, 0)),
            pl.BlockSpec((sb, HB, DKB, DVB), lambda b: (b, 0, 0, 0)),
        ],
        out_specs=[
            pl.BlockSpec((sb, wv), lambda b: (b, 0)),
            pl.BlockSpec((sb, HB, DKB, DVB), lambda b: (b, 0, 0, 0)),
        ],
        out_shape=[
            jax.ShapeDtypeStruct((nb, wv), F32),
            jax.ShapeDtypeStruct(state.shape, F32),
        ],
        compiler_params=_cparams(("arbitrary",)),
        name="ret_sample",
    )(qk_t, z, z, theta_col, state)


def _mm_resid_kernel(*refs, n_in):
    a_refs = refs[:n_in]
    w_refs = refs[n_in:2 * n_in]
    x_ref = refs[2 * n_in]
    o_ref = refs[2 * n_in + 1]
    acc = x_ref[...]
    for a_ref, w_ref in zip(a_refs, w_refs):
        acc = acc + _dot(a_ref[...].astype(BF16), w_ref[...])
    o_ref[...] = acc


def _mm_resid(a_list, w_bf16, resid, tm):
    m, n = resid.shape
    tn = n
    in_specs = []
    for a in a_list:
        in_specs.append(pl.BlockSpec((tm, a.shape[1]), lambda i, j: (i, 0)))
    row = 0
    for a in a_list:
        kb = a.shape[1]
        in_specs.append(pl.BlockSpec((kb, tn), lambda i, j, r=row // kb: (r, j)))
        row += kb
    in_specs.append(pl.BlockSpec((tm, tn), lambda i, j: (i, j)))
    return pl.pallas_call(
        functools.partial(_mm_resid_kernel, n_in=len(a_list)),
        grid=(m // tm, n // tn),
        in_specs=in_specs,
        out_specs=pl.BlockSpec((tm, tn), lambda i, j: (i, j)),
        out_shape=jax.ShapeDtypeStruct((m, n), F32),
        compiler_params=_cparams(("arbitrary", "arbitrary")),
        name="mm_resid",
    )(*a_list, *([w_bf16] * len(a_list)), resid)


def _ffn_kernel(*refs, final_norm, emit_bf16, aliased):
    x_ref, g_ref, wg_ref, wu_ref, wd_ref, fg_ref = refs[:6]
    refs = refs[6 + aliased:]
    if emit_bf16:
        o_ref, wgb_ref, wub_ref, wdb_ref, xn_ref, acc_ref = refs
    else:
        o_ref, xn_ref, acc_ref = refs
    j = pl.program_id(1)

    @pl.when(j == 0)
    def _():
        xn_ref[...] = (_rms(x_ref[...]) * g_ref[...]).astype(BF16)
        acc_ref[...] = jnp.zeros(acc_ref.shape, F32)

    wg, wu, wd = wg_ref[...].astype(BF16), wu_ref[...].astype(BF16), wd_ref[...].astype(BF16)
    if emit_bf16:
        wgb_ref[...], wub_ref[...], wdb_ref[...] = wg, wu, wd
    xn = xn_ref[...]
    gate = _dot(xn, wg)
    up = _dot(xn, wu)
    hid = (gate * _sigmoid(gate) * up).astype(BF16)
    acc_ref[...] += _dot(hid, wd)

    @pl.when(j == pl.num_programs(1) - 1)
    def _():
        y = x_ref[...] + acc_ref[...]
        if final_norm:
            y = _rms(y) * fg_ref[...]
        o_ref[...] = y


def _ffn_call(x, gain, wg, wu, wd, final_gain, final_norm, tm, tf, row_tiles, done=None):
    m = x.shape[0]
    dff = wg.shape[1]
    t0, t1 = row_tiles
    emit_bf16 = wg.dtype == F32
    w_specs = [pl.BlockSpec((D_MODEL, tf), lambda i, j: (0, j)),
               pl.BlockSpec((D_MODEL, tf), lambda i, j: (0, j)),
               pl.BlockSpec((tf, D_MODEL), lambda i, j: (j, 0))]
    in_specs = [pl.BlockSpec((tm, D_MODEL), lambda i, j: (i + t0, 0)),
                pl.BlockSpec((1, D_MODEL), lambda i, j: (0, 0)),
                *w_specs,
                pl.BlockSpec((1, D_MODEL), lambda i, j: (0, 0))]
    args = [x, gain.reshape(1, D_MODEL), wg, wu, wd, final_gain.reshape(1, D_MODEL)]
    out_specs = [pl.BlockSpec((tm, D_MODEL), lambda i, j: (i + t0, 0))]
    out_shape = [jax.ShapeDtypeStruct((m, D_MODEL), F32)]
    aliases = {}
    if done is not None:
        in_specs.append(pl.BlockSpec(memory_space=pl.ANY))
        args.append(done)
        aliases = {len(args) - 1: 0}
    if emit_bf16:
        out_specs += w_specs
        out_shape += [jax.ShapeDtypeStruct(w.shape, BF16) for w in (wg, wu, wd)]
    res = pl.pallas_call(
        functools.partial(_ffn_kernel, final_norm=final_norm, emit_bf16=emit_bf16, aliased=done is not None),
        grid=(t1 - t0, dff // tf),
        in_specs=in_specs,
        out_specs=out_specs,
        out_shape=out_shape,
        input_output_aliases=aliases,
        scratch_shapes=[pltpu.VMEM((tm, D_MODEL), BF16), pltpu.VMEM((tm, D_MODEL), F32)],
        compiler_params=_cparams(("arbitrary", "arbitrary")),
        name="ffn",
    )(*args)
    return res if emit_bf16 else res[0]


def _ffn(x, gain, weights, final_gain, final_norm, tm, tf=512):
    n_tiles = x.shape[0] // tm
    if weights[0].dtype != F32:
        return _ffn_call(x, gain, *weights, final_gain, final_norm, tm, tf, (0, n_tiles)), weights
    y, *weights_bf16 = _ffn_call(x, gain, *weights, final_gain, final_norm, tm, tf // 2, (0, 1))
    if n_tiles > 1:
        y = _ffn_call(x, gain, *weights_bf16, final_gain, final_norm, tm, tf, (1, n_tiles), done=y)
    return y, tuple(weights_bf16)


def _norm_and_shift(x_ref, g_ref, prev_ref, carry_ref, i, tiles_per_seq):
    h = _rms(x_ref[...]) * g_ref[...]
    if prev_ref is None:
        tm = h.shape[0]
        first = (i % tiles_per_seq) == 0
        prev_row = jnp.where(first, 0.0, carry_ref[0:1, :])
        rows = lax.broadcasted_iota(jnp.int32, (tm, 1), 0)
        shifted = jnp.where(rows == 0, prev_row, pltpu.roll(h, 1, axis=0))
        carry_ref[0:1, :] = h[tm - 1:tm, :]
    else:
        shifted = prev_ref[...]
    return h, shifted - h


def _rwkv_rkv_kernel(*refs, per_row_prev, tiles_per_seq):
    if per_row_prev:
        x_ref, g_ref, prev_ref, mu_ref, w_ref, o_ref, hl_ref, h_ref, xx_ref, carry_ref = refs
    else:
        x_ref, g_ref, mu_ref, w_ref, o_ref, hl_ref, h_ref, xx_ref, carry_ref = refs
        prev_ref = None
    i = pl.program_id(0)
    j = pl.program_id(1)

    @pl.when(j == 0)
    def _():
        h, xx = _norm_and_shift(x_ref, g_ref, prev_ref, carry_ref, i, tiles_per_seq)
        h_ref[...] = h
        xx_ref[...] = xx
        if per_row_prev:
            hl_ref[...] = h
        else:
            hl_ref[0] = h[h.shape[0] - 1:, :]

    xm = (h_ref[...] + xx_ref[...] * mu_ref[0]).astype(BF16)
    o_ref[...] = _dot(xm, w_ref[...])


def _rwkv_rkv(x, gain, prev, mu_rkv, w_rkv, tm, tiles_per_seq):
    m = x.shape[0]
    tn = D_MODEL
    per_seq = D_MODEL // tn
    per_row_prev = prev is not None
    in_specs = [pl.BlockSpec((tm, D_MODEL), lambda i, j: (i, 0)),
                pl.BlockSpec((1, D_MODEL), lambda i, j: (0, 0))]
    args = [x, gain.reshape(1, D_MODEL)]
    if per_row_prev:
        in_specs.append(pl.BlockSpec((tm, D_MODEL), lambda i, j: (i, 0)))
        args.append(prev)
        hl_spec = pl.BlockSpec((tm, D_MODEL), lambda i, j: (i, 0))
        hl_shape = jax.ShapeDtypeStruct((m, D_MODEL), F32)
    else:
        nseq = m // (tm * tiles_per_seq)
        hl_spec = pl.BlockSpec((1, 1, D_MODEL), lambda i, j: (i // tiles_per_seq, 0, 0))
        hl_shape = jax.ShapeDtypeStruct((nseq, 1, D_MODEL), F32)
    in_specs += [pl.BlockSpec((1, 1, D_MODEL), lambda i, j: (j // per_seq, 0, 0)),
                 pl.BlockSpec((None, D_MODEL, tn), lambda i, j: (j // per_seq, 0, j % per_seq))]
    args += [mu_rkv, w_rkv]
    return pl.pallas_call(
        functools.partial(_rwkv_rkv_kernel, per_row_prev=per_row_prev, tiles_per_seq=tiles_per_seq),
        grid=(m // tm, 3 * per_seq),
        in_specs=in_specs,
        out_specs=[pl.BlockSpec((tm, tn), lambda i, j: (i, j)), hl_spec],
        out_shape=[jax.ShapeDtypeStruct((m, 3 * D_MODEL), F32), hl_shape],
        scratch_shapes=[pltpu.VMEM((tm, D_MODEL), F32), pltpu.VMEM((tm, D_MODEL), F32),
                        pltpu.VMEM((8, D_MODEL), F32)],
        compiler_params=_cparams(("arbitrary", "arbitrary")),
        name="rwkv_rkv",
    )(*args)


def _rwkv_lora_kernel(*refs, per_row_prev, tiles_per_seq):
    if per_row_prev:
        (x_ref, g_ref, prev_ref, mu_ref, w0_ref, w1_ref, w2_ref, a0_ref, a1_ref, a2_ref, g1_ref, g2_ref,
         lw_ref, a_ref, go_ref, carry_ref) = refs
    else:
        (x_ref, g_ref, mu_ref, w0_ref, w1_ref, w2_ref, a0_ref, a1_ref, a2_ref, g1_ref, g2_ref,
         lw_ref, a_ref, go_ref, carry_ref) = refs
        prev_ref = None
    h, xx = _norm_and_shift(x_ref, g_ref, prev_ref, carry_ref, pl.program_id(0), tiles_per_seq)
    xw = (h + xx * mu_ref[0]).astype(BF16)
    xa = (h + xx * mu_ref[1]).astype(BF16)
    xg = (h + xx * mu_ref[2]).astype(BF16)
    zw = w0_ref[...] + _dot(jnp.tanh(_dot(xw, w1_ref[...])).astype(BF16), w2_ref[...])
    lw_ref[...] = -math.exp(-0.5) * _sigmoid(zw)
    a_ref[...] = _sigmoid(a0_ref[...] + _dot(_dot(xa, a1_ref[...]).astype(BF16), a2_ref[...]))
    go_ref[...] = _dot(_sigmoid(_dot(xg, g1_ref[...])).astype(BF16), g2_ref[...])


def _rwkv_lora(x, gain, prev, mu_wag, w0, w1, w2, a0, a1, a2, g1, g2, tm, tiles_per_seq):
    m = x.shape[0]
    per_row_prev = prev is not None
    full = lambda a: pl.BlockSpec(a.shape, lambda i: (0,) * a.ndim)
    row = pl.BlockSpec((tm, D_MODEL), lambda i: (i, 0))
    args = [x, gain.reshape(1, D_MODEL)]
    in_specs = [row, full(args[1])]
    if per_row_prev:
        in_specs.append(row)
        args.append(prev)
    rest = [mu_wag, w0.reshape(1, D_MODEL), w1, w2, a0.reshape(1, D_MODEL), a1, a2, g1, g2]
    in_specs += [full(a) for a in rest]
    args += rest
    return pl.pallas_call(
        functools.partial(_rwkv_lora_kernel, per_row_prev=per_row_prev, tiles_per_seq=tiles_per_seq),
        grid=(m // tm,),
        in_specs=in_specs,
        out_specs=[row, row, row],
        out_shape=[jax.ShapeDtypeStruct((m, D_MODEL), F32)] * 3,
        scratch_shapes=[pltpu.VMEM((8, D_MODEL), F32)],
        compiler_params=_cparams(("arbitrary",)),
        name="rwkv_lora",
    )(*args)


def _wkv_keys(k, a, kk_gain, ka_gain, head_sum):
    kk = k * kk_gain
    kk = kk / jnp.maximum(jnp.sqrt(head_sum(kk * kk)), 1e-12)
    return kk, kk * a, k * (1.0 + (a - 1.0) * ka_gain)


def _head_norm(o, head_sum):
    mean = head_sum(o) * (1.0 / NC)
    var = head_sum(jnp.square(o - mean)) * (1.0 / NC)
    return (o - mean) * lax.rsqrt(var + GN_EPS)


def _wkv_finish(on, r, k2, v, g, rk, ln_g, ln_b, head_sum):
    bonus = head_sum(r * k2 * rk) * v
    return (on * ln_g + ln_b + bonus) * g


def _pair_sum(x):
    lane = lax.broadcasted_iota(jnp.int32, x.shape, 1)
    s0 = jnp.sum(x[:, :NC], axis=-1, keepdims=True)
    s1 = jnp.sum(x[:, NC:], axis=-1, keepdims=True)
    return jnp.where(lane < NC, s0, s1)


def _bmm(a, b):
    return jnp.einsum("cij,cjk->cik", a, b, preferred_element_type=F32)


def _bmm_nt(a, b):
    return jnp.einsum("cik,cjk->cij", a, b, preferred_element_type=F32)


def _wkv_prompt_kernel(r_ref, k_ref, v_ref, lw_ref, a_ref, kkg_ref, kag_ref, rk_ref, g_ref, lng_ref, lnb_ref,
                       y_ref, so_ref, st_ref, gh_ref, ghb_ref, e0t_ref, ft_ref, wend_ref, bonus_ref, *, tc, nt):
    c = WKV_CHUNK
    nch = tc // c
    s = pl.program_id(0)

    @pl.when(s == 0)
    def _():
        for ref in (st_ref, gh_ref, ghb_ref, e0t_ref, ft_ref, wend_ref, bonus_ref):
            ref[...] = jnp.zeros(ref.shape, ref.dtype)

    prev = jnp.maximum(s - 1, 0)
    slot = prev % 2
    first = prev % nt == 0
    ss = [jnp.where(first, 0.0, st_ref[0]), jnp.where(first, 0.0, st_ref[1])]
    o_chunks = [[], []]
    lane_sum = lambda t: jnp.sum(t, axis=-1, keepdims=True)
    n_points = 2 + int(math.log2(c))
    done = [0]

    def advance_stage2(point):
        upto = nch * (point + 1) // n_points
        for ch in range(done[0], upto):
            w_end = wend_ref[slot, ch]
            for hh in range(2):
                n = hh * nch + ch
                other = slice((1 - hh) * NC, (2 - hh) * NC)
                sb = ss[hh].astype(BF16)
                o = _dot_nt(ghb_ref[slot, n], sb) + gh_ref[slot, n][:, other]
                o_chunks[hh].append(_head_norm(o, lane_sum))
                ss[hh] = ss[hh] * w_end + _dot(sb, e0t_ref[slot, n]) + ft_ref[slot, n]
        done[0] = upto

    ri = lax.broadcasted_iota(jnp.int32, (c, c), 0)
    cj = lax.broadcasted_iota(jnp.int32, (c, c), 1)
    tri = (ri >= cj).astype(BF16)
    gi = lax.broadcasted_iota(jnp.int32, (2 * c, 2 * c), 0)
    gj = lax.broadcasted_iota(jnp.int32, (2 * c, 2 * c), 1) % c
    gmask = (gj < jnp.where(gi < c, gi, gi - c + 1))[None]

    r = r_ref[...]
    v = v_ref[...]
    kk, bv, k2 = _wkv_keys(k_ref[...], a_ref[...], kkg_ref[...], kag_ref[...], _pair_sum)
    to3 = lambda x: x.reshape(nch, c, 2 * NC)
    lw3 = to3(lw_ref[...])
    p0 = lw3.astype(BF16)
    r1 = lw3 - p0.astype(F32)
    p1 = r1.astype(BF16)
    p2 = (r1 - p1.astype(F32)).astype(BF16)
    trib = jnp.broadcast_to(tri[None], (nch, c, c))
    cum = _bmm(trib, p0) + (_bmm(trib, p1) + _bmm(trib, p2))
    cum_last = cum[:, c - 1:c, :]
    w_rest = jnp.exp(cum_last - cum)
    w_inv = jnp.exp(-cum)
    am = -to3(kk) * jnp.exp(cum - lw3)
    rm = to3(r) * jnp.exp(cum)
    bm = (to3(bv) * w_inv).astype(BF16)
    km = (to3(k2) * w_inv).astype(BF16)
    bt = (to3(bv) * w_rest).astype(BF16)
    kt = (to3(k2) * w_rest).astype(BF16)
    v3 = to3(v)
    advance_stage2(0)

    both = lambda x: jnp.concatenate([x, x], axis=0)
    lane = lax.broadcasted_iota(jnp.int32, (2 * nch, 1, 2 * NC), 2)
    head = lax.broadcasted_iota(jnp.int32, (2 * nch, 1, 2 * NC), 0) // nch
    own = (lane // NC) == head
    v_sw = both(pltpu.roll(v, NC, axis=1).reshape(nch, c, 2 * NC))
    x1 = both(jnp.concatenate([am, rm], axis=1).astype(BF16))
    x2 = jnp.where(own, both(jnp.concatenate([bm, km], axis=1)), 0.0)
    g4 = jnp.where(gmask, _bmm_nt(x1, x2), 0.0)
    lv = _bmm(g4[:, :, c:].astype(BF16), v_sw.astype(BF16))
    advance_stage2(1)
    x = jnp.where(own, both(am), lv[:, :c])
    lp = g4[:, :c, :c].astype(BF16)
    steps = int(math.log2(c))
    for i in range(steps - 1):
        prod = _bmm(lp, jnp.concatenate([x.astype(BF16), lp], axis=-1))
        x = x + prod[:, :, :2 * NC]
        lp = prod[:, :, 2 * NC:].astype(BF16)
        advance_stage2(2 + i)
    x = x + _bmm(lp, x.astype(BF16))
    advance_stage2(n_points - 1)
    gh_new = _bmm(g4[:, c:, :c].astype(BF16), x.astype(BF16)) + jnp.where(own, both(rm), lv[:, c:])
    btk = jnp.where(own, both(jnp.concatenate([bt, kt], axis=1)), 0.0)
    pqv = jnp.concatenate([x, jnp.where(own, 0.0, v_sw)], axis=1).astype(BF16)
    st_ref[0] = ss[0]
    st_ref[1] = ss[1]
    so_ref[0, 0] = ss[0][:, :NC]
    so_ref[0, 1] = ss[1][:, NC:]
    on = jnp.concatenate([jnp.concatenate(oc, axis=0) for oc in o_chunks], axis=-1)
    y_ref[...] = (on * lng_ref[...] + lnb_ref[...] + bonus_ref[slot]) * g_ref[...]

    nxt = s % 2
    gh_ref[nxt] = gh_new
    ghb_ref[nxt] = gh_new.astype(BF16)
    for n in range(2 * nch):
        eft = _dot_tn(pqv[n], btk[n])
        e0t_ref[nxt, n] = eft.astype(BF16)
        ft_ref[nxt, n] = eft[NC:] if n < nch else eft[:NC]
    wend_ref[nxt] = jnp.exp(cum_last)
    bonus_ref[nxt] = _pair_sum(r * k2 * rk_ref[...]) * v


def _wkv_prompt(rkv, lw, a, g, kk_gain, ka_gain, rk, ln_g, ln_b, batch, seq, tc):
    nt = seq // tc
    nhp = HC // 2
    nblk = batch * nhp * nt
    nch = tc // WKV_CHUNK

    def rows_cols(blk):
        b = blk // (nhp * nt)
        return b * nt + blk % nt, (blk // nt) % nhp, b

    cur = lambda s: jnp.minimum(s, nblk - 1)
    prv = lambda s: jnp.maximum(s - 1, 0)
    tok = lambda off, which: pl.BlockSpec(
        (tc, 2 * NC), lambda s: (rows_cols(which(s))[0], off + rows_cols(which(s))[1]))
    par = lambda which: pl.BlockSpec((1, 2 * NC), lambda s: (0, rows_cols(which(s))[1]))
    return pl.pallas_call(
        functools.partial(_wkv_prompt_kernel, tc=tc, nt=nt),
        grid=(nblk + 1,),
        in_specs=[tok(0, cur), tok(nhp, cur), tok(2 * nhp, cur), tok(0, cur), tok(0, cur),
                  par(cur), par(cur), par(cur), tok(0, prv), par(prv), par(prv)],
        out_specs=[
            tok(0, prv),
            pl.BlockSpec((1, 2, NC, NC), lambda s: (rows_cols(prv(s))[2], rows_cols(prv(s))[1], 0, 0)),
        ],
        out_shape=[
            jax.ShapeDtypeStruct((batch * seq, D_MODEL), F32),
            jax.ShapeDtypeStruct((batch, HC, NC, NC), F32),
        ],
        scratch_shapes=[
            pltpu.VMEM((2, NC, 2 * NC), F32),
            pltpu.VMEM((2, 2 * nch, WKV_CHUNK, 2 * NC), F32),
            pltpu.VMEM((2, 2 * nch, WKV_CHUNK, 2 * NC), BF16),
            pltpu.VMEM((2, 2 * nch, 2 * NC, 2 * NC), BF16),
            pltpu.VMEM((2, 2 * nch, NC, 2 * NC), F32),
            pltpu.VMEM((2, nch, 1, 2 * NC), F32),
            pltpu.VMEM((2, tc, 2 * NC), F32),
        ],
        compiler_params=_cparams(("arbitrary",)),
        name="wkv_prompt",
    )(rkv, rkv, rkv, lw, a, kk_gain, ka_gain, rk, g, ln_g, ln_b)


def _wkv_sample_kernel(r_ref, k_ref, v_ref, lw_ref, a_ref, g_ref, kkg_ref, kag_ref, rk_ref, lng_ref, lnb_ref,
                       s_ref, y_ref, so_ref, o_ref):
    ys = []
    col_sum = lambda x: jnp.sum(x, axis=0, keepdims=True)
    for hh in range(2):
        sl = slice(hh * NC, (hh + 1) * NC)
        t = lambda ref: ref[...].T[sl, :]
        r, k, v, a, g = t(r_ref), t(k_ref), t(v_ref), t(a_ref), t(g_ref)
        w = jnp.exp(t(lw_ref))
        kk, bv, k2 = _wkv_keys(k, a, kkg_ref[sl, :], kag_ref[sl, :], col_sum)
        for i in range(NC):
            s_old = s_ref[hh, i]
            sk = jnp.sum(s_old * kk, axis=0, keepdims=True)
            s_new = s_old * w - sk * bv + v[i:i + 1, :] * k2
            so_ref[hh, i] = s_new
            o_ref[i:i + 1, :] = jnp.sum(s_new * r, axis=0, keepdims=True)
        ys.append(_wkv_finish(_head_norm(o_ref[...], col_sum), r, k2, v, g,rk_ref[sl, :], lng_ref[sl, :], lnb_ref[sl, :], col_sum))
    y_ref[...] = jnp.concatenate(ys, axis=0).T


def _wkv_sample(rkv, lw, a, g, kk_gain, ka_gain, rk, ln_g, ln_b, state_t):
    nb = lw.shape[0]
    nhp = HC // 2
    tok = lambda off: pl.BlockSpec((nb, 2 * NC), lambda hp: (0, off + hp))
    par = pl.BlockSpec((2 * NC, 1), lambda hp: (hp, 0))
    col = lambda p: p.reshape(D_MODEL, 1)
    st_spec = pl.BlockSpec((2, NC, NC, nb), lambda hp: (hp, 0, 0, 0))
    return pl.pallas_call(
        _wkv_sample_kernel,
        grid=(nhp,),
        in_specs=[tok(0), tok(nhp), tok(2 * nhp), tok(0), tok(0), tok(0), par, par, par, par, par, st_spec],
        out_specs=[pl.BlockSpec((nb, 2 * NC), lambda hp: (0, hp)), st_spec],
        out_shape=[
            jax.ShapeDtypeStruct((nb, D_MODEL), F32),
            jax.ShapeDtypeStruct(state_t.shape, F32),
        ],
        scratch_shapes=[pltpu.VMEM((NC, nb), F32)],
        compiler_params=_cparams(("arbitrary",)),
        name="wkv_sample",
    )(rkv, rkv, rkv, lw, a, g, col(kk_gain), col(ka_gain), col(rk), col(ln_g), col(ln_b), state_t)


def _row_tile(m, want):
    return min(m, want)


def _trunk(x, batch, seq, past, W):
    m = x.shape[0]
    prompt = past is None
    tm = _row_tile(m, 512)
    lams = (W["lambda_q1"], W["lambda_k1"], W["lambda_q2"], W["lambda_k2"])

    ka, va, z = _inproj0(x, W["norm0_mix"], W["w_in0"], tm)
    if prompt:
        o_a = _dattn_prompt(z, ka, va, lams, W["subln_gain"], batch, seq, min(seq, 512))
        o_b, ret_new = _ret_prompt(z, batch, seq, min(seq, 256))
    else:
        o_a = _dattn_sample(z, ka, va, past["cache_k"], past["cache_v"], past["page_table"], lams,
                            W["subln_gain"])
        o_b, ret_new = _ret_sample(z, past["state_ret"], past["pos"])
    x = _mm_resid([o_a, o_b], W["w_out0"], x, _row_tile(m, 256))
    x, W["ffn0"] = _ffn(x, W["norm0_ffn"], W["ffn0"], W["norm_final"], False, tm)

    tiles_per_seq = max(seq // tm, 1)
    prev = None if prompt else past["state_shift"]
    rkv, hlast = _rwkv_rkv(x, W["norm1_mix"], prev, W["mu_rkv"], W["w_rkv"], tm, tiles_per_seq)
    tl = _row_tile(m, 256)
    lw, a, g = _rwkv_lora(x, W["norm1_mix"], prev, W["mu_wag"], W["decay_w0"], W["decay_w1"], W["decay_w2"],
                          W["aaa_a0"], W["aaa_a1"], W["aaa_a2"], W["gate_g1"], W["gate_g2"],
                          tl, max(seq // tl, 1))
    head_params = (W["k_k"], W["k_a"], W["r_k"], W["lnx_gain"], W["lnx_bias"])
    if prompt:
        y, wkv_new = _wkv_prompt(rkv, lw, a, g, *head_params, batch, seq, min(seq, 512))
        shift_new = hlast.reshape(batch, D_MODEL)
    else:
        state_t = jnp.transpose(past["state_wkv"], (1, 2, 3, 0))
        y, wkv_t = _wkv_sample(rkv, lw, a, g, *head_params, state_t)
        wkv_new = jnp.transpose(wkv_t, (3, 0, 1, 2))
        shift_new = hlast
    x = _mm_resid([y], W["w_out1"], x, _row_tile(m, 256))
    x, W["ffn1"] = _ffn(x, W["norm1_ffn"], W["ffn1"], W["norm_final"], True, tm)
    return x, ka, va, ret_new, wkv_new, shift_new


def kernel(x_prompt, x_sample, cache_k, cache_v, page_table, state_ret, state_wkv, state_shift, norm0_mix, w_in0, lambda_q1, lambda_k1, lambda_q2, lambda_k2, subln_gain, w_out0, norm0_ffn, w_gate0, w_up0, w_down0, norm1_mix, mu1, w_r1, w_k1, w_v1, decay_w0, decay_w1, decay_w2, aaa_a0, aaa_a1, aaa_a2, gate_g1, gate_g2, k_k, k_a, r_k, lnx_gain, lnx_bias, w_out1, norm1_ffn, w_gate1, w_up1, w_down1, norm_final):
    bf = lambda w: w.astype(BF16)
    row = lambda p: p.reshape(1, D_MODEL)
    W = dict(
        norm0_mix=norm0_mix, w_in0=bf(w_in0), lambda_q1=lambda_q1, lambda_k1=lambda_k1, lambda_q2=lambda_q2,
        lambda_k2=lambda_k2, subln_gain=subln_gain, w_out0=bf(w_out0), norm0_ffn=norm0_ffn,
        ffn0=(w_gate0, w_up0, w_down0),
        norm1_mix=norm1_mix,
        mu_rkv=jnp.stack([mu1[0], mu1[2], mu1[3]]).reshape(3, 1, D_MODEL),
        mu_wag=jnp.stack([mu1[1], mu1[4], mu1[5]]).reshape(3, 1, D_MODEL),
        w_rkv=jnp.stack([bf(w_r1), bf(w_k1), bf(w_v1)]),
        decay_w0=decay_w0, decay_w1=bf(decay_w1), decay_w2=bf(decay_w2),
        aaa_a0=aaa_a0, aaa_a1=bf(aaa_a1), aaa_a2=bf(aaa_a2), gate_g1=bf(gate_g1), gate_g2=bf(gate_g2),
        k_k=row(k_k), k_a=row(k_a), r_k=row(r_k), lnx_gain=row(lnx_gain), lnx_bias=row(lnx_bias),
        w_out1=bf(w_out1), norm1_ffn=norm1_ffn, ffn1=(w_gate1, w_up1, w_down1),
        norm_final=norm_final,
    )
    bsz, seq = x_prompt.shape[:2]
    y_p, k_p, v_p, ret_p, wkv_p, shift_p = _trunk(x_prompt.reshape(bsz * seq, D_MODEL), bsz, seq, None, W)
    nb, dseq = x_sample.shape[:2]
    past_len = page_table.shape[1] * cache_k.shape[1]
    past = dict(cache_k=cache_k, cache_v=cache_v, page_table=page_table, state_ret=state_ret,
                state_wkv=state_wkv, state_shift=state_shift, pos=past_len)
    y_s, k_s, v_s, ret_s, wkv_s, shift_s = _trunk(x_sample.reshape(nb * dseq, D_MODEL), nb, dseq, past, W)
    return (y_p.reshape(bsz, seq, D_MODEL), y_s.reshape(nb, dseq, D_MODEL),
            k_p.reshape(bsz, seq, HA, 2 * DKA), v_p.reshape(bsz, seq, HA, DVA),
            k_s.reshape(nb, dseq, HA, 2 * DKA), v_s.reshape(nb, dseq, HA, DVA),
            ret_p, ret_s, wkv_p, wkv_s, shift_p, shift_s)
```

```python
import functools
import math

import jax
import jax.numpy as jnp
from jax import lax
from jax.experimental import pallas as pl
from jax.experimental.pallas import tpu as pltpu

F32 = jnp.float32
BF16 = jnp.bfloat16

D_MODEL = 2048
EPS = 1e-6
DKA = 64
DVA = 128
HA = 8
LAMBDA_INIT = 0.8 - 0.6 * math.exp(-0.3 * 0)
DKB = 64
DVB = 128
HB = 8
ROPE_BASE = 10000.0
NC = 64
HC = D_MODEL // NC
GN_EPS = 64e-5
WKV_CHUNK = 64
PAGES_PER_STEP = 16
VMEM_LIMIT = 48 * 1024 * 1024
INPROJ_VMEM_LIMIT = 56 * 1024 * 1024


def _cparams(sem):
    return pltpu.CompilerParams(dimension_semantics=sem, vmem_limit_bytes=VMEM_LIMIT)


def _rms(x, eps=EPS):
    return x * lax.rsqrt(jnp.mean(x * x, axis=-1, keepdims=True) + eps)


def _sigmoid(x):
    return 1.0 / (1.0 + jnp.exp(-x))


def _dot(a, b):
    return jnp.dot(a, b, preferred_element_type=F32)


def _dot_nt(a, b):
    return lax.dot_general(a, b, (((1,), (1,)), ((), ())), preferred_element_type=F32)


def _dot_tn(a, b):
    return lax.dot_general(a, b, (((0,), (0,)), ((), ())), preferred_element_type=F32)


def _split2(x):
    hi = x.astype(BF16)
    lo = (x - hi.astype(F32)).astype(BF16)
    return hi, lo


def _dot_f32(a, b, dot=_dot):
    ah, al = _split2(a)
    bh, bl = _split2(b)
    return dot(ah, bh) + (dot(ah, bl) + dot(al, bh))


def _inproj0_kernel(x_ref, g_ref, w_ref, ka_ref, va_ref, z_ref, xn_ref):
    j = pl.program_id(1)

    @pl.when(j == 0)
    def _():
        xn_ref[...] = (_rms(x_ref[...]) * g_ref[...]).astype(BF16)

    acc = _dot(xn_ref[...], w_ref[...])
    half = acc.shape[1] // 2

    @pl.when(j == 0)
    def _():
        z_ref[:, :half] = acc[:, :half]
        ka_ref[...] = acc[:, half:]

    @pl.when(j == 1)
    def _():
        va_ref[...] = acc[:, :half]
        z_ref[:, half:] = acc[:, half:]

    @pl.when(j == 2)
    def _():
        z_ref[...] = acc


def _inproj0(x, gain, w_bf16, tm):
    m = x.shape[0]
    tn = 2048
    nj = w_bf16.shape[1] // tn
    return pl.pallas_call(
        _inproj0_kernel,
        grid=(m // tm, nj),
        in_specs=[
            pl.BlockSpec((tm, D_MODEL), lambda i, j: (i, 0)),
            pl.BlockSpec((1, D_MODEL), lambda i, j: (0, 0)),
            pl.BlockSpec((D_MODEL, tn), lambda i, j: (0, j)),
        ],
        out_specs=[
            pl.BlockSpec((tm, tn // 2), lambda i, j: (i, 0)),
            pl.BlockSpec((tm, tn // 2), lambda i, j: (i, 0)),
            pl.BlockSpec((tm, tn), lambda i, j: (i, j // 2)),
        ],
        out_shape=[
            jax.ShapeDtypeStruct((m, tn // 2), F32),
            jax.ShapeDtypeStruct((m, tn // 2), F32),
            jax.ShapeDtypeStruct((m, 2 * tn), F32),
        ],
        scratch_shapes=[pltpu.VMEM((tm, D_MODEL), BF16)],
        compiler_params=pltpu.CompilerParams(dimension_semantics=("arbitrary", "arbitrary"),
                                             vmem_limit_bytes=INPROJ_VMEM_LIMIT),
        name="inproj0",
    )(x, gain.reshape(1, D_MODEL), w_bf16)


def _lambda_full(lq1, lk1, lq2, lk2):
    s1 = jnp.sum(lq1 * lk1, axis=-1, keepdims=True)
    s2 = jnp.sum(lq2 * lk2, axis=-1, keepdims=True)
    return jnp.exp(s1) - jnp.exp(s2) + LAMBDA_INIT


def _alibi_slope(h_vec):
    slope = jnp.zeros(h_vec.shape, F32)
    for h in range(HA):
        slope = jnp.where(h_vec == h, 2.0 ** (-(h + 1)), slope)
    return slope


def _subln(o, gain):
    return _rms(o) * gain * (1.0 - LAMBDA_INIT)


def _dattn_prompt_kernel(ii_ref, jj_ref, q_ref, k_ref, v_ref, lq1, lk1, lq2, lk2, gain_ref, o_ref,
                         m_ref, l_ref, a_ref, *, t, hps):
    hg = pl.program_id(1)
    s = pl.program_id(2)
    i = ii_ref[s]
    j = jj_ref[s]

    @pl.when(j == 0)
    def _():
        m_ref[...] = jnp.full(m_ref.shape, -jnp.inf, F32)
        l_ref[...] = jnp.zeros(l_ref.shape, F32)
        a_ref[...] = jnp.zeros(a_ref.shape, F32)

    def step(diagonal):
        col = lax.broadcasted_iota(jnp.int32, (1, t), 1)
        rel = ((j - i) * t + col - (t - 1)).astype(F32)
        if diagonal:
            keep = (lax.broadcasted_iota(jnp.int32, (t, t), 1) <= lax.broadcasted_iota(jnp.int32, (t, t), 0))
        for hh in range(hps):
            hl = slice(hh * DVA, (hh + 1) * DVA)
            bias = _alibi_slope(jnp.full((1, 1), hg * hps + hh, jnp.int32)) * rel
            q = (q_ref[:, hl] * (DKA ** -0.5)).astype(BF16)
            k = k_ref[:, hl].astype(BF16)
            v = v_ref[:, hl].astype(BF16)
            for half in range(2):
                idx = 2 * hh + half
                sl = slice(half * DKA, (half + 1) * DKA)
                sc = _dot_nt(q[:, sl], k[:, sl]) + bias
                if diagonal:
                    sc = jnp.where(keep, sc, -1e30)
                m_prev = m_ref[idx]
                m_new = jnp.maximum(m_prev, jnp.max(sc, axis=-1, keepdims=True))
                alpha = jnp.exp(m_prev - m_new)
                p = jnp.exp(sc - jnp.concatenate([m_new] * (t // DVA), axis=-1))
                l_ref[idx] = alpha * l_ref[idx] + jnp.sum(p, axis=-1, keepdims=True)
                a_ref[idx] = alpha * a_ref[idx] + _dot(p.astype(BF16), v)
                m_ref[idx] = m_new

    @pl.when(j < i)
    def _():
        step(False)

    @pl.when(j == i)
    def _():
        step(True)
        lam = _lambda_full(lq1[...], lk1[...], lq2[...], lk2[...])
        for hh in range(hps):
            o = a_ref[2 * hh] / l_ref[2 * hh] - lam * (a_ref[2 * hh + 1] / l_ref[2 * hh + 1])
            o_ref[:, hh * DVA:(hh + 1) * DVA] = _subln(o, gain_ref[...])


def _dattn_prompt(z, ka, va, lams, subln_gain, batch, seq, t):
    nt = seq // t
    hps = 2
    pairs = [(i, j) for i in range(nt) for j in range(i + 1)]
    ii = jnp.asarray([p[0] for p in pairs], jnp.int32)
    jj = jnp.asarray([p[1] for p in pairs], jnp.int32)
    vec = pl.BlockSpec((1, DKA), lambda b, h, s, ii, jj: (0, 0))
    grid_spec = pltpu.PrefetchScalarGridSpec(
        num_scalar_prefetch=2,
        grid=(batch, HA // hps, len(pairs)),
        in_specs=[
            pl.BlockSpec((t, hps * DVA), lambda b, h, s, ii, jj: (b * nt + ii[s], h)),
            pl.BlockSpec((t, hps * DVA), lambda b, h, s, ii, jj: (b * nt + jj[s], h)),
            pl.BlockSpec((t, hps * DVA), lambda b, h, s, ii, jj: (b * nt + jj[s], h)),
            vec, vec, vec, vec,
            pl.BlockSpec((1, DVA), lambda b, h, s, ii, jj: (0, 0)),
        ],
        out_specs=pl.BlockSpec((t, hps * DVA), lambda b, h, s, ii, jj: (b * nt + ii[s], h)),
        scratch_shapes=[pltpu.VMEM((2 * hps, t, DVA), F32)] * 3,
    )
    return pl.pallas_call(
        functools.partial(_dattn_prompt_kernel, t=t, hps=hps),
        grid_spec=grid_spec,
        out_shape=jax.ShapeDtypeStruct((batch * seq, HA * DVA), F32),
        compiler_params=_cparams(("arbitrary", "arbitrary", "arbitrary")),
        name="dattn_prompt",
    )(ii, jj, z, ka, va, *[l.reshape(1, DKA) for l in lams], subln_gain.reshape(1, DVA))


def _rows_by_head(row, nrow):
    head = lax.broadcasted_iota(jnp.int32, (nrow, 1), 0) * HA // nrow
    out = jnp.zeros((nrow, DVA), F32)
    for h in range(HA):
        out = jnp.where(head == h, row[:, h * DVA:(h + 1) * DVA], out)
    return out


def _dattn_sample_kernel(pt_ref, q_ref, kn_ref, vn_ref, *rest, n_pages, page, pps):
    kp_refs = rest[:pps]
    vp_refs = rest[pps:2 * pps]
    lq1, lk1, lq2, lk2, gain_ref, o_ref, qm_ref, m_ref, l_ref, a_ref = rest[2 * pps:]
    p = pl.program_id(1)
    nrow = 2 * HA
    rid = lax.broadcasted_iota(jnp.int32, (nrow, 1), 0)
    slope = _alibi_slope(rid // 2)
    past = n_pages * page

    @pl.when(p == 0)
    def _():
        half = lax.broadcasted_iota(jnp.int32, (nrow, DVA), 1) // DKA
        qm_ref[...] = jnp.where(half == rid % 2, _rows_by_head(q_ref[0], nrow), 0.0)
        m_ref[...] = jnp.full(m_ref.shape, -jnp.inf, F32)
        l_ref[...] = jnp.zeros(l_ref.shape, F32)
        a_ref[...] = jnp.zeros(a_ref.shape, F32)

    qm = qm_ref[...]
    qmb = qm.astype(BF16)
    n = lax.broadcasted_iota(jnp.int32, (1, page * HA), 1)
    own_head = n % HA == rid // 2
    scores = []
    for r in range(pps):
        rel = (past - ((p * pps + r) * page + n // HA)).astype(F32)
        kf = kp_refs[r][...].reshape(page * HA, DVA).astype(BF16)
        sc = _dot_nt(qmb, kf) * (DKA ** -0.5) - slope * rel
        scores.append(jnp.where(own_head, sc, -1e30))
    m_new = m_ref[...]
    for sc in scores:
        m_new = jnp.maximum(m_new, jnp.max(sc, axis=-1, keepdims=True))
    alpha = jnp.exp(m_ref[...] - m_new)
    l_new = alpha * l_ref[...]
    a_new = alpha * a_ref[...]
    for r, sc in enumerate(scores):
        pr = jnp.exp(sc - m_new)
        l_new = l_new + jnp.sum(pr, axis=-1, keepdims=True)
        a_new = a_new + _dot(pr.astype(BF16), vp_refs[r][...].reshape(page * HA, DVA).astype(BF16))
    l_ref[...] = l_new
    a_ref[...] = a_new
    m_ref[...] = m_new

    @pl.when(p == n_pages // pps - 1)
    def _():
        sc_n = jnp.sum(qm * _rows_by_head(kn_ref[0], nrow), axis=-1, keepdims=True) * (DKA ** -0.5)
        m_f = jnp.maximum(m_ref[...], sc_n)
        al = jnp.exp(m_ref[...] - m_f)
        pn = jnp.exp(sc_n - m_f)
        l_f = al * l_ref[...] + pn
        a_f = (al * a_ref[...] + pn * _rows_by_head(vn_ref[0], nrow)) / l_f
        lam = _lambda_full(lq1[...], lk1[...], lq2[...], lk2[...])
        outs = []
        for h in range(HA):
            o = a_f[2 * h:2 * h + 1] - lam * a_f[2 * h + 1:2 * h + 2]
            outs.append(_subln(o, gain_ref[...]))
        o_ref[0] = jnp.concatenate(outs, axis=-1)


def _dattn_sample(z, ka, va, cache_k, cache_v, page_table, lams, subln_gain):
    nb, n_pages = page_table.shape
    page = cache_k.shape[1]
    w = HA * DVA
    row3 = lambda a: a.reshape(nb, 1, a.shape[-1])
    vec = pl.BlockSpec((1, DKA), lambda b, p, pt: (0, 0))
    pps = max(d for d in range(1, PAGES_PER_STEP + 1) if n_pages % d == 0)
    page_specs = [
        pl.BlockSpec((None, page, HA, DVA), lambda b, p, pt, r=r: (pt[b * n_pages + p * pps + r], 0, 0, 0))
        for r in range(pps)]
    grid_spec = pltpu.PrefetchScalarGridSpec(
        num_scalar_prefetch=1,
        grid=(nb, n_pages // pps),
        in_specs=[
            pl.BlockSpec((1, 1, w), lambda b, p, pt: (b, 0, 0)),
            pl.BlockSpec((1, 1, w), lambda b, p, pt: (b, 0, 0)),
            pl.BlockSpec((1, 1, w), lambda b, p, pt: (b, 0, 0)),
            *page_specs, *page_specs,
            vec, vec, vec, vec,
            pl.BlockSpec((1, DVA), lambda b, p, pt: (0, 0)),
        ],
        out_specs=pl.BlockSpec((1, 1, w), lambda b, p, pt: (b, 0, 0)),
        scratch_shapes=[pltpu.VMEM((2 * HA, DVA), F32), pltpu.VMEM((2 * HA, 1), F32),
                        pltpu.VMEM((2 * HA, 1), F32), pltpu.VMEM((2 * HA, DVA), F32)],
    )
    out = pl.pallas_call(
        functools.partial(_dattn_sample_kernel, n_pages=n_pages, page=page, pps=pps),
        grid_spec=grid_spec,
        out_shape=jax.ShapeDtypeStruct((nb, 1, w), F32),
        compiler_params=_cparams(("arbitrary", "arbitrary")),
        name="dattn_sample",
    )(page_table.reshape(-1), row3(z), row3(ka), row3(va), *([cache_k] * pps), *([cache_v] * pps),
      *[l.reshape(1, DKA) for l in lams], subln_gain.reshape(1, DVA))
    return out.reshape(nb, w)


def _ret_log_gamma(h):
    return math.log(1.0 - 2.0 ** (-5.0 - h))


def _ret_theta(width):
    theta = 1.0 / (ROPE_BASE ** jnp.linspace(0.0, 1.0, DKB // 2, dtype=F32))
    return jnp.tile(jnp.repeat(theta, 2), width // DKB).reshape(1, width)


def _rotate_pairs(x, cos, sin):
    n = x.shape[-1]
    lane = lax.broadcasted_iota(jnp.int32, x.shape, x.ndim - 1)
    nxt = pltpu.roll(x, n - 1, axis=x.ndim - 1)
    prv = pltpu.roll(x, 1, axis=x.ndim - 1)
    return x * cos + jnp.where(lane % 2 == 0, -nxt, prv) * sin


def _ret_prompt_kernel(q_ref, k_ref, v_ref, g_ref, th_ref, o_ref, s_ref, dec_ref, cos_ref, sin_ref, *, c):
    ci = pl.program_id(1)
    ri = lax.broadcasted_iota(jnp.int32, (c, c), 0)
    cj = lax.broadcasted_iota(jnp.int32, (c, c), 1)
    idx = lax.broadcasted_iota(jnp.int32, (c, 1), 0)

    @pl.when(jnp.logical_and(pl.program_id(0) == 0, ci == 0))
    def _():
        diff = (ri - cj).astype(F32)
        for h in range(HB):
            dec_ref[h] = jnp.where(diff >= 0, jnp.exp(_ret_log_gamma(h) * jnp.maximum(diff, 0.0)), 0.0)
        within = idx.astype(F32) * th_ref[...]
        cos_ref[...] = jnp.cos(within)
        sin_ref[...] = jnp.sin(within)

    @pl.when(ci == 0)
    def _():
        s_ref[...] = jnp.zeros(s_ref.shape, F32)

    start = (ci * c).astype(F32) * th_ref[...]
    c0, s0 = jnp.cos(start), jnp.sin(start)
    cos = jnp.concatenate([cos_ref[...] * c0 - sin_ref[...] * s0] * (HB * DKB // 128), axis=-1)
    sin = jnp.concatenate([sin_ref[...] * c0 + cos_ref[...] * s0] * (HB * DKB // 128), axis=-1)
    qr = _rotate_pairs(q_ref[...], cos, sin)
    kr = _rotate_pairs(k_ref[...], cos, sin) * (DKB ** -0.5)
    idx_f = idx.astype(F32)
    outs = []
    for h in range(HB):
        lg = _ret_log_gamma(h)
        qh = qr[:, h * DKB:(h + 1) * DKB].astype(BF16)
        kh = kr[:, h * DKB:(h + 1) * DKB]
        vh = v_ref[:, h * DVB:(h + 1) * DVB].astype(BF16)
        s_old = s_ref[0, h]
        scores = _dot_nt(qh, kh.astype(BF16)) * dec_ref[h]
        o = _dot(scores.astype(BF16), vh)
        o = o + _dot(qh, s_old.astype(BF16)) * jnp.exp(lg * (idx_f + 1.0))
        kd = (kh * jnp.exp(lg * (c - 1.0 - idx_f))).astype(BF16)
        s_ref[0, h] = math.exp(lg * c) * s_old + _dot_tn(kd, vh)
        gate = g_ref[:, h * DVB:(h + 1) * DVB]
        outs.append(_rms(o) * (gate * _sigmoid(gate)))
    o_ref[...] = jnp.concatenate(outs, axis=-1)


def _ret_prompt(z, batch, seq, c):
    nc = seq // c
    wq = HB * DKB
    wv = HB * DVB
    return pl.pallas_call(
        functools.partial(_ret_prompt_kernel, c=c),
        grid=(batch, nc),
        in_specs=[
            pl.BlockSpec((c, wq), lambda b, i: (b * nc + i, 2)),
            pl.BlockSpec((c, wq), lambda b, i: (b * nc + i, 3)),
            pl.BlockSpec((c, wv), lambda b, i: (b * nc + i, 2)),
            pl.BlockSpec((c, wv), lambda b, i: (b * nc + i, 3)),
            pl.BlockSpec((1, 128), lambda b, i: (0, 0)),
        ],
        out_specs=[
            pl.BlockSpec((c, wv), lambda b, i: (b * nc + i, 0)),
            pl.BlockSpec((1, HB, DKB, DVB), lambda b, i: (b, 0, 0, 0)),
        ],
        out_shape=[
            jax.ShapeDtypeStruct((batch * seq, wv), F32),
            jax.ShapeDtypeStruct((batch, HB, DKB, DVB), F32),
        ],
        scratch_shapes=[pltpu.VMEM((HB, c, c), F32), pltpu.VMEM((c, 128), F32), pltpu.VMEM((c, 128), F32)],
        compiler_params=_cparams(("arbitrary", "arbitrary")),
        name="ret_prompt",
    )(z, z, z, z, _ret_theta(128))


def _ret_sample_kernel(qk_ref, v_ref, g_ref, th_ref, s_ref, o_ref, so_ref, *, pos):
    ang = float(pos) * th_ref[...]
    x = qk_ref[...]
    ncol = x.shape[1]
    d = lax.broadcasted_iota(jnp.int32, x.shape, 0)
    partner = jnp.where(d % 2 == 0, -pltpu.roll(x, DKB - 1, axis=0), pltpu.roll(x, 1, axis=0))
    qk = x * jnp.cos(ang) + partner * jnp.sin(ang)
    is_k = lax.broadcasted_iota(jnp.int32, (1, ncol), 1) % (2 * HB) >= HB
    qkb = (qk * jnp.where(is_k, DKB ** -0.5, 1.0)).astype(BF16)
    sel_row = lax.broadcasted_iota(jnp.int32, (ncol, HB * DVB), 0)
    sel_head = lax.broadcasted_iota(jnp.int32, (ncol, HB * DVB), 1) // DVB
    rows = []
    for s in range(v_ref.shape[0]):
        q_sel = jnp.where(sel_row == s * 2 * HB + sel_head, 1.0, 0.0)
        kv_sel = jnp.where(sel_row == s * 2 * HB + HB + sel_head, v_ref[s:s + 1, :], 0.0)
        res = _dot(qkb, jnp.concatenate([q_sel, kv_sel], axis=-1).astype(BF16))
        outs = []
        for h in range(HB):
            q_lanes = res[:, h * DVB:(h + 1) * DVB]
            outer = res[:, (HB + h) * DVB:(HB + h + 1) * DVB]
            s_new = math.exp(_ret_log_gamma(h)) * s_ref[s, h] + outer
            so_ref[s, h] = s_new
            o = jnp.sum(q_lanes * s_new, axis=0, keepdims=True)
            gate = g_ref[s:s + 1, h * DVB:(h + 1) * DVB]
            outs.append(_rms(o) * (gate * _sigmoid(gate)))
        rows.append(jnp.concatenate(outs, axis=-1))
    o_ref[...] = jnp.concatenate(rows, axis=0)


def _ret_sample(z, state, pos):
    nb = z.shape[0]
    wq = HB * DKB
    wv = HB * DVB
    sb = 8 if nb % 8 == 0 else nb
    qk_t = jnp.transpose(z[:, wv:wv + 2 * wq].reshape(nb, 2 * HB, DKB), (2, 0, 1)).reshape(DKB, nb * 2 * HB)
    theta_col = _ret_theta(DKB).reshape(DKB, 1)
    return pl.pallas_call(
        functools.partial(_ret_sample_kernel, pos=pos),
        grid=(nb // sb,),
        in_specs=[
            pl.BlockSpec((DKB, sb * 2 * HB), lambda b: (0, b)),
            pl.BlockSpec((sb, wv), lambda b: (b, 2)),
            pl.BlockSpec((sb, wv), lambda b: (b, 3)),
            pl.BlockSpec((DKB, 1), lambda b: (0, 0)),
            pl.BlockSpec((sb, HB, DKB, DVB), lambda b: (b, 0, 0, 0)),
        ],
        out_specs=[
            pl.BlockSpec((sb, wv), lambda b: (b, 0)),
            pl.BlockSpec((sb, HB, DKB, DVB), lambda b: (b, 0, 0, 0)),
        ],
        out_shape=[
            jax.ShapeDtypeStruct((nb, wv), F32),
            jax.ShapeDtypeStruct(state.shape, F32),
        ],
        compiler_params=_cparams(("arbitrary",)),
        name="ret_sample",
    )(qk_t, z, z, theta_col, state)


def _mm_resid_kernel(*refs, n_in):
    a_refs = refs[:n_in]
    w_refs = refs[n_in:2 * n_in]
    x_ref = refs[2 * n_in]
    o_ref = refs[2 * n_in + 1]
    acc = x_ref[...]
    for a_ref, w_ref in zip(a_refs, w_refs):
        acc = acc + _dot(a_ref[...].astype(BF16), w_ref[...])
    o_ref[...] = acc


def _mm_resid(a_list, w_bf16, resid, tm):
    m, n = resid.shape
    tn = n
    in_specs = []
    for a in a_list:
        in_specs.append(pl.BlockSpec((tm, a.shape[1]), lambda i, j: (i, 0)))
    row = 0
    for a in a_list:
        kb = a.shape[1]
        in_specs.append(pl.BlockSpec((kb, tn), lambda i, j, r=row // kb: (r, j)))
        row += kb
    in_specs.append(pl.BlockSpec((tm, tn), lambda i, j: (i, j)))
    return pl.pallas_call(
        functools.partial(_mm_resid_kernel, n_in=len(a_list)),
        grid=(m // tm, n // tn),
        in_specs=in_specs,
        out_specs=pl.BlockSpec((tm, tn), lambda i, j: (i, j)),
        out_shape=jax.ShapeDtypeStruct((m, n), F32),
        compiler_params=_cparams(("arbitrary", "arbitrary")),
        name="mm_resid",
    )(*a_list, *([w_bf16] * len(a_list)), resid)


def _ffn_kernel(*refs, final_norm, emit_bf16, aliased):
    x_ref, g_ref, wg_ref, wu_ref, wd_ref, fg_ref = refs[:6]
    refs = refs[6 + aliased:]
    if emit_bf16:
        o_ref, wgb_ref, wub_ref, wdb_ref, xn_ref, acc_ref = refs
    else:
        o_ref, xn_ref, acc_ref = refs
    j = pl.program_id(1)

    @pl.when(j == 0)
    def _():
        xn_ref[...] = (_rms(x_ref[...]) * g_ref[...]).astype(BF16)
        acc_ref[...] = jnp.zeros(acc_ref.shape, F32)

    wg, wu, wd = wg_ref[...].astype(BF16), wu_ref[...].astype(BF16), wd_ref[...].astype(BF16)
    if emit_bf16:
        wgb_ref[...], wub_ref[...], wdb_ref[...] = wg, wu, wd
    xn = xn_ref[...]
    gate = _dot(xn, wg)
    up = _dot(xn, wu)
    hid = (gate * _sigmoid(gate) * up).astype(BF16)
    acc_ref[...] += _dot(hid, wd)

    @pl.when(j == pl.num_programs(1) - 1)
    def _():
        y = x_ref[...] + acc_ref[...]
        if final_norm:
            y = _rms(y) * fg_ref[...]
        o_ref[...] = y


def _ffn_call(x, gain, wg, wu, wd, final_gain, final_norm, tm, tf, row_tiles, done=None):
    m = x.shape[0]
    dff = wg.shape[1]
    t0, t1 = row_tiles
    emit_bf16 = wg.dtype == F32
    w_specs = [pl.BlockSpec((D_MODEL, tf), lambda i, j: (0, j)),
               pl.BlockSpec((D_MODEL, tf), lambda i, j: (0, j)),
               pl.BlockSpec((tf, D_MODEL), lambda i, j: (j, 0))]
    in_specs = [pl.BlockSpec((tm, D_MODEL), lambda i, j: (i + t0, 0)),
                pl.BlockSpec((1, D_MODEL), lambda i, j: (0, 0)),
                *w_specs,
                pl.BlockSpec((1, D_MODEL), lambda i, j: (0, 0))]
    args = [x, gain.reshape(1, D_MODEL), wg, wu, wd, final_gain.reshape(1, D_MODEL)]
    out_specs = [pl.BlockSpec((tm, D_MODEL), lambda i, j: (i + t0, 0))]
    out_shape = [jax.ShapeDtypeStruct((m, D_MODEL), F32)]
    aliases = {}
    if done is not None:
        in_specs.append(pl.BlockSpec(memory_space=pl.ANY))
        args.append(done)
        aliases = {len(args) - 1: 0}
    if emit_bf16:
        out_specs += w_specs
        out_shape += [jax.ShapeDtypeStruct(w.shape, BF16) for w in (wg, wu, wd)]
    res = pl.pallas_call(
        functools.partial(_ffn_kernel, final_norm=final_norm, emit_bf16=emit_bf16, aliased=done is not None),
        grid=(t1 - t0, dff // tf),
        in_specs=in_specs,
        out_specs=out_specs,
        out_shape=out_shape,
        input_output_aliases=aliases,
        scratch_shapes=[pltpu.VMEM((tm, D_MODEL), BF16), pltpu.VMEM((tm, D_MODEL), F32)],
        compiler_params=_cparams(("arbitrary", "arbitrary")),
        name="ffn",
    )(*args)
    return res if emit_bf16 else res[0]


def _ffn(x, gain, weights, final_gain, final_norm, tm, tf=512):
    n_tiles = x.shape[0] // tm
    if weights[0].dtype != F32:
        return _ffn_call(x, gain, *weights, final_gain, final_norm, tm, tf, (0, n_tiles)), weights
    y, *weights_bf16 = _ffn_call(x, gain, *weights, final_gain, final_norm, tm, tf // 2, (0, 1))
    if n_tiles > 1:
        y = _ffn_call(x, gain, *weights_bf16, final_gain, final_norm, tm, tf, (1, n_tiles), done=y)
    return y, tuple(weights_bf16)


def _norm_and_shift(x_ref, g_ref, prev_ref, carry_ref, i, tiles_per_seq):
    h = _rms(x_ref[...]) * g_ref[...]
    if prev_ref is None:
        tm = h.shape[0]
        first = (i % tiles_per_seq) == 0
        prev_row = jnp.where(first, 0.0, carry_ref[0:1, :])
        rows = lax.broadcasted_iota(jnp.int32, (tm, 1), 0)
        shifted = jnp.where(rows == 0, prev_row, pltpu.roll(h, 1, axis=0))
        carry_ref[0:1, :] = h[tm - 1:tm, :]
    else:
        shifted = prev_ref[...]
    return h, shifted - h


def _rwkv_rkv_kernel(*refs, per_row_prev, tiles_per_seq):
    if per_row_prev:
        x_ref, g_ref, prev_ref, mu_ref, w_ref, o_ref, hl_ref, h_ref, xx_ref, carry_ref = refs
    else:
        x_ref, g_ref, mu_ref, w_ref, o_ref, hl_ref, h_ref, xx_ref, carry_ref = refs
        prev_ref = None
    i = pl.program_id(0)
    j = pl.program_id(1)

    @pl.when(j == 0)
    def _():
        h, xx = _norm_and_shift(x_ref, g_ref, prev_ref, carry_ref, i, tiles_per_seq)
        h_ref[...] = h
        xx_ref[...] = xx
        if per_row_prev:
            hl_ref[...] = h
        else:
            hl_ref[0] = h[h.shape[0] - 1:, :]

    xm = (h_ref[...] + xx_ref[...] * mu_ref[0]).astype(BF16)
    o_ref[...] = _dot(xm, w_ref[...])


def _rwkv_rkv(x, gain, prev, mu_rkv, w_rkv, tm, tiles_per_seq):
    m = x.shape[0]
    tn = D_MODEL
    per_seq = D_MODEL // tn
    per_row_prev = prev is not None
    in_specs = [pl.BlockSpec((tm, D_MODEL), lambda i, j: (i, 0)),
                pl.BlockSpec((1, D_MODEL), lambda i, j: (0, 0))]
    args = [x, gain.reshape(1, D_MODEL)]
    if per_row_prev:
        in_specs.append(pl.BlockSpec((tm, D_MODEL), lambda i, j: (i, 0)))
        args.append(prev)
        hl_spec = pl.BlockSpec((tm, D_MODEL), lambda i, j: (i, 0))
        hl_shape = jax.ShapeDtypeStruct((m, D_MODEL), F32)
    else:
        nseq = m // (tm * tiles_per_seq)
        hl_spec = pl.BlockSpec((1, 1, D_MODEL), lambda i, j: (i // tiles_per_seq, 0, 0))
        hl_shape = jax.ShapeDtypeStruct((nseq, 1, D_MODEL), F32)
    in_specs += [pl.BlockSpec((1, 1, D_MODEL), lambda i, j: (j // per_seq, 0, 0)),
                 pl.BlockSpec((None, D_MODEL, tn), lambda i, j: (j // per_seq, 0, j % per_seq))]
    args += [mu_rkv, w_rkv]
    return pl.pallas_call(
        functools.partial(_rwkv_rkv_kernel, per_row_prev=per_row_prev, tiles_per_seq=tiles_per_seq),
        grid=(m // tm, 3 * per_seq),
        in_specs=in_specs,
        out_specs=[pl.BlockSpec((tm, tn), lambda i, j: (i, j)), hl_spec],
        out_shape=[jax.ShapeDtypeStruct((m, 3 * D_MODEL), F32), hl_shape],
        scratch_shapes=[pltpu.VMEM((tm, D_MODEL), F32), pltpu.VMEM((tm, D_MODEL), F32),
                        pltpu.VMEM((8, D_MODEL), F32)],
        compiler_params=_cparams(("arbitrary", "arbitrary")),
        name="rwkv_rkv",
    )(*args)


def _rwkv_lora_kernel(*refs, per_row_prev, tiles_per_seq):
    if per_row_prev:
        (x_ref, g_ref, prev_ref, mu_ref, w0_ref, w1_ref, w2_ref, a0_ref, a1_ref, a2_ref, g1_ref, g2_ref,
         lw_ref, a_ref, go_ref, carry_ref) = refs
    else:
        (x_ref, g_ref, mu_ref, w0_ref, w1_ref, w2_ref, a0_ref, a1_ref, a2_ref, g1_ref, g2_ref,
         lw_ref, a_ref, go_ref, carry_ref) = refs
        prev_ref = None
    h, xx = _norm_and_shift(x_ref, g_ref, prev_ref, carry_ref, pl.program_id(0), tiles_per_seq)
    xw = (h + xx * mu_ref[0]).astype(BF16)
    xa = (h + xx * mu_ref[1]).astype(BF16)
    xg = (h + xx * mu_ref[2]).astype(BF16)
    zw = w0_ref[...] + _dot(jnp.tanh(_dot(xw, w1_ref[...])).astype(BF16), w2_ref[...])
    lw_ref[...] = -math.exp(-0.5) * _sigmoid(zw)
    a_ref[...] = _sigmoid(a0_ref[...] + _dot(_dot(xa, a1_ref[...]).astype(BF16), a2_ref[...]))
    go_ref[...] = _dot(_sigmoid(_dot(xg, g1_ref[...])).astype(BF16), g2_ref[...])


def _rwkv_lora(x, gain, prev, mu_wag, w0, w1, w2, a0, a1, a2, g1, g2, tm, tiles_per_seq):
    m = x.shape[0]
    per_row_prev = prev is not None
    full = lambda a: pl.BlockSpec(a.shape, lambda i: (0,) * a.ndim)
    row = pl.BlockSpec((tm, D_MODEL), lambda i: (i, 0))
    args = [x, gain.reshape(1, D_MODEL)]
    in_specs = [row, full(args[1])]
    if per_row_prev:
        in_specs.append(row)
        args.append(prev)
    rest = [mu_wag, w0.reshape(1, D_MODEL), w1, w2, a0.reshape(1, D_MODEL), a1, a2, g1, g2]
    in_specs += [full(a) for a in rest]
    args += rest
    return pl.pallas_call(
        functools.partial(_rwkv_lora_kernel, per_row_prev=per_row_prev, tiles_per_seq=tiles_per_seq),
        grid=(m // tm,),
        in_specs=in_specs,
        out_specs=[row, row, row],
        out_shape=[jax.ShapeDtypeStruct((m, D_MODEL), F32)] * 3,
        scratch_shapes=[pltpu.VMEM((8, D_MODEL), F32)],
        compiler_params=_cparams(("arbitrary",)),
        name="rwkv_lora",
    )(*args)


def _wkv_keys(k, a, kk_gain, ka_gain, head_sum):
    kk = k * kk_gain
    kk = kk / jnp.maximum(jnp.sqrt(head_sum(kk * kk)), 1e-12)
    return kk, kk * a, k * (1.0 + (a - 1.0) * ka_gain)


def _head_norm(o, head_sum):
    mean = head_sum(o) * (1.0 / NC)
    var = head_sum(jnp.square(o - mean)) * (1.0 / NC)
    return (o - mean) * lax.rsqrt(var + GN_EPS)


def _wkv_finish(on, r, k2, v, g, rk, ln_g, ln_b, head_sum):
    bonus = head_sum(r * k2 * rk) * v
    return (on * ln_g + ln_b + bonus) * g


def _pair_sum(x):
    lane = lax.broadcasted_iota(jnp.int32, x.shape, 1)
    s0 = jnp.sum(x[:, :NC], axis=-1, keepdims=True)
    s1 = jnp.sum(x[:, NC:], axis=-1, keepdims=True)
    return jnp.where(lane < NC, s0, s1)


def _bmm(a, b):
    return jnp.einsum("cij,cjk->cik", a, b, preferred_element_type=F32)


def _bmm_nt(a, b):
    return jnp.einsum("cik,cjk->cij", a, b, preferred_element_type=F32)


def _wkv_prompt_kernel(r_ref, k_ref, v_ref, lw_ref, a_ref, kkg_ref, kag_ref, rk_ref, g_ref, lng_ref, lnb_ref,
                       y_ref, so_ref, st_ref, gh_ref, ghb_ref, e0t_ref, ft_ref, wend_ref, bonus_ref, *, tc, nt):
    c = WKV_CHUNK
    nch = tc // c
    s = pl.program_id(0)

    @pl.when(s == 0)
    def _():
        for ref in (st_ref, gh_ref, ghb_ref, e0t_ref, ft_ref, wend_ref, bonus_ref):
            ref[...] = jnp.zeros(ref.shape, ref.dtype)

    prev = jnp.maximum(s - 1, 0)
    slot = prev % 2
    first = prev % nt == 0
    ss = [jnp.where(first, 0.0, st_ref[0]), jnp.where(first, 0.0, st_ref[1])]
    o_chunks = [[], []]
    lane_sum = lambda t: jnp.sum(t, axis=-1, keepdims=True)
    n_points = 2 + int(math.log2(c))
    done = [0]

    def advance_stage2(point):
        upto = nch * (point + 1) // n_points
        for ch in range(done[0], upto):
            w_end = wend_ref[slot, ch]
            for hh in range(2):
                n = hh * nch + ch
                other = slice((1 - hh) * NC, (2 - hh) * NC)
                sb = ss[hh].astype(BF16)
                o = _dot_nt(ghb_ref[slot, n], sb) + gh_ref[slot, n][:, other]
                o_chunks[hh].append(_head_norm(o, lane_sum))
                ss[hh] = ss[hh] * w_end + _dot(sb, e0t_ref[slot, n]) + ft_ref[slot, n]
        done[0] = upto

    ri = lax.broadcasted_iota(jnp.int32, (c, c), 0)
    cj = lax.broadcasted_iota(jnp.int32, (c, c), 1)
    tri = (ri >= cj).astype(BF16)
    gi = lax.broadcasted_iota(jnp.int32, (2 * c, 2 * c), 0)
    gj = lax.broadcasted_iota(jnp.int32, (2 * c, 2 * c), 1) % c
    gmask = (gj < jnp.where(gi < c, gi, gi - c + 1))[None]

    r = r_ref[...]
    v = v_ref[...]
    kk, bv, k2 = _wkv_keys(k_ref[...], a_ref[...], kkg_ref[...], kag_ref[...], _pair_sum)
    to3 = lambda x: x.reshape(nch, c, 2 * NC)
    lw3 = to3(lw_ref[...])
    p0 = lw3.astype(BF16)
    r1 = lw3 - p0.astype(F32)
    p1 = r1.astype(BF16)
    p2 = (r1 - p1.astype(F32)).astype(BF16)
    trib = jnp.broadcast_to(tri[None], (nch, c, c))
    cum = _bmm(trib, p0) + (_bmm(trib, p1) + _bmm(trib, p2))
    cum_last = cum[:, c - 1:c, :]
    w_rest = jnp.exp(cum_last - cum)
    w_inv = jnp.exp(-cum)
    am = -to3(kk) * jnp.exp(cum - lw3)
    rm = to3(r) * jnp.exp(cum)
    bm = (to3(bv) * w_inv).astype(BF16)
    km = (to3(k2) * w_inv).astype(BF16)
    bt = (to3(bv) * w_rest).astype(BF16)
    kt = (to3(k2) * w_rest).astype(BF16)
    v3 = to3(v)
    advance_stage2(0)

    both = lambda x: jnp.concatenate([x, x], axis=0)
    lane = lax.broadcasted_iota(jnp.int32, (2 * nch, 1, 2 * NC), 2)
    head = lax.broadcasted_iota(jnp.int32, (2 * nch, 1, 2 * NC), 0) // nch
    own = (lane // NC) == head
    v_sw = both(pltpu.roll(v, NC, axis=1).reshape(nch, c, 2 * NC))
    x1 = both(jnp.concatenate([am, rm], axis=1).astype(BF16))
    x2 = jnp.where(own, both(jnp.concatenate([bm, km], axis=1)), 0.0)
    g4 = jnp.where(gmask, _bmm_nt(x1, x2), 0.0)
    lv = _bmm(g4[:, :, c:].astype(BF16), v_sw.astype(BF16))
    advance_stage2(1)
    x = jnp.where(own, both(am), lv[:, :c])
    lp = g4[:, :c, :c].astype(BF16)
    steps = int(math.log2(c))
    for i in range(steps - 1):
        prod = _bmm(lp, jnp.concatenate([x.astype(BF16), lp], axis=-1))
        x = x + prod[:, :, :2 * NC]
        lp = prod[:, :, 2 * NC:].astype(BF16)
        advance_stage2(2 + i)
    x = x + _bmm(lp, x.astype(BF16))
    advance_stage2(n_points - 1)
    gh_new = _bmm(g4[:, c:, :c].astype(BF16), x.astype(BF16)) + jnp.where(own, both(rm), lv[:, c:])
    btk = jnp.where(own, both(jnp.concatenate([bt, kt], axis=1)), 0.0)
    pqv = jnp.concatenate([x, jnp.where(own, 0.0, v_sw)], axis=1).astype(BF16)
    st_ref[0] = ss[0]
    st_ref[1] = ss[1]
    so_ref[0, 0] = ss[0][:, :NC]
    so_ref[0, 1] = ss[1][:, NC:]
    on = jnp.concatenate([jnp.concatenate(oc, axis=0) for oc in o_chunks], axis=-1)
    y_ref[...] = (on * lng_ref[...] + lnb_ref[...] + bonus_ref[slot]) * g_ref[...]

    nxt = s % 2
    gh_ref[nxt] = gh_new
    ghb_ref[nxt] = gh_new.astype(BF16)
    for n in range(2 * nch):
        eft = _dot_tn(pqv[n], btk[n])
        e0t_ref[nxt, n] = eft.astype(BF16)
        ft_ref[nxt, n] = eft[NC:] if n < nch else eft[:NC]
    wend_ref[nxt] = jnp.exp(cum_last)
    bonus_ref[nxt] = _pair_sum(r * k2 * rk_ref[...]) * v


def _wkv_prompt(rkv, lw, a, g, kk_gain, ka_gain, rk, ln_g, ln_b, batch, seq, tc):
    nt = seq // tc
    nhp = HC // 2
    nblk = batch * nhp * nt
    nch = tc // WKV_CHUNK

    def rows_cols(blk):
        b = blk // (nhp * nt)
        return b * nt + blk % nt, (blk // nt) % nhp, b

    cur = lambda s: jnp.minimum(s, nblk - 1)
    prv = lambda s: jnp.maximum(s - 1, 0)
    tok = lambda off, which: pl.BlockSpec(
        (tc, 2 * NC), lambda s: (rows_cols(which(s))[0], off + rows_cols(which(s))[1]))
    par = lambda which: pl.BlockSpec((1, 2 * NC), lambda s: (0, rows_cols(which(s))[1]))
    return pl.pallas_call(
        functools.partial(_wkv_prompt_kernel, tc=tc, nt=nt),
        grid=(nblk + 1,),
        in_specs=[tok(0, cur), tok(nhp, cur), tok(2 * nhp, cur), tok(0, cur), tok(0, cur),
                  par(cur), par(cur), par(cur), tok(0, prv), par(prv), par(prv)],
        out_specs=[
            tok(0, prv),
            pl.BlockSpec((1, 2, NC, NC), lambda s: (rows_cols(prv(s))[2], rows_cols(prv(s))[1], 0, 0)),
        ],
        out_shape=[
            jax.ShapeDtypeStruct((batch * seq, D_MODEL), F32),
            jax.ShapeDtypeStruct((batch, HC, NC, NC), F32),
        ],
        scratch_shapes=[
            pltpu.VMEM((2, NC, 2 * NC), F32),
            pltpu.VMEM((2, 2 * nch, WKV_CHUNK, 2 * NC), F32),
            pltpu.VMEM((2, 2 * nch, WKV_CHUNK, 2 * NC), BF16),
            pltpu.VMEM((2, 2 * nch, 2 * NC, 2 * NC), BF16),
            pltpu.VMEM((2, 2 * nch, NC, 2 * NC), F32),
            pltpu.VMEM((2, nch, 1, 2 * NC), F32),
            pltpu.VMEM((2, tc, 2 * NC), F32),
        ],
        compiler_params=_cparams(("arbitrary",)),
        name="wkv_prompt",
    )(rkv, rkv, rkv, lw, a, kk_gain, ka_gain, rk, g, ln_g, ln_b)


def _wkv_sample_kernel(r_ref, k_ref, v_ref, lw_ref, a_ref, g_ref, kkg_ref, kag_ref, rk_ref, lng_ref, lnb_ref,
                       s_ref, y_ref, so_ref, o_ref):
    ys = []
    col_sum = lambda x: jnp.sum(x, axis=0, keepdims=True)
    for hh in range(2):
        sl = slice(hh * NC, (hh + 1) * NC)
        t = lambda ref: ref[...].T[sl, :]
        r, k, v, a, g = t(r_ref), t(k_ref), t(v_ref), t(a_ref), t(g_ref)
        w = jnp.exp(t(lw_ref))
        kk, bv, k2 = _wkv_keys(k, a, kkg_ref[sl, :], kag_ref[sl, :], col_sum)
        for i in range(NC):
            s_old = s_ref[hh, i]
            sk = jnp.sum(s_old * kk, axis=0, keepdims=True)
            s_new = s_old * w - sk * bv + v[i:i + 1, :] * k2
            so_ref[hh, i] = s_new
            o_ref[i:i + 1, :] = jnp.sum(s_new * r, axis=0, keepdims=True)
        ys.append(_wkv_finish(_head_norm(o_ref[...], col_sum), r, k2, v, g,rk_ref[sl, :], lng_ref[sl, :], lnb_ref[sl, :], col_sum))
    y_ref[...] = jnp.concatenate(ys, axis=0).T


def _wkv_sample(rkv, lw, a, g, kk_gain, ka_gain, rk, ln_g, ln_b, state_t):
    nb = lw.shape[0]
    nhp = HC // 2
    tok = lambda off: pl.BlockSpec((nb, 2 * NC), lambda hp: (0, off + hp))
    par = pl.BlockSpec((2 * NC, 1), lambda hp: (hp, 0))
    col = lambda p: p.reshape(D_MODEL, 1)
    st_spec = pl.BlockSpec((2, NC, NC, nb), lambda hp: (hp, 0, 0, 0))
    return pl.pallas_call(
        _wkv_sample_kernel,
        grid=(nhp,),
        in_specs=[tok(0), tok(nhp), tok(2 * nhp), tok(0), tok(0), tok(0), par, par, par, par, par, st_spec],
        out_specs=[pl.BlockSpec((nb, 2 * NC), lambda hp: (0, hp)), st_spec],
        out_shape=[
            jax.ShapeDtypeStruct((nb, D_MODEL), F32),
            jax.ShapeDtypeStruct(state_t.shape, F32),
        ],
        scratch_shapes=[pltpu.VMEM((NC, nb), F32)],
        compiler_params=_cparams(("arbitrary",)),
        name="wkv_sample",
    )(rkv, rkv, rkv, lw, a, g, col(kk_gain), col(ka_gain), col(rk), col(ln_g), col(ln_b), state_t)


def _row_tile(m, want):
    return min(m, want)


def _trunk(x, batch, seq, past, W):
    m = x.shape[0]
    prompt = past is None
    tm = _row_tile(m, 512)
    lams = (W["lambda_q1"], W["lambda_k1"], W["lambda_q2"], W["lambda_k2"])

    ka, va, z = _inproj0(x, W["norm0_mix"], W["w_in0"], tm)
    if prompt:
        o_a = _dattn_prompt(z, ka, va, lams, W["subln_gain"], batch, seq, min(seq, 512))
        o_b, ret_new = _ret_prompt(z, batch, seq, min(seq, 256))
    else:
        o_a = _dattn_sample(z, ka, va, past["cache_k"], past["cache_v"], past["page_table"], lams,
                            W["subln_gain"])
        o_b, ret_new = _ret_sample(z, past["state_ret"], past["pos"])
    x = _mm_resid([o_a, o_b], W["w_out0"], x, _row_tile(m, 256))
    x, W["ffn0"] = _ffn(x, W["norm0_ffn"], W["ffn0"], W["norm_final"], False, tm)

    tiles_per_seq = max(seq // tm, 1)
    prev = None if prompt else past["state_shift"]
    rkv, hlast = _rwkv_rkv(x, W["norm1_mix"], prev, W["mu_rkv"], W["w_rkv"], tm, tiles_per_seq)
    tl = _row_tile(m, 256)
    lw, a, g = _rwkv_lora(x, W["norm1_mix"], prev, W["mu_wag"], W["decay_w0"], W["decay_w1"], W["decay_w2"],
                          W["aaa_a0"], W["aaa_a1"], W["aaa_a2"], W["gate_g1"], W["gate_g2"],
                          tl, max(seq // tl, 1))
    head_params = (W["k_k"], W["k_a"], W["r_k"], W["lnx_gain"], W["lnx_bias"])
    if prompt:
        y, wkv_new = _wkv_prompt(rkv, lw, a, g, *head_params, batch, seq, min(seq, 512))
        shift_new = hlast.reshape(batch, D_MODEL)
    else:
        state_t = jnp.transpose(past["state_wkv"], (1, 2, 3, 0))
        y, wkv_t = _wkv_sample(rkv, lw, a, g, *head_params, state_t)
        wkv_new = jnp.transpose(wkv_t, (3, 0, 1, 2))
        shift_new = hlast
    x = _mm_resid([y], W["w_out1"], x, _row_tile(m, 256))
    x, W["ffn1"] = _ffn(x, W["norm1_ffn"], W["ffn1"], W["norm_final"], True, tm)
    return x, ka, va, ret_new, wkv_new, shift_new


def kernel(x_prompt, x_sample, cache_k, cache_v, page_table, state_ret, state_wkv, state_shift, norm0_mix, w_in0, lambda_q1, lambda_k1, lambda_q2, lambda_k2, subln_gain, w_out0, norm0_ffn, w_gate0, w_up0, w_down0, norm1_mix, mu1, w_r1, w_k1, w_v1, decay_w0, decay_w1, decay_w2, aaa_a0, aaa_a1, aaa_a2, gate_g1, gate_g2, k_k, k_a, r_k, lnx_gain, lnx_bias, w_out1, norm1_ffn, w_gate1, w_up1, w_down1, norm_final):
    bf = lambda w: w.astype(BF16)
    row = lambda p: p.reshape(1, D_MODEL)
    W = dict(
        norm0_mix=norm0_mix, w_in0=bf(w_in0), lambda_q1=lambda_q1, lambda_k1=lambda_k1, lambda_q2=lambda_q2,
        lambda_k2=lambda_k2, subln_gain=subln_gain, w_out0=bf(w_out0), norm0_ffn=norm0_ffn,
        ffn0=(w_gate0, w_up0, w_down0),
        norm1_mix=norm1_mix,
        mu_rkv=jnp.stack([mu1[0], mu1[2], mu1[3]]).reshape(3, 1, D_MODEL),
        mu_wag=jnp.stack([mu1[1], mu1[4], mu1[5]]).reshape(3, 1, D_MODEL),
        w_rkv=jnp.stack([bf(w_r1), bf(w_k1), bf(w_v1)]),
        decay_w0=decay_w0, decay_w1=bf(decay_w1), decay_w2=bf(decay_w2),
        aaa_a0=aaa_a0, aaa_a1=bf(aaa_a1), aaa_a2=bf(aaa_a2), gate_g1=bf(gate_g1), gate_g2=bf(gate_g2),
        k_k=row(k_k), k_a=row(k_a), r_k=row(r_k), lnx_gain=row(lnx_gain), lnx_bias=row(lnx_bias),
        w_out1=bf(w_out1), norm1_ffn=norm1_ffn, ffn1=(w_gate1, w_up1, w_down1),
        norm_final=norm_final,
    )
    bsz, seq = x_prompt.shape[:2]
    y_p, k_p, v_p, ret_p, wkv_p, shift_p = _trunk(x_prompt.reshape(bsz * seq, D_MODEL), bsz, seq, None, W)
    nb, dseq = x_sample.shape[:2]
    past_len = page_table.shape[1] * cache_k.shape[1]
    past = dict(cache_k=cache_k, cache_v=cache_v, page_table=page_table, state_ret=state_ret,
                state_wkv=state_wkv, state_shift=state_shift, pos=past_len)
    y_s, k_s, v_s, ret_s, wkv_s, shift_s = _trunk(x_sample.reshape(nb * dseq, D_MODEL), nb, dseq, past, W)
    return (y_p.reshape(bsz, seq, D_MODEL), y_s.reshape(nb, dseq, D_MODEL),
            k_p.reshape(bsz, seq, HA, 2 * DKA), v_p.reshape(bsz, seq, HA, DVA),
            k_s.reshape(nb, dseq, HA, 2 * DKA), v_s.reshape(nb, dseq, HA, DVA),
            ret_p, ret_s, wkv_p, wkv_s, shift_p, shift_s)
```

```python
import functools
import math

import jax
import jax.numpy as jnp
from jax import lax
from jax.experimental import pallas as pl
from jax.experimental.pallas import tpu as pltpu

F32 = jnp.float32
BF16 = jnp.bfloat16

D_MODEL = 2048
EPS = 1e-6
DKA = 64
DVA = 128
HA = 8
LAMBDA_INIT = 0.8 - 0.6 * math.exp(-0.3 * 0)
DKB = 64
DVB = 128
HB = 8
ROPE_BASE = 10000.0
NC = 64
HC = D_MODEL // NC
GN_EPS = 64e-5
WKV_CHUNK = 64
PAGES_PER_STEP = 16
VMEM_LIMIT = 48 * 1024 * 1024
INPROJ_VMEM_LIMIT = 56 * 1024 * 1024


def _cparams(sem):
    return pltpu.CompilerParams(dimension_semantics=sem, vmem_limit_bytes=VMEM_LIMIT)


def _rms(x, eps=EPS):
    return x * lax.rsqrt(jnp.mean(x * x, axis=-1, keepdims=True) + eps)


def _sigmoid(x):
    return 1.0 / (1.0 + jnp.exp(-x))


def _dot(a, b):
    return jnp.dot(a, b, preferred_element_type=F32)


def _dot_nt(a, b):
    return lax.dot_general(a, b, (((1,), (1,)), ((), ())), preferred_element_type=F32)


def _dot_tn(a, b):
    return lax.dot_general(a, b, (((0,), (0,)), ((), ())), preferred_element_type=F32)


def _split2(x):
    hi = x.astype(BF16)
    lo = (x - hi.astype(F32)).astype(BF16)
    return hi, lo


def _dot_f32(a, b, dot=_dot):
    ah, al = _split2(a)
    bh, bl = _split2(b)
    return dot(ah, bh) + (dot(ah, bl) + dot(al, bh))


def _inproj0_kernel(x_ref, g_ref, w_ref, ka_ref, va_ref, z_ref, xn_ref):
    j = pl.program_id(1)

    @pl.when(j == 0)
    def _():
        xn_ref[...] = (_rms(x_ref[...]) * g_ref[...]).astype(BF16)

    acc = _dot(xn_ref[...], w_ref[...])
    half = acc.shape[1] // 2

    @pl.when(j == 0)
    def _():
        z_ref[:, :half] = acc[:, :half]
        ka_ref[...] = acc[:, half:]

    @pl.when(j == 1)
    def _():
        va_ref[...] = acc[:, :half]
        z_ref[:, half:] = acc[:, half:]

    @pl.when(j == 2)
    def _():
        z_ref[...] = acc


def _inproj0(x, gain, w_bf16, tm):
    m = x.shape[0]
    tn = 2048
    nj = w_bf16.shape[1] // tn
    return pl.pallas_call(
        _inproj0_kernel,
        grid=(m // tm, nj),
        in_specs=[
            pl.BlockSpec((tm, D_MODEL), lambda i, j: (i, 0)),
            pl.BlockSpec((1, D_MODEL), lambda i, j: (0, 0)),
            pl.BlockSpec((D_MODEL, tn), lambda i, j: (0, j)),
        ],
        out_specs=[
            pl.BlockSpec((tm, tn // 2), lambda i, j: (i, 0)),
            pl.BlockSpec((tm, tn // 2), lambda i, j: (i, 0)),
            pl.BlockSpec((tm, tn), lambda i, j: (i, j // 2)),
        ],
        out_shape=[
            jax.ShapeDtypeStruct((m, tn // 2), F32),
            jax.ShapeDtypeStruct((m, tn // 2), F32),
            jax.ShapeDtypeStruct((m, 2 * tn), F32),
        ],
        scratch_shapes=[pltpu.VMEM((tm, D_MODEL), BF16)],
        compiler_params=pltpu.CompilerParams(dimension_semantics=("arbitrary", "arbitrary"),
                                             vmem_limit_bytes=INPROJ_VMEM_LIMIT),
        name="inproj0",
    )(x, gain.reshape(1, D_MODEL), w_bf16)


def _lambda_full(lq1, lk1, lq2, lk2):
    s1 = jnp.sum(lq1 * lk1, axis=-1, keepdims=True)
    s2 = jnp.sum(lq2 * lk2, axis=-1, keepdims=True)
    return jnp.exp(s1) - jnp.exp(s2) + LAMBDA_INIT


def _alibi_slope(h_vec):
    slope = jnp.zeros(h_vec.shape, F32)
    for h in range(HA):
        slope = jnp.where(h_vec == h, 2.0 ** (-(h + 1)), slope)
    return slope


def _subln(o, gain):
    return _rms(o) * gain * (1.0 - LAMBDA_INIT)


def _dattn_prompt_kernel(ii_ref, jj_ref, q_ref, k_ref, v_ref, lq1, lk1, lq2, lk2, gain_ref, o_ref,
                         m_ref, l_ref, a_ref, *, t, hps):
    hg = pl.program_id(1)
    s = pl.program_id(2)
    i = ii_ref[s]
    j = jj_ref[s]

    @pl.when(j == 0)
    def _():
        m_ref[...] = jnp.full(m_ref.shape, -jnp.inf, F32)
        l_ref[...] = jnp.zeros(l_ref.shape, F32)
        a_ref[...] = jnp.zeros(a_ref.shape, F32)

    def step(diagonal):
        col = lax.broadcasted_iota(jnp.int32, (1, t), 1)
        rel = ((j - i) * t + col - (t - 1)).astype(F32)
        if diagonal:
            keep = (lax.broadcasted_iota(jnp.int32, (t, t), 1) <= lax.broadcasted_iota(jnp.int32, (t, t), 0))
        for hh in range(hps):
            hl = slice(hh * DVA, (hh + 1) * DVA)
            bias = _alibi_slope(jnp.full((1, 1), hg * hps + hh, jnp.int32)) * rel
            q = (q_ref[:, hl] * (DKA ** -0.5)).astype(BF16)
            k = k_ref[:, hl].astype(BF16)
            v = v_ref[:, hl].astype(BF16)
            for half in range(2):
                idx = 2 * hh + half
                sl = slice(half * DKA, (half + 1) * DKA)
                sc = _dot_nt(q[:, sl], k[:, sl]) + bias
                if diagonal:
                    sc = jnp.where(keep, sc, -1e30)
                m_prev = m_ref[idx]
                m_new = jnp.maximum(m_prev, jnp.max(sc, axis=-1, keepdims=True))
                alpha = jnp.exp(m_prev - m_new)
                p = jnp.exp(sc - jnp.concatenate([m_new] * (t // DVA), axis=-1))
                l_ref[idx] = alpha * l_ref[idx] + jnp.sum(p, axis=-1, keepdims=True)
                a_ref[idx] = alpha * a_ref[idx] + _dot(p.astype(BF16), v)
                m_ref[idx] = m_new

    @pl.when(j < i)
    def _():
        step(False)

    @pl.when(j == i)
    def _():
        step(True)
        lam = _lambda_full(lq1[...], lk1[...], lq2[...], lk2[...])
        for hh in range(hps):
            o = a_ref[2 * hh] / l_ref[2 * hh] - lam * (a_ref[2 * hh + 1] / l_ref[2 * hh + 1])
            o_ref[:, hh * DVA:(hh + 1) * DVA] = _subln(o, gain_ref[...])


def _dattn_prompt(z, ka, va, lams, subln_gain, batch, seq, t):
    nt = seq // t
    hps = 4
    pairs = [(i, j) for i in range(nt) for j in range(i + 1)]
    ii = jnp.asarray([p[0] for p in pairs], jnp.int32)
    jj = jnp.asarray([p[1] for p in pairs], jnp.int32)
    vec = pl.BlockSpec((1, DKA), lambda b, h, s, ii, jj: (0, 0))
    grid_spec = pltpu.PrefetchScalarGridSpec(
        num_scalar_prefetch=2,
        grid=(batch, HA // hps, len(pairs)),
        in_specs=[
            pl.BlockSpec((t, hps * DVA), lambda b, h, s, ii, jj: (b * nt + ii[s], h)),
            pl.BlockSpec((t, hps * DVA), lambda b, h, s, ii, jj: (b * nt + jj[s], h)),
            pl.BlockSpec((t, hps * DVA), lambda b, h, s, ii, jj: (b * nt + jj[s], h)),
            vec, vec, vec, vec,
            pl.BlockSpec((1, DVA), lambda b, h, s, ii, jj: (0, 0)),
        ],
        out_specs=pl.BlockSpec((t, hps * DVA), lambda b, h, s, ii, jj: (b * nt + ii[s], h)),
        scratch_shapes=[pltpu.VMEM((2 * hps, t, DVA), F32)] * 3,
    )
    return pl.pallas_call(
        functools.partial(_dattn_prompt_kernel, t=t, hps=hps),
        grid_spec=grid_spec,
        out_shape=jax.ShapeDtypeStruct((batch * seq, HA * DVA), F32),
        compiler_params=_cparams(("arbitrary", "arbitrary", "arbitrary")),
        name="dattn_prompt",
    )(ii, jj, z, ka, va, *[l.reshape(1, DKA) for l in lams], subln_gain.reshape(1, DVA))


def _rows_by_head(row, nrow):
    head = lax.broadcasted_iota(jnp.int32, (nrow, 1), 0) * HA // nrow
    out = jnp.zeros((nrow, DVA), F32)
    for h in range(HA):
        out = jnp.where(head == h, row[:, h * DVA:(h + 1) * DVA], out)
    return out


def _dattn_sample_kernel(pt_ref, q_ref, kn_ref, vn_ref, *rest, n_pages, page, pps):
    kp_refs = rest[:pps]
    vp_refs = rest[pps:2 * pps]
    lq1, lk1, lq2, lk2, gain_ref, o_ref, qm_ref, m_ref, l_ref, a_ref = rest[2 * pps:]
    p = pl.program_id(1)
    nrow = 2 * HA
    rid = lax.broadcasted_iota(jnp.int32, (nrow, 1), 0)
    slope = _alibi_slope(rid // 2)
    past = n_pages * page

    @pl.when(p == 0)
    def _():
        half = lax.broadcasted_iota(jnp.int32, (nrow, DVA), 1) // DKA
        qm_ref[...] = jnp.where(half == rid % 2, _rows_by_head(q_ref[0], nrow), 0.0)
        m_ref[...] = jnp.full(m_ref.shape, -jnp.inf, F32)
        l_ref[...] = jnp.zeros(l_ref.shape, F32)
        a_ref[...] = jnp.zeros(a_ref.shape, F32)

    qm = qm_ref[...]
    qmb = qm.astype(BF16)
    n = lax.broadcasted_iota(jnp.int32, (1, page * HA), 1)
    own_head = n % HA == rid // 2
    scores = []
    for r in range(pps):
        rel = (past - ((p * pps + r) * page + n // HA)).astype(F32)
        kf = kp_refs[r][...].reshape(page * HA, DVA).astype(BF16)
        sc = _dot_nt(qmb, kf) * (DKA ** -0.5) - slope * rel
        scores.append(jnp.where(own_head, sc, -1e30))
    m_new = m_ref[...]
    for sc in scores:
        m_new = jnp.maximum(m_new, jnp.max(sc, axis=-1, keepdims=True))
    alpha = jnp.exp(m_ref[...] - m_new)
    l_new = alpha * l_ref[...]
    a_new = alpha * a_ref[...]
    for r, sc in enumerate(scores):
        pr = jnp.exp(sc - m_new)
        l_new = l_new + jnp.sum(pr, axis=-1, keepdims=True)
        a_new = a_new + _dot(pr.astype(BF16), vp_refs[r][...].reshape(page * HA, DVA).astype(BF16))
    l_ref[...] = l_new
    a_ref[...] = a_new
    m_ref[...] = m_new

    @pl.when(p == n_pages // pps - 1)
    def _():
        sc_n = jnp.sum(qm * _rows_by_head(kn_ref[0], nrow), axis=-1, keepdims=True) * (DKA ** -0.5)
        m_f = jnp.maximum(m_ref[...], sc_n)
        al = jnp.exp(m_ref[...] - m_f)
        pn = jnp.exp(sc_n - m_f)
        l_f = al * l_ref[...] + pn
        a_f = (al * a_ref[...] + pn * _rows_by_head(vn_ref[0], nrow)) / l_f
        lam = _lambda_full(lq1[...], lk1[...], lq2[...], lk2[...])
        outs = []
        for h in range(HA):
            o = a_f[2 * h:2 * h + 1] - lam * a_f[2 * h + 1:2 * h + 2]
            outs.append(_subln(o, gain_ref[...]))
        o_ref[0] = jnp.concatenate(outs, axis=-1)


def _dattn_sample(z, ka, va, cache_k, cache_v, page_table, lams, subln_gain):
    nb, n_pages = page_table.shape
    page = cache_k.shape[1]
    w = HA * DVA
    row3 = lambda a: a.reshape(nb, 1, a.shape[-1])
    vec = pl.BlockSpec((1, DKA), lambda b, p, pt: (0, 0))
    pps = max(d for d in range(1, PAGES_PER_STEP + 1) if n_pages % d == 0)
    page_specs = [
        pl.BlockSpec((None, page, HA, DVA), lambda b, p, pt, r=r: (pt[b * n_pages + p * pps + r], 0, 0, 0))
        for r in range(pps)]
    grid_spec = pltpu.PrefetchScalarGridSpec(
        num_scalar_prefetch=1,
        grid=(nb, n_pages // pps),
        in_specs=[
            pl.BlockSpec((1, 1, w), lambda b, p, pt: (b, 0, 0)),
            pl.BlockSpec((1, 1, w), lambda b, p, pt: (b, 0, 0)),
            pl.BlockSpec((1, 1, w), lambda b, p, pt: (b, 0, 0)),
            *page_specs, *page_specs,
            vec, vec, vec, vec,
            pl.BlockSpec((1, DVA), lambda b, p, pt: (0, 0)),
        ],
        out_specs=pl.BlockSpec((1, 1, w), lambda b, p, pt: (b, 0, 0)),
        scratch_shapes=[pltpu.VMEM((2 * HA, DVA), F32), pltpu.VMEM((2 * HA, 1), F32),
                        pltpu.VMEM((2 * HA, 1), F32), pltpu.VMEM((2 * HA, DVA), F32)],
    )
    out = pl.pallas_call(
        functools.partial(_dattn_sample_kernel, n_pages=n_pages, page=page, pps=pps),
        grid_spec=grid_spec,
        out_shape=jax.ShapeDtypeStruct((nb, 1, w), F32),
        compiler_params=_cparams(("arbitrary", "arbitrary")),
        name="dattn_sample",
    )(page_table.reshape(-1), row3(z), row3(ka), row3(va), *([cache_k] * pps), *([cache_v] * pps),
      *[l.reshape(1, DKA) for l in lams], subln_gain.reshape(1, DVA))
    return out.reshape(nb, w)


def _ret_log_gamma(h):
    return math.log(1.0 - 2.0 ** (-5.0 - h))


def _ret_theta(width):
    theta = 1.0 / (ROPE_BASE ** jnp.linspace(0.0, 1.0, DKB // 2, dtype=F32))
    return jnp.tile(jnp.repeat(theta, 2), width // DKB).reshape(1, width)


def _rotate_pairs(x, cos, sin):
    n = x.shape[-1]
    lane = lax.broadcasted_iota(jnp.int32, x.shape, x.ndim - 1)
    nxt = pltpu.roll(x, n - 1, axis=x.ndim - 1)
    prv = pltpu.roll(x, 1, axis=x.ndim - 1)
    return x * cos + jnp.where(lane % 2 == 0, -nxt, prv) * sin


def _ret_prompt_kernel(q_ref, k_ref, v_ref, g_ref, th_ref, o_ref, s_ref, dec_ref, cos_ref, sin_ref, *, c):
    ci = pl.program_id(1)
    ri = lax.broadcasted_iota(jnp.int32, (c, c), 0)
    cj = lax.broadcasted_iota(jnp.int32, (c, c), 1)
    idx = lax.broadcasted_iota(jnp.int32, (c, 1), 0)

    @pl.when(jnp.logical_and(pl.program_id(0) == 0, ci == 0))
    def _():
        diff = (ri - cj).astype(F32)
        for h in range(HB):
            dec_ref[h] = jnp.where(diff >= 0, jnp.exp(_ret_log_gamma(h) * jnp.maximum(diff, 0.0)), 0.0)
        within = idx.astype(F32) * th_ref[...]
        cos_ref[...] = jnp.cos(within)
        sin_ref[...] = jnp.sin(within)

    @pl.when(ci == 0)
    def _():
        s_ref[...] = jnp.zeros(s_ref.shape, F32)

    start = (ci * c).astype(F32) * th_ref[...]
    c0, s0 = jnp.cos(start), jnp.sin(start)
    cos = jnp.concatenate([cos_ref[...] * c0 - sin_ref[...] * s0] * (HB * DKB // 128), axis=-1)
    sin = jnp.concatenate([sin_ref[...] * c0 + cos_ref[...] * s0] * (HB * DKB // 128), axis=-1)
    qr = _rotate_pairs(q_ref[...], cos, sin)
    kr = _rotate_pairs(k_ref[...], cos, sin) * (DKB ** -0.5)
    idx_f = idx.astype(F32)
    outs = []
    for h in range(HB):
        lg = _ret_log_gamma(h)
        qh = qr[:, h * DKB:(h + 1) * DKB].astype(BF16)
        kh = kr[:, h * DKB:(h + 1) * DKB]
        vh = v_ref[:, h * DVB:(h + 1) * DVB].astype(BF16)
        s_old = s_ref[0, h]
        scores = _dot_nt(qh, kh.astype(BF16)) * dec_ref[h]
        o = _dot(scores.astype(BF16), vh)
        o = o + _dot(qh, s_old.astype(BF16)) * jnp.exp(lg * (idx_f + 1.0))
        kd = (kh * jnp.exp(lg * (c - 1.0 - idx_f))).astype(BF16)
        s_ref[0, h] = math.exp(lg * c) * s_old + _dot_tn(kd, vh)
        gate = g_ref[:, h * DVB:(h + 1) * DVB]
        outs.append(_rms(o) * (gate * _sigmoid(gate)))
    o_ref[...] = jnp.concatenate(outs, axis=-1)


def _ret_prompt(z, batch, seq, c):
    nc = seq // c
    wq = HB * DKB
    wv = HB * DVB
    return pl.pallas_call(
        functools.partial(_ret_prompt_kernel, c=c),
        grid=(batch, nc),
        in_specs=[
            pl.BlockSpec((c, wq), lambda b, i: (b * nc + i, 2)),
            pl.BlockSpec((c, wq), lambda b, i: (b * nc + i, 3)),
            pl.BlockSpec((c, wv), lambda b, i: (b * nc + i, 2)),
            pl.BlockSpec((c, wv), lambda b, i: (b * nc + i, 3)),
            pl.BlockSpec((1, 128), lambda b, i: (0, 0)),
        ],
        out_specs=[
            pl.BlockSpec((c, wv), lambda b, i: (b * nc + i, 0)),
            pl.BlockSpec((1, HB, DKB, DVB), lambda b, i: (b, 0, 0, 0)),
        ],
        out_shape=[
            jax.ShapeDtypeStruct((batch * seq, wv), F32),
            jax.ShapeDtypeStruct((batch, HB, DKB, DVB), F32),
        ],
        scratch_shapes=[pltpu.VMEM((HB, c, c), F32), pltpu.VMEM((c, 128), F32), pltpu.VMEM((c, 128), F32)],
        compiler_params=_cparams(("arbitrary", "arbitrary")),
        name="ret_prompt",
    )(z, z, z, z, _ret_theta(128))


def _ret_sample_kernel(qk_ref, v_ref, g_ref, th_ref, s_ref, o_ref, so_ref, *, pos):
    ang = float(pos) * th_ref[...]
    x = qk_ref[...]
    ncol = x.shape[1]
    d = lax.broadcasted_iota(jnp.int32, x.shape, 0)
    partner = jnp.where(d % 2 == 0, -pltpu.roll(x, DKB - 1, axis=0), pltpu.roll(x, 1, axis=0))
    qk = x * jnp.cos(ang) + partner * jnp.sin(ang)
    is_k = lax.broadcasted_iota(jnp.int32, (1, ncol), 1) % (2 * HB) >= HB
    qkb = (qk * jnp.where(is_k, DKB ** -0.5, 1.0)).astype(BF16)
    sel_row = lax.broadcasted_iota(jnp.int32, (ncol, HB * DVB), 0)
    sel_head = lax.broadcasted_iota(jnp.int32, (ncol, HB * DVB), 1) // DVB
    rows = []
    for s in range(v_ref.shape[0]):
        q_sel = jnp.where(sel_row == s * 2 * HB + sel_head, 1.0, 0.0)
        kv_sel = jnp.where(sel_row == s * 2 * HB + HB + sel_head, v_ref[s:s + 1, :], 0.0)
        res = _dot(qkb, jnp.concatenate([q_sel, kv_sel], axis=-1).astype(BF16))
        outs = []
        for h in range(HB):
            q_lanes = res[:, h * DVB:(h + 1) * DVB]
            outer = res[:, (HB + h) * DVB:(HB + h + 1) * DVB]
            s_new = math.exp(_ret_log_gamma(h)) * s_ref[s, h] + outer
            so_ref[s, h] = s_new
            o = jnp.sum(q_lanes * s_new, axis=0, keepdims=True)
            gate = g_ref[s:s + 1, h * DVB:(h + 1) * DVB]
            outs.append(_rms(o) * (gate * _sigmoid(gate)))
        rows.append(jnp.concatenate(outs, axis=-1))
    o_ref[...] = jnp.concatenate(rows, axis=0)


def _ret_sample(z, state, pos):
    nb = z.shape[0]
    wq = HB * DKB
    wv = HB * DVB
    sb = 8 if nb % 8 == 0 else nb
    qk_t = jnp.transpose(z[:, wv:wv + 2 * wq].reshape(nb, 2 * HB, DKB), (2, 0, 1)).reshape(DKB, nb * 2 * HB)
    theta_col = _ret_theta(DKB).reshape(DKB, 1)
    return pl.pallas_call(
        functools.partial(_ret_sample_kernel, pos=pos),
        grid=(nb // sb,),
        in_specs=[
            pl.BlockSpec((DKB, sb * 2 * HB), lambda b: (0, b)),
            pl.BlockSpec((sb, wv), lambda b: (b, 2)),
            pl.BlockSpec((sb, wv), lambda b: (b, 3)),
            pl.BlockSpec((DKB, 1), lambda b: (0, 0)),
            pl.BlockSpec((sb, HB, DKB, DVB), lambda b: (b, 0, 0, 0)),
        ],
        out_specs=[
            pl.BlockSpec((sb, wv), lambda b: (b, 0)),
            pl.BlockSpec((sb, HB, DKB, DVB), lambda b: (b, 0, 0, 0)),
        ],
        out_shape=[
            jax.ShapeDtypeStruct((nb, wv), F32),
            jax.ShapeDtypeStruct(state.shape, F32),
        ],
        compiler_params=_cparams(("arbitrary",)),
        name="ret_sample",
    )(qk_t, z, z, theta_col, state)


def _mm_resid_kernel(*refs, n_in):
    a_refs = refs[:n_in]
    w_refs = refs[n_in:2 * n_in]
    x_ref = refs[2 * n_in]
    o_ref = refs[2 * n_in + 1]
    acc = x_ref[...]
    for a_ref, w_ref in zip(a_refs, w_refs):
        acc = acc + _dot(a_ref[...].astype(BF16), w_ref[...])
    o_ref[...] = acc


def _mm_resid(a_list, w_bf16, resid, tm):
    m, n = resid.shape
    tn = n
    in_specs = []
    for a in a_list:
        in_specs.append(pl.BlockSpec((tm, a.shape[1]), lambda i, j: (i, 0)))
    row = 0
    for a in a_list:
        kb = a.shape[1]
        in_specs.append(pl.BlockSpec((kb, tn), lambda i, j, r=row // kb: (r, j)))
        row += kb
    in_specs.append(pl.BlockSpec((tm, tn), lambda i, j: (i, j)))
    return pl.pallas_call(
        functools.partial(_mm_resid_kernel, n_in=len(a_list)),
        grid=(m // tm, n // tn),
        in_specs=in_specs,
        out_specs=pl.BlockSpec((tm, tn), lambda i, j: (i, j)),
        out_shape=jax.ShapeDtypeStruct((m, n), F32),
        compiler_params=_cparams(("arbitrary", "arbitrary")),
        name="mm_resid",
    )(*a_list, *([w_bf16] * len(a_list)), resid)


def _ffn_kernel(*refs, final_norm, emit_bf16, aliased):
    x_ref, g_ref, wg_ref, wu_ref, wd_ref, fg_ref = refs[:6]
    refs = refs[6 + aliased:]
    if emit_bf16:
        o_ref, wgb_ref, wub_ref, wdb_ref, xn_ref, acc_ref = refs
    else:
        o_ref, xn_ref, acc_ref = refs
    j = pl.program_id(1)

    @pl.when(j == 0)
    def _():
        xn_ref[...] = (_rms(x_ref[...]) * g_ref[...]).astype(BF16)
        acc_ref[...] = jnp.zeros(acc_ref.shape, F32)

    wg, wu, wd = wg_ref[...].astype(BF16), wu_ref[...].astype(BF16), wd_ref[...].astype(BF16)
    if emit_bf16:
        wgb_ref[...], wub_ref[...], wdb_ref[...] = wg, wu, wd
    xn = xn_ref[...]
    gate = _dot(xn, wg)
    up = _dot(xn, wu)
    hid = (gate * _sigmoid(gate) * up).astype(BF16)
    acc_ref[...] += _dot(hid, wd)

    @pl.when(j == pl.num_programs(1) - 1)
    def _():
        y = x_ref[...] + acc_ref[...]
        if final_norm:
            y = _rms(y) * fg_ref[...]
        o_ref[...] = y


def _ffn_call(x, gain, wg, wu, wd, final_gain, final_norm, tm, tf, row_tiles, done=None):
    m = x.shape[0]
    dff = wg.shape[1]
    t0, t1 = row_tiles
    emit_bf16 = wg.dtype == F32
    w_specs = [pl.BlockSpec((D_MODEL, tf), lambda i, j: (0, j)),
               pl.BlockSpec((D_MODEL, tf), lambda i, j: (0, j)),
               pl.BlockSpec((tf, D_MODEL), lambda i, j: (j, 0))]
    in_specs = [pl.BlockSpec((tm, D_MODEL), lambda i, j: (i + t0, 0)),
                pl.BlockSpec((1, D_MODEL), lambda i, j: (0, 0)),
                *w_specs,
                pl.BlockSpec((1, D_MODEL), lambda i, j: (0, 0))]
    args = [x, gain.reshape(1, D_MODEL), wg, wu, wd, final_gain.reshape(1, D_MODEL)]
    out_specs = [pl.BlockSpec((tm, D_MODEL), lambda i, j: (i + t0, 0))]
    out_shape = [jax.ShapeDtypeStruct((m, D_MODEL), F32)]
    aliases = {}
    if done is not None:
        in_specs.append(pl.BlockSpec(memory_space=pl.ANY))
        args.append(done)
        aliases = {len(args) - 1: 0}
    if emit_bf16:
        out_specs += w_specs
        out_shape += [jax.ShapeDtypeStruct(w.shape, BF16) for w in (wg, wu, wd)]
    res = pl.pallas_call(
        functools.partial(_ffn_kernel, final_norm=final_norm, emit_bf16=emit_bf16, aliased=done is not None),
        grid=(t1 - t0, dff // tf),
        in_specs=in_specs,
        out_specs=out_specs,
        out_shape=out_shape,
        input_output_aliases=aliases,
        scratch_shapes=[pltpu.VMEM((tm, D_MODEL), BF16), pltpu.VMEM((tm, D_MODEL), F32)],
        compiler_params=_cparams(("arbitrary", "arbitrary")),
        name="ffn",
    )(*args)
    return res if emit_bf16 else res[0]


def _ffn(x, gain, weights, final_gain, final_norm, tm, tf=512):
    n_tiles = x.shape[0] // tm
    if weights[0].dtype != F32:
        return _ffn_call(x, gain, *weights, final_gain, final_norm, tm, tf, (0, n_tiles)), weights
    y, *weights_bf16 = _ffn_call(x, gain, *weights, final_gain, final_norm, tm, tf // 2, (0, 1))
    if n_tiles > 1:
        y = _ffn_call(x, gain, *weights_bf16, final_gain, final_norm, tm, tf, (1, n_tiles), done=y)
    return y, tuple(weights_bf16)


def _norm_and_shift(x_ref, g_ref, prev_ref, carry_ref, i, tiles_per_seq):
    h = _rms(x_ref[...]) * g_ref[...]
    if prev_ref is None:
        tm = h.shape[0]
        first = (i % tiles_per_seq) == 0
        prev_row = jnp.where(first, 0.0, carry_ref[0:1, :])
        rows = lax.broadcasted_iota(jnp.int32, (tm, 1), 0)
        shifted = jnp.where(rows == 0, prev_row, pltpu.roll(h, 1, axis=0))
        carry_ref[0:1, :] = h[tm - 1:tm, :]
    else:
        shifted = prev_ref[...]
    return h, shifted - h


def _rwkv_rkv_kernel(*refs, per_row_prev, tiles_per_seq):
    if per_row_prev:
        x_ref, g_ref, prev_ref, mu_ref, w_ref, o_ref, hl_ref, h_ref, xx_ref, carry_ref = refs
    else:
        x_ref, g_ref, mu_ref, w_ref, o_ref, hl_ref, h_ref, xx_ref, carry_ref = refs
        prev_ref = None
    i = pl.program_id(0)
    j = pl.program_id(1)

    @pl.when(j == 0)
    def _():
        h, xx = _norm_and_shift(x_ref, g_ref, prev_ref, carry_ref, i, tiles_per_seq)
        h_ref[...] = h
        xx_ref[...] = xx
        if per_row_prev:
            hl_ref[...] = h
        else:
            hl_ref[0] = h[h.shape[0] - 1:, :]

    xm = (h_ref[...] + xx_ref[...] * mu_ref[0]).astype(BF16)
    o_ref[...] = _dot(xm, w_ref[...])


def _rwkv_rkv(x, gain, prev, mu_rkv, w_rkv, tm, tiles_per_seq):
    m = x.shape[0]
    tn = D_MODEL
    per_seq = D_MODEL // tn
    per_row_prev = prev is not None
    in_specs = [pl.BlockSpec((tm, D_MODEL), lambda i, j: (i, 0)),
                pl.BlockSpec((1, D_MODEL), lambda i, j: (0, 0))]
    args = [x, gain.reshape(1, D_MODEL)]
    if per_row_prev:
        in_specs.append(pl.BlockSpec((tm, D_MODEL), lambda i, j: (i, 0)))
        args.append(prev)
        hl_spec = pl.BlockSpec((tm, D_MODEL), lambda i, j: (i, 0))
        hl_shape = jax.ShapeDtypeStruct((m, D_MODEL), F32)
    else:
        nseq = m // (tm * tiles_per_seq)
        hl_spec = pl.BlockSpec((1, 1, D_MODEL), lambda i, j: (i // tiles_per_seq, 0, 0))
        hl_shape = jax.ShapeDtypeStruct((nseq, 1, D_MODEL), F32)
    in_specs += [pl.BlockSpec((1, 1, D_MODEL), lambda i, j: (j // per_seq, 0, 0)),
                 pl.BlockSpec((None, D_MODEL, tn), lambda i, j: (j // per_seq, 0, j % per_seq))]
    args += [mu_rkv, w_rkv]
    return pl.pallas_call(
        functools.partial(_rwkv_rkv_kernel, per_row_prev=per_row_prev, tiles_per_seq=tiles_per_seq),
        grid=(m // tm, 3 * per_seq),
        in_specs=in_specs,
        out_specs=[pl.BlockSpec((tm, tn), lambda i, j: (i, j)), hl_spec],
        out_shape=[jax.ShapeDtypeStruct((m, 3 * D_MODEL), F32), hl_shape],
        scratch_shapes=[pltpu.VMEM((tm, D_MODEL), F32), pltpu.VMEM((tm, D_MODEL), F32),
                        pltpu.VMEM((8, D_MODEL), F32)],
        compiler_params=_cparams(("arbitrary", "arbitrary")),
        name="rwkv_rkv",
    )(*args)


def _rwkv_lora_kernel(*refs, per_row_prev, tiles_per_seq):
    if per_row_prev:
        (x_ref, g_ref, prev_ref, mu_ref, w0_ref, w1_ref, w2_ref, a0_ref, a1_ref, a2_ref, g1_ref, g2_ref,
         lw_ref, a_ref, go_ref, carry_ref) = refs
    else:
        (x_ref, g_ref, mu_ref, w0_ref, w1_ref, w2_ref, a0_ref, a1_ref, a2_ref, g1_ref, g2_ref,
         lw_ref, a_ref, go_ref, carry_ref) = refs
        prev_ref = None
    h, xx = _norm_and_shift(x_ref, g_ref, prev_ref, carry_ref, pl.program_id(0), tiles_per_seq)
    xw = (h + xx * mu_ref[0]).astype(BF16)
    xa = (h + xx * mu_ref[1]).astype(BF16)
    xg = (h + xx * mu_ref[2]).astype(BF16)
    zw = w0_ref[...] + _dot(jnp.tanh(_dot(xw, w1_ref[...])).astype(BF16), w2_ref[...])
    lw_ref[...] = -math.exp(-0.5) * _sigmoid(zw)
    a_ref[...] = _sigmoid(a0_ref[...] + _dot(_dot(xa, a1_ref[...]).astype(BF16), a2_ref[...]))
    go_ref[...] = _dot(_sigmoid(_dot(xg, g1_ref[...])).astype(BF16), g2_ref[...])


def _rwkv_lora(x, gain, prev, mu_wag, w0, w1, w2, a0, a1, a2, g1, g2, tm, tiles_per_seq):
    m = x.shape[0]
    per_row_prev = prev is not None
    full = lambda a: pl.BlockSpec(a.shape, lambda i: (0,) * a.ndim)
    row = pl.BlockSpec((tm, D_MODEL), lambda i: (i, 0))
    args = [x, gain.reshape(1, D_MODEL)]
    in_specs = [row, full(args[1])]
    if per_row_prev:
        in_specs.append(row)
        args.append(prev)
    rest = [mu_wag, w0.reshape(1, D_MODEL), w1, w2, a0.reshape(1, D_MODEL), a1, a2, g1, g2]
    in_specs += [full(a) for a in rest]
    args += rest
    return pl.pallas_call(
        functools.partial(_rwkv_lora_kernel, per_row_prev=per_row_prev, tiles_per_seq=tiles_per_seq),
        grid=(m // tm,),
        in_specs=in_specs,
        out_specs=[row, row, row],
        out_shape=[jax.ShapeDtypeStruct((m, D_MODEL), F32)] * 3,
        scratch_shapes=[pltpu.VMEM((8, D_MODEL), F32)],
        compiler_params=_cparams(("arbitrary",)),
        name="rwkv_lora",
    )(*args)


def _wkv_keys(k, a, kk_gain, ka_gain, head_sum):
    kk = k * kk_gain
    kk = kk / jnp.maximum(jnp.sqrt(head_sum(kk * kk)), 1e-12)
    return kk, kk * a, k * (1.0 + (a - 1.0) * ka_gain)


def _head_norm(o, head_sum):
    mean = head_sum(o) * (1.0 / NC)
    var = head_sum(jnp.square(o - mean)) * (1.0 / NC)
    return (o - mean) * lax.rsqrt(var + GN_EPS)


def _wkv_finish(on, r, k2, v, g, rk, ln_g, ln_b, head_sum):
    bonus = head_sum(r * k2 * rk) * v
    return (on * ln_g + ln_b + bonus) * g


def _pair_sum(x):
    lane = lax.broadcasted_iota(jnp.int32, x.shape, 1)
    s0 = jnp.sum(x[:, :NC], axis=-1, keepdims=True)
    s1 = jnp.sum(x[:, NC:], axis=-1, keepdims=True)
    return jnp.where(lane < NC, s0, s1)


def _bmm(a, b):
    return jnp.einsum("cij,cjk->cik", a, b, preferred_element_type=F32)


def _bmm_nt(a, b):
    return jnp.einsum("cik,cjk->cij", a, b, preferred_element_type=F32)


def _wkv_prompt_kernel(r_ref, k_ref, v_ref, lw_ref, a_ref, kkg_ref, kag_ref, rk_ref, g_ref, lng_ref, lnb_ref,
                       y_ref, so_ref, st_ref, gh_ref, ghb_ref, e0t_ref, ft_ref, wend_ref, bonus_ref, *, tc, nt):
    c = WKV_CHUNK
    nch = tc // c
    s = pl.program_id(0)

    @pl.when(s == 0)
    def _():
        for ref in (st_ref, gh_ref, ghb_ref, e0t_ref, ft_ref, wend_ref, bonus_ref):
            ref[...] = jnp.zeros(ref.shape, ref.dtype)

    prev = jnp.maximum(s - 1, 0)
    slot = prev % 2
    first = prev % nt == 0
    ss = [jnp.where(first, 0.0, st_ref[0]), jnp.where(first, 0.0, st_ref[1])]
    o_chunks = [[], []]
    lane_sum = lambda t: jnp.sum(t, axis=-1, keepdims=True)
    n_points = 2 + int(math.log2(c))
    done = [0]

    def advance_stage2(point):
        upto = nch * (point + 1) // n_points
        for ch in range(done[0], upto):
            w_end = wend_ref[slot, ch]
            for hh in range(2):
                n = hh * nch + ch
                other = slice((1 - hh) * NC, (2 - hh) * NC)
                sb = ss[hh].astype(BF16)
                o = _dot_nt(ghb_ref[slot, n], sb) + gh_ref[slot, n][:, other]
                o_chunks[hh].append(_head_norm(o, lane_sum))
                ss[hh] = ss[hh] * w_end + _dot(sb, e0t_ref[slot, n]) + ft_ref[slot, n]
        done[0] = upto

    ri = lax.broadcasted_iota(jnp.int32, (c, c), 0)
    cj = lax.broadcasted_iota(jnp.int32, (c, c), 1)
    tri = (ri >= cj).astype(BF16)
    gi = lax.broadcasted_iota(jnp.int32, (2 * c, 2 * c), 0)
    gj = lax.broadcasted_iota(jnp.int32, (2 * c, 2 * c), 1) % c
    gmask = (gj < jnp.where(gi < c, gi, gi - c + 1))[None]

    r = r_ref[...]
    v = v_ref[...]
    kk, bv, k2 = _wkv_keys(k_ref[...], a_ref[...], kkg_ref[...], kag_ref[...], _pair_sum)
    to3 = lambda x: x.reshape(nch, c, 2 * NC)
    lw3 = to3(lw_ref[...])
    p0 = lw3.astype(BF16)
    r1 = lw3 - p0.astype(F32)
    p1 = r1.astype(BF16)
    p2 = (r1 - p1.astype(F32)).astype(BF16)
    trib = jnp.broadcast_to(tri[None], (nch, c, c))
    cum = _bmm(trib, p0) + (_bmm(trib, p1) + _bmm(trib, p2))
    cum_last = cum[:, c - 1:c, :]
    w_rest = jnp.exp(cum_last - cum)
    w_inv = jnp.exp(-cum)
    am = -to3(kk) * jnp.exp(cum - lw3)
    rm = to3(r) * jnp.exp(cum)
    bm = (to3(bv) * w_inv).astype(BF16)
    km = (to3(k2) * w_inv).astype(BF16)
    bt = (to3(bv) * w_rest).astype(BF16)
    kt = (to3(k2) * w_rest).astype(BF16)
    v3 = to3(v)
    advance_stage2(0)

    both = lambda x: jnp.concatenate([x, x], axis=0)
    lane = lax.broadcasted_iota(jnp.int32, (2 * nch, 1, 2 * NC), 2)
    head = lax.broadcasted_iota(jnp.int32, (2 * nch, 1, 2 * NC), 0) // nch
    own = (lane // NC) == head
    v_sw = both(pltpu.roll(v, NC, axis=1).reshape(nch, c, 2 * NC))
    x1 = both(jnp.concatenate([am, rm], axis=1).astype(BF16))
    x2 = jnp.where(own, both(jnp.concatenate([bm, km], axis=1)), 0.0)
    g4 = jnp.where(gmask, _bmm_nt(x1, x2), 0.0)
    lv = _bmm(g4[:, :, c:].astype(BF16), v_sw.astype(BF16))
    advance_stage2(1)
    x = jnp.where(own, both(am), lv[:, :c])
    lp = g4[:, :c, :c].astype(BF16)
    steps = int(math.log2(c))
    for i in range(steps - 1):
        prod = _bmm(lp, jnp.concatenate([x.astype(BF16), lp], axis=-1))
        x = x + prod[:, :, :2 * NC]
        lp = prod[:, :, 2 * NC:].astype(BF16)
        advance_stage2(2 + i)
    x = x + _bmm(lp, x.astype(BF16))
    advance_stage2(n_points - 1)
    gh_new = _bmm(g4[:, c:, :c].astype(BF16), x.astype(BF16)) + jnp.where(own, both(rm), lv[:, c:])
    btk = jnp.where(own, both(jnp.concatenate([bt, kt], axis=1)), 0.0)
    pqv = jnp.concatenate([x, jnp.where(own, 0.0, v_sw)], axis=1).astype(BF16)
    st_ref[0] = ss[0]
    st_ref[1] = ss[1]
    so_ref[0, 0] = ss[0][:, :NC]
    so_ref[0, 1] = ss[1][:, NC:]
    on = jnp.concatenate([jnp.concatenate(oc, axis=0) for oc in o_chunks], axis=-1)
    y_ref[...] = (on * lng_ref[...] + lnb_ref[...] + bonus_ref[slot]) * g_ref[...]

    nxt = s % 2
    gh_ref[nxt] = gh_new
    ghb_ref[nxt] = gh_new.astype(BF16)
    for n in range(2 * nch):
        eft = _dot_tn(pqv[n], btk[n])
        e0t_ref[nxt, n] = eft.astype(BF16)
        ft_ref[nxt, n] = eft[NC:] if n < nch else eft[:NC]
    wend_ref[nxt] = jnp.exp(cum_last)
    bonus_ref[nxt] = _pair_sum(r * k2 * rk_ref[...]) * v


def _wkv_prompt(rkv, lw, a, g, kk_gain, ka_gain, rk, ln_g, ln_b, batch, seq, tc):
    nt = seq // tc
    nhp = HC // 2
    nblk = batch * nhp * nt
    nch = tc // WKV_CHUNK

    def rows_cols(blk):
        b = blk // (nhp * nt)
        return b * nt + blk % nt, (blk // nt) % nhp, b

    cur = lambda s: jnp.minimum(s, nblk - 1)
    prv = lambda s: jnp.maximum(s - 1, 0)
    tok = lambda off, which: pl.BlockSpec(
        (tc, 2 * NC), lambda s: (rows_cols(which(s))[0], off + rows_cols(which(s))[1]))
    par = lambda which: pl.BlockSpec((1, 2 * NC), lambda s: (0, rows_cols(which(s))[1]))
    return pl.pallas_call(
        functools.partial(_wkv_prompt_kernel, tc=tc, nt=nt),
        grid=(nblk + 1,),
        in_specs=[tok(0, cur), tok(nhp, cur), tok(2 * nhp, cur), tok(0, cur), tok(0, cur),
                  par(cur), par(cur), par(cur), tok(0, prv), par(prv), par(prv)],
        out_specs=[
            tok(0, prv),
            pl.BlockSpec((1, 2, NC, NC), lambda s: (rows_cols(prv(s))[2], rows_cols(prv(s))[1], 0, 0)),
        ],
        out_shape=[
            jax.ShapeDtypeStruct((batch * seq, D_MODEL), F32),
            jax.ShapeDtypeStruct((batch, HC, NC, NC), F32),
        ],
        scratch_shapes=[
            pltpu.VMEM((2, NC, 2 * NC), F32),
            pltpu.VMEM((2, 2 * nch, WKV_CHUNK, 2 * NC), F32),
            pltpu.VMEM((2, 2 * nch, WKV_CHUNK, 2 * NC), BF16),
            pltpu.VMEM((2, 2 * nch, 2 * NC, 2 * NC), BF16),
            pltpu.VMEM((2, 2 * nch, NC, 2 * NC), F32),
            pltpu.VMEM((2, nch, 1, 2 * NC), F32),
            pltpu.VMEM((2, tc, 2 * NC), F32),
        ],
        compiler_params=_cparams(("arbitrary",)),
        name="wkv_prompt",
    )(rkv, rkv, rkv, lw, a, kk_gain, ka_gain, rk, g, ln_g, ln_b)


def _wkv_sample_kernel(r_ref, k_ref, v_ref, lw_ref, a_ref, g_ref, kkg_ref, kag_ref, rk_ref, lng_ref, lnb_ref,
                       s_ref, y_ref, so_ref, o_ref):
    ys = []
    col_sum = lambda x: jnp.sum(x, axis=0, keepdims=True)
    for hh in range(2):
        sl = slice(hh * NC, (hh + 1) * NC)
        t = lambda ref: ref[...].T[sl, :]
        r, k, v, a, g = t(r_ref), t(k_ref), t(v_ref), t(a_ref), t(g_ref)
        w = jnp.exp(t(lw_ref))
        kk, bv, k2 = _wkv_keys(k, a, kkg_ref[sl, :], kag_ref[sl, :], col_sum)
        for i in range(NC):
            s_old = s_ref[hh, i]
            sk = jnp.sum(s_old * kk, axis=0, keepdims=True)
            s_new = s_old * w - sk * bv + v[i:i + 1, :] * k2
            so_ref[hh, i] = s_new
            o_ref[i:i + 1, :] = jnp.sum(s_new * r, axis=0, keepdims=True)
        ys.append(_wkv_finish(_head_norm(o_ref[...], col_sum), r, k2, v, g,rk_ref[sl, :], lng_ref[sl, :], lnb_ref[sl, :], col_sum))
    y_ref[...] = jnp.concatenate(ys, axis=0).T


def _wkv_sample(rkv, lw, a, g, kk_gain, ka_gain, rk, ln_g, ln_b, state_t):
    nb = lw.shape[0]
    nhp = HC // 2
    tok = lambda off: pl.BlockSpec((nb, 2 * NC), lambda hp: (0, off + hp))
    par = pl.BlockSpec((2 * NC, 1), lambda hp: (hp, 0))
    col = lambda p: p.reshape(D_MODEL, 1)
    st_spec = pl.BlockSpec((2, NC, NC, nb), lambda hp: (hp, 0, 0, 0))
    return pl.pallas_call(
        _wkv_sample_kernel,
        grid=(nhp,),
        in_specs=[tok(0), tok(nhp), tok(2 * nhp), tok(0), tok(0), tok(0), par, par, par, par, par, st_spec],
        out_specs=[pl.BlockSpec((nb, 2 * NC), lambda hp: (0, hp)), st_spec],
        out_shape=[
            jax.ShapeDtypeStruct((nb, D_MODEL), F32),
            jax.ShapeDtypeStruct(state_t.shape, F32),
        ],
        scratch_shapes=[pltpu.VMEM((NC, nb), F32)],
        compiler_params=_cparams(("arbitrary",)),
        name="wkv_sample",
    )(rkv, rkv, rkv, lw, a, g, col(kk_gain), col(ka_gain), col(rk), col(ln_g), col(ln_b), state_t)


def _row_tile(m, want):
    return min(m, want)


def _trunk(x, batch, seq, past, W):
    m = x.shape[0]
    prompt = past is None
    tm = _row_tile(m, 512)
    lams = (W["lambda_q1"], W["lambda_k1"], W["lambda_q2"], W["lambda_k2"])

    ka, va, z = _inproj0(x, W["norm0_mix"], W["w_in0"], tm)
    if prompt:
        o_a = _dattn_prompt(z, ka, va, lams, W["subln_gain"], batch, seq, min(seq, 512))
        o_b, ret_new = _ret_prompt(z, batch, seq, min(seq, 256))
    else:
        o_a = _dattn_sample(z, ka, va, past["cache_k"], past["cache_v"], past["page_table"], lams,
                            W["subln_gain"])
        o_b, ret_new = _ret_sample(z, past["state_ret"], past["pos"])
    x = _mm_resid([o_a, o_b], W["w_out0"], x, _row_tile(m, 512))
    x, W["ffn0"] = _ffn(x, W["norm0_ffn"], W["ffn0"], W["norm_final"], False, tm)

    tiles_per_seq = max(seq // tm, 1)
    prev = None if prompt else past["state_shift"]
    rkv, hlast = _rwkv_rkv(x, W["norm1_mix"], prev, W["mu_rkv"], W["w_rkv"], tm, tiles_per_seq)
    tl = _row_tile(m, 256)
    lw, a, g = _rwkv_lora(x, W["norm1_mix"], prev, W["mu_wag"], W["decay_w0"], W["decay_w1"], W["decay_w2"],
                          W["aaa_a0"], W["aaa_a1"], W["aaa_a2"], W["gate_g1"], W["gate_g2"],
                          tl, max(seq // tl, 1))
    head_params = (W["k_k"], W["k_a"], W["r_k"], W["lnx_gain"], W["lnx_bias"])
    if prompt:
        y, wkv_new = _wkv_prompt(rkv, lw, a, g, *head_params, batch, seq, min(seq, 512))
        shift_new = hlast.reshape(batch, D_MODEL)
    else:
        state_t = jnp.transpose(past["state_wkv"], (1, 2, 3, 0))
        y, wkv_t = _wkv_sample(rkv, lw, a, g, *head_params, state_t)
        wkv_new = jnp.transpose(wkv_t, (3, 0, 1, 2))
        shift_new = hlast
    x = _mm_resid([y], W["w_out1"], x, _row_tile(m, 512))
    x, W["ffn1"] = _ffn(x, W["norm1_ffn"], W["ffn1"], W["norm_final"], True, tm)
    return x, ka, va, ret_new, wkv_new, shift_new


def kernel(x_prompt, x_sample, cache_k, cache_v, page_table, state_ret, state_wkv, state_shift, norm0_mix, w_in0, lambda_q1, lambda_k1, lambda_q2, lambda_k2, subln_gain, w_out0, norm0_ffn, w_gate0, w_up0, w_down0, norm1_mix, mu1, w_r1, w_k1, w_v1, decay_w0, decay_w1, decay_w2, aaa_a0, aaa_a1, aaa_a2, gate_g1, gate_g2, k_k, k_a, r_k, lnx_gain, lnx_bias, w_out1, norm1_ffn, w_gate1, w_up1, w_down1, norm_final):
    bf = lambda w: w.astype(BF16)
    row = lambda p: p.reshape(1, D_MODEL)
    W = dict(
        norm0_mix=norm0_mix, w_in0=bf(w_in0), lambda_q1=lambda_q1, lambda_k1=lambda_k1, lambda_q2=lambda_q2,
        lambda_k2=lambda_k2, subln_gain=subln_gain, w_out0=bf(w_out0), norm0_ffn=norm0_ffn,
        ffn0=(w_gate0, w_up0, w_down0),
        norm1_mix=norm1_mix,
        mu_rkv=jnp.stack([mu1[0], mu1[2], mu1[3]]).reshape(3, 1, D_MODEL),
        mu_wag=jnp.stack([mu1[1], mu1[4], mu1[5]]).reshape(3, 1, D_MODEL),
        w_rkv=jnp.stack([bf(w_r1), bf(w_k1), bf(w_v1)]),
        decay_w0=decay_w0, decay_w1=bf(decay_w1), decay_w2=bf(decay_w2),
        aaa_a0=aaa_a0, aaa_a1=bf(aaa_a1), aaa_a2=bf(aaa_a2), gate_g1=bf(gate_g1), gate_g2=bf(gate_g2),
        k_k=row(k_k), k_a=row(k_a), r_k=row(r_k), lnx_gain=row(lnx_gain), lnx_bias=row(lnx_bias),
        w_out1=bf(w_out1), norm1_ffn=norm1_ffn, ffn1=(w_gate1, w_up1, w_down1),
        norm_final=norm_final,
    )
    bsz, seq = x_prompt.shape[:2]
    y_p, k_p, v_p, ret_p, wkv_p, shift_p = _trunk(x_prompt.reshape(bsz * seq, D_MODEL), bsz, seq, None, W)
    nb, dseq = x_sample.shape[:2]
    past_len = page_table.shape[1] * cache_k.shape[1]
    past = dict(cache_k=cache_k, cache_v=cache_v, page_table=page_table, state_ret=state_ret,
                state_wkv=state_wkv, state_shift=state_shift, pos=past_len)
    y_s, k_s, v_s, ret_s, wkv_s, shift_s = _trunk(x_sample.reshape(nb * dseq, D_MODEL), nb, dseq, past, W)
    return (y_p.reshape(bsz, seq, D_MODEL), y_s.reshape(nb, dseq, D_MODEL),
            k_p.reshape(bsz, seq, HA, 2 * DKA), v_p.reshape(bsz, seq, HA, DVA),
            k_s.reshape(nb, dseq, HA, 2 * DKA), v_s.reshape(nb, dseq, HA, DVA),
            ret_p, ret_s, wkv_p, wkv_s, shift_p, shift_s)
```

```python
import functools
import math

import jax
import jax.numpy as jnp
from jax import lax
from jax.experimental import pallas as pl
from jax.experimental.pallas import tpu as pltpu

F32 = jnp.float32
BF16 = jnp.bfloat16

D_MODEL = 2048
EPS = 1e-6
DKA = 64
DVA = 128
HA = 8
LAMBDA_INIT = 0.8 - 0.6 * math.exp(-0.3 * 0)
DKB = 64
DVB = 128
HB = 8
ROPE_BASE = 10000.0
NC = 64
HC = D_MODEL // NC
GN_EPS = 64e-5
WKV_CHUNK = 64
PAGES_PER_STEP = 16
VMEM_LIMIT = 48 * 1024 * 1024
INPROJ_VMEM_LIMIT = 56 * 1024 * 1024


def _cparams(sem):
    return pltpu.CompilerParams(dimension_semantics=sem, vmem_limit_bytes=VMEM_LIMIT)


def _rms(x, eps=EPS):
    return x * lax.rsqrt(jnp.mean(x * x, axis=-1, keepdims=True) + eps)


def _sigmoid(x):
    return 1.0 / (1.0 + jnp.exp(-x))


def _dot(a, b):
    return jnp.dot(a, b, preferred_element_type=F32)


def _dot_nt(a, b):
    return lax.dot_general(a, b, (((1,), (1,)), ((), ())), preferred_element_type=F32)


def _dot_tn(a, b):
    return lax.dot_general(a, b, (((0,), (0,)), ((), ())), preferred_element_type=F32)


def _split2(x):
    hi = x.astype(BF16)
    lo = (x - hi.astype(F32)).astype(BF16)
    return hi, lo


def _dot_f32(a, b, dot=_dot):
    ah, al = _split2(a)
    bh, bl = _split2(b)
    return dot(ah, bh) + (dot(ah, bl) + dot(al, bh))


def _inproj0_kernel(x_ref, g_ref, w_ref, ka_ref, va_ref, z_ref, xn_ref):
    j = pl.program_id(1)

    @pl.when(j == 0)
    def _():
        xn_ref[...] = (_rms(x_ref[...]) * g_ref[...]).astype(BF16)

    acc = _dot(xn_ref[...], w_ref[...])
    half = acc.shape[1] // 2

    @pl.when(j == 0)
    def _():
        z_ref[:, :half] = acc[:, :half]
        ka_ref[...] = acc[:, half:]

    @pl.when(j == 1)
    def _():
        va_ref[...] = acc[:, :half]
        z_ref[:, half:] = acc[:, half:]

    @pl.when(j == 2)
    def _():
        z_ref[...] = acc


def _inproj0(x, gain, w_bf16, tm):
    m = x.shape[0]
    tn = 2048
    nj = w_bf16.shape[1] // tn
    return pl.pallas_call(
        _inproj0_kernel,
        grid=(m // tm, nj),
        in_specs=[
            pl.BlockSpec((tm, D_MODEL), lambda i, j: (i, 0)),
            pl.BlockSpec((1, D_MODEL), lambda i, j: (0, 0)),
            pl.BlockSpec((D_MODEL, tn), lambda i, j: (0, j)),
        ],
        out_specs=[
            pl.BlockSpec((tm, tn // 2), lambda i, j: (i, 0)),
            pl.BlockSpec((tm, tn // 2), lambda i, j: (i, 0)),
            pl.BlockSpec((tm, tn), lambda i, j: (i, j // 2)),
        ],
        out_shape=[
            jax.ShapeDtypeStruct((m, tn // 2), F32),
            jax.ShapeDtypeStruct((m, tn // 2), F32),
            jax.ShapeDtypeStruct((m, 2 * tn), F32),
        ],
        scratch_shapes=[pltpu.VMEM((tm, D_MODEL), BF16)],
        compiler_params=pltpu.CompilerParams(dimension_semantics=("arbitrary", "arbitrary"),
                                             vmem_limit_bytes=INPROJ_VMEM_LIMIT),
        name="inproj0",
    )(x, gain.reshape(1, D_MODEL), w_bf16)


def _lambda_full(lq1, lk1, lq2, lk2):
    s1 = jnp.sum(lq1 * lk1, axis=-1, keepdims=True)
    s2 = jnp.sum(lq2 * lk2, axis=-1, keepdims=True)
    return jnp.exp(s1) - jnp.exp(s2) + LAMBDA_INIT


def _alibi_slope(h_vec):
    slope = jnp.zeros(h_vec.shape, F32)
    for h in range(HA):
        slope = jnp.where(h_vec == h, 2.0 ** (-(h + 1)), slope)
    return slope


def _subln(o, gain):
    return _rms(o) * gain * (1.0 - LAMBDA_INIT)


def _dattn_prompt_kernel(ii_ref, jj_ref, q_ref, k_ref, v_ref, lq1, lk1, lq2, lk2, gain_ref, o_ref,
                         m_ref, l_ref, a_ref, *, t, hps):
    hg = pl.program_id(1)
    s = pl.program_id(2)
    i = ii_ref[s]
    j = jj_ref[s]

    @pl.when(j == 0)
    def _():
        m_ref[...] = jnp.full(m_ref.shape, -jnp.inf, F32)
        l_ref[...] = jnp.zeros(l_ref.shape, F32)
        a_ref[...] = jnp.zeros(a_ref.shape, F32)

    def step(diagonal):
        col = lax.broadcasted_iota(jnp.int32, (1, t), 1)
        rel = ((j - i) * t + col - (t - 1)).astype(F32)
        if diagonal:
            keep = (lax.broadcasted_iota(jnp.int32, (t, t), 1) <= lax.broadcasted_iota(jnp.int32, (t, t), 0))
        for hh in range(hps):
            hl = slice(hh * DVA, (hh + 1) * DVA)
            bias = _alibi_slope(jnp.full((1, 1), hg * hps + hh, jnp.int32)) * rel
            q = (q_ref[:, hl] * (DKA ** -0.5)).astype(BF16)
            k = k_ref[:, hl].astype(BF16)
            v = v_ref[:, hl].astype(BF16)
            for half in range(2):
                idx = 2 * hh + half
                sl = slice(half * DKA, (half + 1) * DKA)
                sc = _dot_nt(q[:, sl], k[:, sl]) + bias
                if diagonal:
                    sc = jnp.where(keep, sc, -1e30)
                m_prev = m_ref[idx]
                m_new = jnp.maximum(m_prev, jnp.max(sc, axis=-1, keepdims=True))
                alpha = jnp.exp(m_prev - m_new)
                p = jnp.exp(sc - jnp.concatenate([m_new] * (t // DVA), axis=-1))
                l_ref[idx] = alpha * l_ref[idx] + jnp.sum(p, axis=-1, keepdims=True)
                a_ref[idx] = alpha * a_ref[idx] + _dot(p.astype(BF16), v)
                m_ref[idx] = m_new

    @pl.when(j < i)
    def _():
        step(False)

    @pl.when(j == i)
    def _():
        step(True)
        lam = _lambda_full(lq1[...], lk1[...], lq2[...], lk2[...])
        for hh in range(hps):
            o = a_ref[2 * hh] / l_ref[2 * hh] - lam * (a_ref[2 * hh + 1] / l_ref[2 * hh + 1])
            o_ref[:, hh * DVA:(hh + 1) * DVA] = _subln(o, gain_ref[...])


def _dattn_prompt(z, ka, va, lams, subln_gain, batch, seq, t):
    nt = seq // t
    hps = HA
    pairs = [(i, j) for i in range(nt) for j in range(i + 1)]
    ii = jnp.asarray([p[0] for p in pairs], jnp.int32)
    jj = jnp.asarray([p[1] for p in pairs], jnp.int32)
    vec = pl.BlockSpec((1, DKA), lambda b, h, s, ii, jj: (0, 0))
    grid_spec = pltpu.PrefetchScalarGridSpec(
        num_scalar_prefetch=2,
        grid=(batch, HA // hps, len(pairs)),
        in_specs=[
            pl.BlockSpec((t, hps * DVA), lambda b, h, s, ii, jj: (b * nt + ii[s], h)),
            pl.BlockSpec((t, hps * DVA), lambda b, h, s, ii, jj: (b * nt + jj[s], h)),
            pl.BlockSpec((t, hps * DVA), lambda b, h, s, ii, jj: (b * nt + jj[s], h)),
            vec, vec, vec, vec,
            pl.BlockSpec((1, DVA), lambda b, h, s, ii, jj: (0, 0)),
        ],
        out_specs=pl.BlockSpec((t, hps * DVA), lambda b, h, s, ii, jj: (b * nt + ii[s], h)),
        scratch_shapes=[pltpu.VMEM((2 * hps, t, DVA), F32)] * 3,
    )
    return pl.pallas_call(
        functools.partial(_dattn_prompt_kernel, t=t, hps=hps),
        grid_spec=grid_spec,
        out_shape=jax.ShapeDtypeStruct((batch * seq, HA * DVA), F32),
        compiler_params=_cparams(("arbitrary", "arbitrary", "arbitrary")),
        name="dattn_prompt",
    )(ii, jj, z, ka, va, *[l.reshape(1, DKA) for l in lams], subln_gain.reshape(1, DVA))


def _rows_by_head(row, nrow):
    head = lax.broadcasted_iota(jnp.int32, (nrow, 1), 0) * HA // nrow
    out = jnp.zeros((nrow, DVA), F32)
    for h in range(HA):
        out = jnp.where(head == h, row[:, h * DVA:(h + 1) * DVA], out)
    return out


def _dattn_sample_kernel(pt_ref, q_ref, kn_ref, vn_ref, *rest, n_pages, page, pps):
    kp_refs = rest[:pps]
    vp_refs = rest[pps:2 * pps]
    lq1, lk1, lq2, lk2, gain_ref, o_ref, qm_ref, m_ref, l_ref, a_ref = rest[2 * pps:]
    p = pl.program_id(1)
    nrow = 2 * HA
    rid = lax.broadcasted_iota(jnp.int32, (nrow, 1), 0)
    slope = _alibi_slope(rid // 2)
    past = n_pages * page

    @pl.when(p == 0)
    def _():
        half = lax.broadcasted_iota(jnp.int32, (nrow, DVA), 1) // DKA
        qm_ref[...] = jnp.where(half == rid % 2, _rows_by_head(q_ref[0], nrow), 0.0)
        m_ref[...] = jnp.full(m_ref.shape, -jnp.inf, F32)
        l_ref[...] = jnp.zeros(l_ref.shape, F32)
        a_ref[...] = jnp.zeros(a_ref.shape, F32)

    qm = qm_ref[...]
    qmb = qm.astype(BF16)
    n = lax.broadcasted_iota(jnp.int32, (1, page * HA), 1)
    own_head = n % HA == rid // 2
    scores = []
    for r in range(pps):
        rel = (past - ((p * pps + r) * page + n // HA)).astype(F32)
        kf = kp_refs[r][...].reshape(page * HA, DVA).astype(BF16)
        sc = _dot_nt(qmb, kf) * (DKA ** -0.5) - slope * rel
        scores.append(jnp.where(own_head, sc, -1e30))
    m_new = m_ref[...]
    for sc in scores:
        m_new = jnp.maximum(m_new, jnp.max(sc, axis=-1, keepdims=True))
    alpha = jnp.exp(m_ref[...] - m_new)
    l_new = alpha * l_ref[...]
    a_new = alpha * a_ref[...]
    for r, sc in enumerate(scores):
        pr = jnp.exp(sc - m_new)
        l_new = l_new + jnp.sum(pr, axis=-1, keepdims=True)
        a_new = a_new + _dot(pr.astype(BF16), vp_refs[r][...].reshape(page * HA, DVA).astype(BF16))
    l_ref[...] = l_new
    a_ref[...] = a_new
    m_ref[...] = m_new

    @pl.when(p == n_pages // pps - 1)
    def _():
        sc_n = jnp.sum(qm * _rows_by_head(kn_ref[0], nrow), axis=-1, keepdims=True) * (DKA ** -0.5)
        m_f = jnp.maximum(m_ref[...], sc_n)
        al = jnp.exp(m_ref[...] - m_f)
        pn = jnp.exp(sc_n - m_f)
        l_f = al * l_ref[...] + pn
        a_f = (al * a_ref[...] + pn * _rows_by_head(vn_ref[0], nrow)) / l_f
        lam = _lambda_full(lq1[...], lk1[...], lq2[...], lk2[...])
        outs = []
        for h in range(HA):
            o = a_f[2 * h:2 * h + 1] - lam * a_f[2 * h + 1:2 * h + 2]
            outs.append(_subln(o, gain_ref[...]))
        o_ref[0] = jnp.concatenate(outs, axis=-1)


def _dattn_sample(z, ka, va, cache_k, cache_v, page_table, lams, subln_gain):
    nb, n_pages = page_table.shape
    page = cache_k.shape[1]
    w = HA * DVA
    row3 = lambda a: a.reshape(nb, 1, a.shape[-1])
    vec = pl.BlockSpec((1, DKA), lambda b, p, pt: (0, 0))
    pps = max(d for d in range(1, PAGES_PER_STEP + 1) if n_pages % d == 0)
    page_specs = [
        pl.BlockSpec((None, page, HA, DVA), lambda b, p, pt, r=r: (pt[b * n_pages + p * pps + r], 0, 0, 0))
        for r in range(pps)]
    grid_spec = pltpu.PrefetchScalarGridSpec(
        num_scalar_prefetch=1,
        grid=(nb, n_pages // pps),
        in_specs=[
            pl.BlockSpec((1, 1, w), lambda b, p, pt: (b, 0, 0)),
            pl.BlockSpec((1, 1, w), lambda b, p, pt: (b, 0, 0)),
            pl.BlockSpec((1, 1, w), lambda b, p, pt: (b, 0, 0)),
            *page_specs, *page_specs,
            vec, vec, vec, vec,
            pl.BlockSpec((1, DVA), lambda b, p, pt: (0, 0)),
        ],
        out_specs=pl.BlockSpec((1, 1, w), lambda b, p, pt: (b, 0, 0)),
        scratch_shapes=[pltpu.VMEM((2 * HA, DVA), F32), pltpu.VMEM((2 * HA, 1), F32),
                        pltpu.VMEM((2 * HA, 1), F32), pltpu.VMEM((2 * HA, DVA), F32)],
    )
    out = pl.pallas_call(
        functools.partial(_dattn_sample_kernel, n_pages=n_pages, page=page, pps=pps),
        grid_spec=grid_spec,
        out_shape=jax.ShapeDtypeStruct((nb, 1, w), F32),
        compiler_params=_cparams(("arbitrary", "arbitrary")),
        name="dattn_sample",
    )(page_table.reshape(-1), row3(z), row3(ka), row3(va), *([cache_k] * pps), *([cache_v] * pps),
      *[l.reshape(1, DKA) for l in lams], subln_gain.reshape(1, DVA))
    return out.reshape(nb, w)


def _ret_log_gamma(h):
    return math.log(1.0 - 2.0 ** (-5.0 - h))


def _ret_theta(width):
    theta = 1.0 / (ROPE_BASE ** jnp.linspace(0.0, 1.0, DKB // 2, dtype=F32))
    return jnp.tile(jnp.repeat(theta, 2), width // DKB).reshape(1, width)


def _rotate_pairs(x, cos, sin):
    n = x.shape[-1]
    lane = lax.broadcasted_iota(jnp.int32, x.shape, x.ndim - 1)
    nxt = pltpu.roll(x, n - 1, axis=x.ndim - 1)
    prv = pltpu.roll(x, 1, axis=x.ndim - 1)
    return x * cos + jnp.where(lane % 2 == 0, -nxt, prv) * sin


def _ret_prompt_kernel(q_ref, k_ref, v_ref, g_ref, th_ref, o_ref, s_ref, dec_ref, cos_ref, sin_ref, *, c):
    ci = pl.program_id(1)
    ri = lax.broadcasted_iota(jnp.int32, (c, c), 0)
    cj = lax.broadcasted_iota(jnp.int32, (c, c), 1)
    idx = lax.broadcasted_iota(jnp.int32, (c, 1), 0)

    @pl.when(jnp.logical_and(pl.program_id(0) == 0, ci == 0))
    def _():
        diff = (ri - cj).astype(F32)
        for h in range(HB):
            dec_ref[h] = jnp.where(diff >= 0, jnp.exp(_ret_log_gamma(h) * jnp.maximum(diff, 0.0)), 0.0)
        within = idx.astype(F32) * th_ref[...]
        cos_ref[...] = jnp.cos(within)
        sin_ref[...] = jnp.sin(within)

    @pl.when(ci == 0)
    def _():
        s_ref[...] = jnp.zeros(s_ref.shape, F32)

    start = (ci * c).astype(F32) * th_ref[...]
    c0, s0 = jnp.cos(start), jnp.sin(start)
    cos = jnp.concatenate([cos_ref[...] * c0 - sin_ref[...] * s0] * (HB * DKB // 128), axis=-1)
    sin = jnp.concatenate([sin_ref[...] * c0 + cos_ref[...] * s0] * (HB * DKB // 128), axis=-1)
    qr = _rotate_pairs(q_ref[...], cos, sin)
    kr = _rotate_pairs(k_ref[...], cos, sin) * (DKB ** -0.5)
    idx_f = idx.astype(F32)
    outs = []
    for h in range(HB):
        lg = _ret_log_gamma(h)
        qh = qr[:, h * DKB:(h + 1) * DKB].astype(BF16)
        kh = kr[:, h * DKB:(h + 1) * DKB]
        vh = v_ref[:, h * DVB:(h + 1) * DVB].astype(BF16)
        s_old = s_ref[0, h]
        scores = _dot_nt(qh, kh.astype(BF16)) * dec_ref[h]
        o = _dot(scores.astype(BF16), vh)
        o = o + _dot(qh, s_old.astype(BF16)) * jnp.exp(lg * (idx_f + 1.0))
        kd = (kh * jnp.exp(lg * (c - 1.0 - idx_f))).astype(BF16)
        s_ref[0, h] = math.exp(lg * c) * s_old + _dot_tn(kd, vh)
        gate = g_ref[:, h * DVB:(h + 1) * DVB]
        outs.append(_rms(o) * (gate * _sigmoid(gate)))
    o_ref[...] = jnp.concatenate(outs, axis=-1)


def _ret_prompt(z, batch, seq, c):
    nc = seq // c
    wq = HB * DKB
    wv = HB * DVB
    return pl.pallas_call(
        functools.partial(_ret_prompt_kernel, c=c),
        grid=(batch, nc),
        in_specs=[
            pl.BlockSpec((c, wq), lambda b, i: (b * nc + i, 2)),
            pl.BlockSpec((c, wq), lambda b, i: (b * nc + i, 3)),
            pl.BlockSpec((c, wv), lambda b, i: (b * nc + i, 2)),
            pl.BlockSpec((c, wv), lambda b, i: (b * nc + i, 3)),
            pl.BlockSpec((1, 128), lambda b, i: (0, 0)),
        ],
        out_specs=[
            pl.BlockSpec((c, wv), lambda b, i: (b * nc + i, 0)),
            pl.BlockSpec((1, HB, DKB, DVB), lambda b, i: (b, 0, 0, 0)),
        ],
        out_shape=[
            jax.ShapeDtypeStruct((batch * seq, wv), F32),
            jax.ShapeDtypeStruct((batch, HB, DKB, DVB), F32),
        ],
        scratch_shapes=[pltpu.VMEM((HB, c, c), F32), pltpu.VMEM((c, 128), F32), pltpu.VMEM((c, 128), F32)],
        compiler_params=_cparams(("arbitrary", "arbitrary")),
        name="ret_prompt",
    )(z, z, z, z, _ret_theta(128))


def _ret_sample_kernel(qk_ref, v_ref, g_ref, th_ref, s_ref, o_ref, so_ref, *, pos):
    ang = float(pos) * th_ref[...]
    x = qk_ref[...]
    ncol = x.shape[1]
    d = lax.broadcasted_iota(jnp.int32, x.shape, 0)
    partner = jnp.where(d % 2 == 0, -pltpu.roll(x, DKB - 1, axis=0), pltpu.roll(x, 1, axis=0))
    qk = x * jnp.cos(ang) + partner * jnp.sin(ang)
    is_k = lax.broadcasted_iota(jnp.int32, (1, ncol), 1) % (2 * HB) >= HB
    qkb = (qk * jnp.where(is_k, DKB ** -0.5, 1.0)).astype(BF16)
    sel_row = lax.broadcasted_iota(jnp.int32, (ncol, HB * DVB), 0)
    sel_head = lax.broadcasted_iota(jnp.int32, (ncol, HB * DVB), 1) // DVB
    rows = []
    for s in range(v_ref.shape[0]):
        q_sel = jnp.where(sel_row == s * 2 * HB + sel_head, 1.0, 0.0)
        kv_sel = jnp.where(sel_row == s * 2 * HB + HB + sel_head, v_ref[s:s + 1, :], 0.0)
        res = _dot(qkb, jnp.concatenate([q_sel, kv_sel], axis=-1).astype(BF16))
        outs = []
        for h in range(HB):
            q_lanes = res[:, h * DVB:(h + 1) * DVB]
            outer = res[:, (HB + h) * DVB:(HB + h + 1) * DVB]
            s_new = math.exp(_ret_log_gamma(h)) * s_ref[s, h] + outer
            so_ref[s, h] = s_new
            o = jnp.sum(q_lanes * s_new, axis=0, keepdims=True)
            gate = g_ref[s:s + 1, h * DVB:(h + 1) * DVB]
            outs.append(_rms(o) * (gate * _sigmoid(gate)))
        rows.append(jnp.concatenate(outs, axis=-1))
    o_ref[...] = jnp.concatenate(rows, axis=0)


def _ret_sample(z, state, pos):
    nb = z.shape[0]
    wq = HB * DKB
    wv = HB * DVB
    sb = 8 if nb % 8 == 0 else nb
    qk_t = jnp.transpose(z[:, wv:wv + 2 * wq].reshape(nb, 2 * HB, DKB), (2, 0, 1)).reshape(DKB, nb * 2 * HB)
    theta_col = _ret_theta(DKB).reshape(DKB, 1)
    return pl.pallas_call(
        functools.partial(_ret_sample_kernel, pos=pos),
        grid=(nb // sb,),
        in_specs=[
            pl.BlockSpec((DKB, sb * 2 * HB), lambda b: (0, b)),
            pl.BlockSpec((sb, wv), lambda b: (b, 2)),
            pl.BlockSpec((sb, wv), lambda b: (b, 3)),
            pl.BlockSpec((DKB, 1), lambda b: (0, 0)),
            pl.BlockSpec((sb, HB, DKB, DVB), lambda b: (b, 0, 0, 0)),
        ],
        out_specs=[
            pl.BlockSpec((sb, wv), lambda b: (b, 0)),
            pl.BlockSpec((sb, HB, DKB, DVB), lambda b: (b, 0, 0, 0)),
        ],
        out_shape=[
            jax.ShapeDtypeStruct((nb, wv), F32),
            jax.ShapeDtypeStruct(state.shape, F32),
        ],
        compiler_params=_cparams(("arbitrary",)),
        name="ret_sample",
    )(qk_t, z, z, theta_col, state)


def _mm_resid_kernel(*refs, n_in):
    a_refs = refs[:n_in]
    w_refs = refs[n_in:2 * n_in]
    x_ref = refs[2 * n_in]
    o_ref = refs[2 * n_in + 1]
    acc = x_ref[...]
    for a_ref, w_ref in zip(a_refs, w_refs):
        acc = acc + _dot(a_ref[...].astype(BF16), w_ref[...])
    o_ref[...] = acc


def _mm_resid(a_list, w_bf16, resid, tm):
    m, n = resid.shape
    tn = n
    in_specs = []
    for a in a_list:
        in_specs.append(pl.BlockSpec((tm, a.shape[1]), lambda i, j: (i, 0)))
    row = 0
    for a in a_list:
        kb = a.shape[1]
        in_specs.append(pl.BlockSpec((kb, tn), lambda i, j, r=row // kb: (r, j)))
        row += kb
    in_specs.append(pl.BlockSpec((tm, tn), lambda i, j: (i, j)))
    return pl.pallas_call(
        functools.partial(_mm_resid_kernel, n_in=len(a_list)),
        grid=(m // tm, n // tn),
        in_specs=in_specs,
        out_specs=pl.BlockSpec((tm, tn), lambda i, j: (i, j)),
        out_shape=jax.ShapeDtypeStruct((m, n), F32),
        compiler_params=_cparams(("arbitrary", "arbitrary")),
        name="mm_resid",
    )(*a_list, *([w_bf16] * len(a_list)), resid)


def _ffn_kernel(*refs, final_norm, emit_bf16, aliased):
    x_ref, g_ref, wg_ref, wu_ref, wd_ref, fg_ref = refs[:6]
    refs = refs[6 + aliased:]
    if emit_bf16:
        o_ref, wgb_ref, wub_ref, wdb_ref, xn_ref, acc_ref = refs
    else:
        o_ref, xn_ref, acc_ref = refs
    j = pl.program_id(1)

    @pl.when(j == 0)
    def _():
        xn_ref[...] = (_rms(x_ref[...]) * g_ref[...]).astype(BF16)
        acc_ref[...] = jnp.zeros(acc_ref.shape, F32)

    wg, wu, wd = wg_ref[...].astype(BF16), wu_ref[...].astype(BF16), wd_ref[...].astype(BF16)
    if emit_bf16:
        wgb_ref[...], wub_ref[...], wdb_ref[...] = wg, wu, wd
    xn = xn_ref[...]
    gate = _dot(xn, wg)
    up = _dot(xn, wu)
    hid = (gate * _sigmoid(gate) * up).astype(BF16)
    acc_ref[...] += _dot(hid, wd)

    @pl.when(j == pl.num_programs(1) - 1)
    def _():
        y = x_ref[...] + acc_ref[...]
        if final_norm:
            y = _rms(y) * fg_ref[...]
        o_ref[...] = y


def _ffn_call(x, gain, wg, wu, wd, final_gain, final_norm, tm, tf, row_tiles, done=None):
    m = x.shape[0]
    dff = wg.shape[1]
    t0, t1 = row_tiles
    emit_bf16 = wg.dtype == F32
    w_specs = [pl.BlockSpec((D_MODEL, tf), lambda i, j: (0, j)),
               pl.BlockSpec((D_MODEL, tf), lambda i, j: (0, j)),
               pl.BlockSpec((tf, D_MODEL), lambda i, j: (j, 0))]
    in_specs = [pl.BlockSpec((tm, D_MODEL), lambda i, j: (i + t0, 0)),
                pl.BlockSpec((1, D_MODEL), lambda i, j: (0, 0)),
                *w_specs,
                pl.BlockSpec((1, D_MODEL), lambda i, j: (0, 0))]
    args = [x, gain.reshape(1, D_MODEL), wg, wu, wd, final_gain.reshape(1, D_MODEL)]
    out_specs = [pl.BlockSpec((tm, D_MODEL), lambda i, j: (i + t0, 0))]
    out_shape = [jax.ShapeDtypeStruct((m, D_MODEL), F32)]
    aliases = {}
    if done is not None:
        in_specs.append(pl.BlockSpec(memory_space=pl.ANY))
        args.append(done)
        aliases = {len(args) - 1: 0}
    if emit_bf16:
        out_specs += w_specs
        out_shape += [jax.ShapeDtypeStruct(w.shape, BF16) for w in (wg, wu, wd)]
    res = pl.pallas_call(
        functools.partial(_ffn_kernel, final_norm=final_norm, emit_bf16=emit_bf16, aliased=done is not None),
        grid=(t1 - t0, dff // tf),
        in_specs=in_specs,
        out_specs=out_specs,
        out_shape=out_shape,
        input_output_aliases=aliases,
        scratch_shapes=[pltpu.VMEM((tm, D_MODEL), BF16), pltpu.VMEM((tm, D_MODEL), F32)],
        compiler_params=_cparams(("arbitrary", "arbitrary")),
        name="ffn",
    )(*args)
    return res if emit_bf16 else res[0]


def _ffn(x, gain, weights, final_gain, final_norm, tm, tf=512):
    n_tiles = x.shape[0] // tm
    if weights[0].dtype != F32:
        return _ffn_call(x, gain, *weights, final_gain, final_norm, tm, tf, (0, n_tiles)), weights
    y, *weights_bf16 = _ffn_call(x, gain, *weights, final_gain, final_norm, tm, tf // 2, (0, 1))
    if n_tiles > 1:
        y = _ffn_call(x, gain, *weights_bf16, final_gain, final_norm, tm, tf, (1, n_tiles), done=y)
    return y, tuple(weights_bf16)


def _norm_and_shift(x_ref, g_ref, prev_ref, carry_ref, i, tiles_per_seq):
    h = _rms(x_ref[...]) * g_ref[...]
    if prev_ref is None:
        tm = h.shape[0]
        first = (i % tiles_per_seq) == 0
        prev_row = jnp.where(first, 0.0, carry_ref[0:1, :])
        rows = lax.broadcasted_iota(jnp.int32, (tm, 1), 0)
        shifted = jnp.where(rows == 0, prev_row, pltpu.roll(h, 1, axis=0))
        carry_ref[0:1, :] = h[tm - 1:tm, :]
    else:
        shifted = prev_ref[...]
    return h, shifted - h


def _rwkv_rkv_kernel(*refs, per_row_prev, tiles_per_seq):
    if per_row_prev:
        x_ref, g_ref, prev_ref, mu_ref, w_ref, o_ref, hl_ref, h_ref, xx_ref, carry_ref = refs
    else:
        x_ref, g_ref, mu_ref, w_ref, o_ref, hl_ref, h_ref, xx_ref, carry_ref = refs
        prev_ref = None
    i = pl.program_id(0)
    j = pl.program_id(1)

    @pl.when(j == 0)
    def _():
        h, xx = _norm_and_shift(x_ref, g_ref, prev_ref, carry_ref, i, tiles_per_seq)
        h_ref[...] = h
        xx_ref[...] = xx
        if per_row_prev:
            hl_ref[...] = h
        else:
            hl_ref[0] = h[h.shape[0] - 1:, :]

    xm = (h_ref[...] + xx_ref[...] * mu_ref[0]).astype(BF16)
    o_ref[...] = _dot(xm, w_ref[...])


def _rwkv_rkv(x, gain, prev, mu_rkv, w_rkv, tm, tiles_per_seq):
    m = x.shape[0]
    tn = D_MODEL
    per_seq = D_MODEL // tn
    per_row_prev = prev is not None
    in_specs = [pl.BlockSpec((tm, D_MODEL), lambda i, j: (i, 0)),
                pl.BlockSpec((1, D_MODEL), lambda i, j: (0, 0))]
    args = [x, gain.reshape(1, D_MODEL)]
    if per_row_prev:
        in_specs.append(pl.BlockSpec((tm, D_MODEL), lambda i, j: (i, 0)))
        args.append(prev)
        hl_spec = pl.BlockSpec((tm, D_MODEL), lambda i, j: (i, 0))
        hl_shape = jax.ShapeDtypeStruct((m, D_MODEL), F32)
    else:
        nseq = m // (tm * tiles_per_seq)
        hl_spec = pl.BlockSpec((1, 1, D_MODEL), lambda i, j: (i // tiles_per_seq, 0, 0))
        hl_shape = jax.ShapeDtypeStruct((nseq, 1, D_MODEL), F32)
    in_specs += [pl.BlockSpec((1, 1, D_MODEL), lambda i, j: (j // per_seq, 0, 0)),
                 pl.BlockSpec((None, D_MODEL, tn), lambda i, j: (j // per_seq, 0, j % per_seq))]
    args += [mu_rkv, w_rkv]
    return pl.pallas_call(
        functools.partial(_rwkv_rkv_kernel, per_row_prev=per_row_prev, tiles_per_seq=tiles_per_seq),
        grid=(m // tm, 3 * per_seq),
        in_specs=in_specs,
        out_specs=[pl.BlockSpec((tm, tn), lambda i, j: (i, j)), hl_spec],
        out_shape=[jax.ShapeDtypeStruct((m, 3 * D_MODEL), F32), hl_shape],
        scratch_shapes=[pltpu.VMEM((tm, D_MODEL), F32), pltpu.VMEM((tm, D_MODEL), F32),
                        pltpu.VMEM((8, D_MODEL), F32)],
        compiler_params=_cparams(("arbitrary", "arbitrary")),
        name="rwkv_rkv",
    )(*args)


def _rwkv_lora_kernel(*refs, per_row_prev, tiles_per_seq):
    if per_row_prev:
        (x_ref, g_ref, prev_ref, mu_ref, w0_ref, w1_ref, w2_ref, a0_ref, a1_ref, a2_ref, g1_ref, g2_ref,
         lw_ref, a_ref, go_ref, carry_ref) = refs
    else:
        (x_ref, g_ref, mu_ref, w0_ref, w1_ref, w2_ref, a0_ref, a1_ref, a2_ref, g1_ref, g2_ref,
         lw_ref, a_ref, go_ref, carry_ref) = refs
        prev_ref = None
    h, xx = _norm_and_shift(x_ref, g_ref, prev_ref, carry_ref, pl.program_id(0), tiles_per_seq)
    xw = (h + xx * mu_ref[0]).astype(BF16)
    xa = (h + xx * mu_ref[1]).astype(BF16)
    xg = (h + xx * mu_ref[2]).astype(BF16)
    zw = w0_ref[...] + _dot(jnp.tanh(_dot(xw, w1_ref[...])).astype(BF16), w2_ref[...])
    lw_ref[...] = -math.exp(-0.5) * _sigmoid(zw)
    a_ref[...] = _sigmoid(a0_ref[...] + _dot(_dot(xa, a1_ref[...]).astype(BF16), a2_ref[...]))
    go_ref[...] = _dot(_sigmoid(_dot(xg, g1_ref[...])).astype(BF16), g2_ref[...])


def _rwkv_lora(x, gain, prev, mu_wag, w0, w1, w2, a0, a1, a2, g1, g2, tm, tiles_per_seq):
    m = x.shape[0]
    per_row_prev = prev is not None
    full = lambda a: pl.BlockSpec(a.shape, lambda i: (0,) * a.ndim)
    row = pl.BlockSpec((tm, D_MODEL), lambda i: (i, 0))
    args = [x, gain.reshape(1, D_MODEL)]
    in_specs = [row, full(args[1])]
    if per_row_prev:
        in_specs.append(row)
        args.append(prev)
    rest = [mu_wag, w0.reshape(1, D_MODEL), w1, w2, a0.reshape(1, D_MODEL), a1, a2, g1, g2]
    in_specs += [full(a) for a in rest]
    args += rest
    return pl.pallas_call(
        functools.partial(_rwkv_lora_kernel, per_row_prev=per_row_prev, tiles_per_seq=tiles_per_seq),
        grid=(m // tm,),
        in_specs=in_specs,
        out_specs=[row, row, row],
        out_shape=[jax.ShapeDtypeStruct((m, D_MODEL), F32)] * 3,
        scratch_shapes=[pltpu.VMEM((8, D_MODEL), F32)],
        compiler_params=_cparams(("arbitrary",)),
        name="rwkv_lora",
    )(*args)


def _wkv_keys(k, a, kk_gain, ka_gain, head_sum):
    kk = k * kk_gain
    kk = kk / jnp.maximum(jnp.sqrt(head_sum(kk * kk)), 1e-12)
    return kk, kk * a, k * (1.0 + (a - 1.0) * ka_gain)


def _head_norm(o, head_sum):
    mean = head_sum(o) * (1.0 / NC)
    var = head_sum(jnp.square(o - mean)) * (1.0 / NC)
    return (o - mean) * lax.rsqrt(var + GN_EPS)


def _wkv_finish(on, r, k2, v, g, rk, ln_g, ln_b, head_sum):
    bonus = head_sum(r * k2 * rk) * v
    return (on * ln_g + ln_b + bonus) * g


def _pair_sum(x):
    lane = lax.broadcasted_iota(jnp.int32, x.shape, 1)
    s0 = jnp.sum(x[:, :NC], axis=-1, keepdims=True)
    s1 = jnp.sum(x[:, NC:], axis=-1, keepdims=True)
    return jnp.where(lane < NC, s0, s1)


def _bmm(a, b):
    return jnp.einsum("cij,cjk->cik", a, b, preferred_element_type=F32)


def _bmm_nt(a, b):
    return jnp.einsum("cik,cjk->cij", a, b, preferred_element_type=F32)


def _wkv_prompt_kernel(r_ref, k_ref, v_ref, lw_ref, a_ref, kkg_ref, kag_ref, rk_ref, g_ref, lng_ref, lnb_ref,
                       y_ref, so_ref, st_ref, gh_ref, ghb_ref, e0t_ref, ft_ref, wend_ref, bonus_ref, *, tc, nt):
    c = WKV_CHUNK
    nch = tc // c
    s = pl.program_id(0)

    @pl.when(s == 0)
    def _():
        for ref in (st_ref, gh_ref, ghb_ref, e0t_ref, ft_ref, wend_ref, bonus_ref):
            ref[...] = jnp.zeros(ref.shape, ref.dtype)

    prev = jnp.maximum(s - 1, 0)
    slot = prev % 2
    first = prev % nt == 0
    ss = [jnp.where(first, 0.0, st_ref[0]), jnp.where(first, 0.0, st_ref[1])]
    o_chunks = [[], []]
    lane_sum = lambda t: jnp.sum(t, axis=-1, keepdims=True)
    n_points = 2 + int(math.log2(c))
    done = [0]

    def advance_stage2(point):
        upto = nch * (point + 1) // n_points
        for ch in range(done[0], upto):
            w_end = wend_ref[slot, ch]
            for hh in range(2):
                n = hh * nch + ch
                other = slice((1 - hh) * NC, (2 - hh) * NC)
                sb = ss[hh].astype(BF16)
                o = _dot_nt(ghb_ref[slot, n], sb) + gh_ref[slot, n][:, other]
                o_chunks[hh].append(_head_norm(o, lane_sum))
                ss[hh] = ss[hh] * w_end + _dot(sb, e0t_ref[slot, n]) + ft_ref[slot, n]
        done[0] = upto

    ri = lax.broadcasted_iota(jnp.int32, (c, c), 0)
    cj = lax.broadcasted_iota(jnp.int32, (c, c), 1)
    tri = (ri >= cj).astype(BF16)
    gi = lax.broadcasted_iota(jnp.int32, (2 * c, 2 * c), 0)
    gj = lax.broadcasted_iota(jnp.int32, (2 * c, 2 * c), 1) % c
    gmask = (gj < jnp.where(gi < c, gi, gi - c + 1))[None]

    r = r_ref[...]
    v = v_ref[...]
    kk, bv, k2 = _wkv_keys(k_ref[...], a_ref[...], kkg_ref[...], kag_ref[...], _pair_sum)
    to3 = lambda x: x.reshape(nch, c, 2 * NC)
    lw3 = to3(lw_ref[...])
    p0 = lw3.astype(BF16)
    r1 = lw3 - p0.astype(F32)
    p1 = r1.astype(BF16)
    p2 = (r1 - p1.astype(F32)).astype(BF16)
    trib = jnp.broadcast_to(tri[None], (nch, c, c))
    cum = _bmm(trib, p0) + (_bmm(trib, p1) + _bmm(trib, p2))
    cum_last = cum[:, c - 1:c, :]
    w_rest = jnp.exp(cum_last - cum)
    w_inv = jnp.exp(-cum)
    am = -to3(kk) * jnp.exp(cum - lw3)
    rm = to3(r) * jnp.exp(cum)
    bm = (to3(bv) * w_inv).astype(BF16)
    km = (to3(k2) * w_inv).astype(BF16)
    bt = (to3(bv) * w_rest).astype(BF16)
    kt = (to3(k2) * w_rest).astype(BF16)
    v3 = to3(v)
    advance_stage2(0)

    both = lambda x: jnp.concatenate([x, x], axis=0)
    lane = lax.broadcasted_iota(jnp.int32, (2 * nch, 1, 2 * NC), 2)
    head = lax.broadcasted_iota(jnp.int32, (2 * nch, 1, 2 * NC), 0) // nch
    own = (lane // NC) == head
    v_sw = both(pltpu.roll(v, NC, axis=1).reshape(nch, c, 2 * NC))
    x1 = both(jnp.concatenate([am, rm], axis=1).astype(BF16))
    x2 = jnp.where(own, both(jnp.concatenate([bm, km], axis=1)), 0.0)
    g4 = jnp.where(gmask, _bmm_nt(x1, x2), 0.0)
    lv = _bmm(g4[:, :, c:].astype(BF16), v_sw.astype(BF16))
    advance_stage2(1)
    x = jnp.where(own, both(am), lv[:, :c])
    lp = g4[:, :c, :c].astype(BF16)
    steps = int(math.log2(c))
    for i in range(steps - 1):
        prod = _bmm(lp, jnp.concatenate([x.astype(BF16), lp], axis=-1))
        x = x + prod[:, :, :2 * NC]
        lp = prod[:, :, 2 * NC:].astype(BF16)
        advance_stage2(2 + i)
    x = x + _bmm(lp, x.astype(BF16))
    advance_stage2(n_points - 1)
    gh_new = _bmm(g4[:, c:, :c].astype(BF16), x.astype(BF16)) + jnp.where(own, both(rm), lv[:, c:])
    btk = jnp.where(own, both(jnp.concatenate([bt, kt], axis=1)), 0.0)
    pqv = jnp.concatenate([x, jnp.where(own, 0.0, v_sw)], axis=1).astype(BF16)
    st_ref[0] = ss[0]
    st_ref[1] = ss[1]
    so_ref[0, 0] = ss[0][:, :NC]
    so_ref[0, 1] = ss[1][:, NC:]
    on = jnp.concatenate([jnp.concatenate(oc, axis=0) for oc in o_chunks], axis=-1)
    y_ref[...] = (on * lng_ref[...] + lnb_ref[...] + bonus_ref[slot]) * g_ref[...]

    nxt = s % 2
    gh_ref[nxt] = gh_new
    ghb_ref[nxt] = gh_new.astype(BF16)
    for n in range(2 * nch):
        eft = _dot_tn(pqv[n], btk[n])
        e0t_ref[nxt, n] = eft.astype(BF16)
        ft_ref[nxt, n] = eft[NC:] if n < nch else eft[:NC]
    wend_ref[nxt] = jnp.exp(cum_last)
    bonus_ref[nxt] = _pair_sum(r * k2 * rk_ref[...]) * v


def _wkv_prompt(rkv, lw, a, g, kk_gain, ka_gain, rk, ln_g, ln_b, batch, seq, tc):
    nt = seq // tc
    nhp = HC // 2
    nblk = batch * nhp * nt
    nch = tc // WKV_CHUNK

    def rows_cols(blk):
        b = blk // (nhp * nt)
        return b * nt + blk % nt, (blk // nt) % nhp, b

    cur = lambda s: jnp.minimum(s, nblk - 1)
    prv = lambda s: jnp.maximum(s - 1, 0)
    tok = lambda off, which: pl.BlockSpec(
        (tc, 2 * NC), lambda s: (rows_cols(which(s))[0], off + rows_cols(which(s))[1]))
    par = lambda which: pl.BlockSpec((1, 2 * NC), lambda s: (0, rows_cols(which(s))[1]))
    return pl.pallas_call(
        functools.partial(_wkv_prompt_kernel, tc=tc, nt=nt),
        grid=(nblk + 1,),
        in_specs=[tok(0, cur), tok(nhp, cur), tok(2 * nhp, cur), tok(0, cur), tok(0, cur),
                  par(cur), par(cur), par(cur), tok(0, prv), par(prv), par(prv)],
        out_specs=[
            tok(0, prv),
            pl.BlockSpec((1, 2, NC, NC), lambda s: (rows_cols(prv(s))[2], rows_cols(prv(s))[1], 0, 0)),
        ],
        out_shape=[
            jax.ShapeDtypeStruct((batch * seq, D_MODEL), F32),
            jax.ShapeDtypeStruct((batch, HC, NC, NC), F32),
        ],
        scratch_shapes=[
            pltpu.VMEM((2, NC, 2 * NC), F32),
            pltpu.VMEM((2, 2 * nch, WKV_CHUNK, 2 * NC), F32),
            pltpu.VMEM((2, 2 * nch, WKV_CHUNK, 2 * NC), BF16),
            pltpu.VMEM((2, 2 * nch, 2 * NC, 2 * NC), BF16),
            pltpu.VMEM((2, 2 * nch, NC, 2 * NC), F32),
            pltpu.VMEM((2, nch, 1, 2 * NC), F32),
            pltpu.VMEM((2, tc, 2 * NC), F32),
        ],
        compiler_params=_cparams(("arbitrary",)),
        name="wkv_prompt",
    )(rkv, rkv, rkv, lw, a, kk_gain, ka_gain, rk, g, ln_g, ln_b)


def _wkv_sample_kernel(r_ref, k_ref, v_ref, lw_ref, a_ref, g_ref, kkg_ref, kag_ref, rk_ref, lng_ref, lnb_ref,
                       s_ref, y_ref, so_ref, o_ref):
    ys = []
    col_sum = lambda x: jnp.sum(x, axis=0, keepdims=True)
    for hh in range(2):
        sl = slice(hh * NC, (hh + 1) * NC)
        t = lambda ref: ref[...].T[sl, :]
        r, k, v, a, g = t(r_ref), t(k_ref), t(v_ref), t(a_ref), t(g_ref)
        w = jnp.exp(t(lw_ref))
        kk, bv, k2 = _wkv_keys(k, a, kkg_ref[sl, :], kag_ref[sl, :], col_sum)
        for i in range(NC):
            s_old = s_ref[hh, i]
            sk = jnp.sum(s_old * kk, axis=0, keepdims=True)
            s_new = s_old * w - sk * bv + v[i:i + 1, :] * k2
            so_ref[hh, i] = s_new
            o_ref[i:i + 1, :] = jnp.sum(s_new * r, axis=0, keepdims=True)
        ys.append(_wkv_finish(_head_norm(o_ref[...], col_sum), r, k2, v, g,rk_ref[sl, :], lng_ref[sl, :], lnb_ref[sl, :], col_sum))
    y_ref[...] = jnp.concatenate(ys, axis=0).T


def _wkv_sample(rkv, lw, a, g, kk_gain, ka_gain, rk, ln_g, ln_b, state_t):
    nb = lw.shape[0]
    nhp = HC // 2
    tok = lambda off: pl.BlockSpec((nb, 2 * NC), lambda hp: (0, off + hp))
    par = pl.BlockSpec((2 * NC, 1), lambda hp: (hp, 0))
    col = lambda p: p.reshape(D_MODEL, 1)
    st_spec = pl.BlockSpec((2, NC, NC, nb), lambda hp: (hp, 0, 0, 0))
    return pl.pallas_call(
        _wkv_sample_kernel,
        grid=(nhp,),
        in_specs=[tok(0), tok(nhp), tok(2 * nhp), tok(0), tok(0), tok(0), par, par, par, par, par, st_spec],
        out_specs=[pl.BlockSpec((nb, 2 * NC), lambda hp: (0, hp)), st_spec],
        out_shape=[
            jax.ShapeDtypeStruct((nb, D_MODEL), F32),
            jax.ShapeDtypeStruct(state_t.shape, F32),
        ],
        scratch_shapes=[pltpu.VMEM((NC, nb), F32)],
        compiler_params=_cparams(("arbitrary",)),
        name="wkv_sample",
    )(rkv, rkv, rkv, lw, a, g, col(kk_gain), col(ka_gain), col(rk), col(ln_g), col(ln_b), state_t)


def _row_tile(m, want):
    return min(m, want)


def _trunk(x, batch, seq, past, W):
    m = x.shape[0]
    prompt = past is None
    tm = _row_tile(m, 512)
    lams = (W["lambda_q1"], W["lambda_k1"], W["lambda_q2"], W["lambda_k2"])

    ka, va, z = _inproj0(x, W["norm0_mix"], W["w_in0"], tm)
    if prompt:
        o_a = _dattn_prompt(z, ka, va, lams, W["subln_gain"], batch, seq, min(seq, 512))
        o_b, ret_new = _ret_prompt(z, batch, seq, min(seq, 256))
    else:
        o_a = _dattn_sample(z, ka, va, past["cache_k"], past["cache_v"], past["page_table"], lams,
                            W["subln_gain"])
        o_b, ret_new = _ret_sample(z, past["state_ret"], past["pos"])
    x = _mm_resid([o_a, o_b], W["w_out0"], x, _row_tile(m, 512))
    x, W["ffn0"] = _ffn(x, W["norm0_ffn"], W["ffn0"], W["norm_final"], False, tm)

    tiles_per_seq = max(seq // tm, 1)
    prev = None if prompt else past["state_shift"]
    rkv, hlast = _rwkv_rkv(x, W["norm1_mix"], prev, W["mu_rkv"], W["w_rkv"], tm, tiles_per_seq)
    tl = _row_tile(m, 256)
    lw, a, g = _rwkv_lora(x, W["norm1_mix"], prev, W["mu_wag"], W["decay_w0"], W["decay_w1"], W["decay_w2"],
                          W["aaa_a0"], W["aaa_a1"], W["aaa_a2"], W["gate_g1"], W["gate_g2"],
                          tl, max(seq // tl, 1))
    head_params = (W["k_k"], W["k_a"], W["r_k"], W["lnx_gain"], W["lnx_bias"])
    if prompt:
        y, wkv_new = _wkv_prompt(rkv, lw, a, g, *head_params, batch, seq, min(seq, 512))
        shift_new = hlast.reshape(batch, D_MODEL)
    else:
        state_t = jnp.transpose(past["state_wkv"], (1, 2, 3, 0))
        y, wkv_t = _wkv_sample(rkv, lw, a, g, *head_params, state_t)
        wkv_new = jnp.transpose(wkv_t, (3, 0, 1, 2))
        shift_new = hlast
    x = _mm_resid([y], W["w_out1"], x, _row_tile(m, 512))
    x, W["ffn1"] = _ffn(x, W["norm1_ffn"], W["ffn1"], W["norm_final"], True, tm)
    return x, ka, va, ret_new, wkv_new, shift_new


def kernel(x_prompt, x_sample, cache_k, cache_v, page_table, state_ret, state_wkv, state_shift, norm0_mix, w_in0, lambda_q1, lambda_k1, lambda_q2, lambda_k2, subln_gain, w_out0, norm0_ffn, w_gate0, w_up0, w_down0, norm1_mix, mu1, w_r1, w_k1, w_v1, decay_w0, decay_w1, decay_w2, aaa_a0, aaa_a1, aaa_a2, gate_g1, gate_g2, k_k, k_a, r_k, lnx_gain, lnx_bias, w_out1, norm1_ffn, w_gate1, w_up1, w_down1, norm_final):
    bf = lambda w: w.astype(BF16)
    row = lambda p: p.reshape(1, D_MODEL)
    W = dict(
        norm0_mix=norm0_mix, w_in0=bf(w_in0), lambda_q1=lambda_q1, lambda_k1=lambda_k1, lambda_q2=lambda_q2,
        lambda_k2=lambda_k2, subln_gain=subln_gain, w_out0=bf(w_out0), norm0_ffn=norm0_ffn,
        ffn0=(w_gate0, w_up0, w_down0),
        norm1_mix=norm1_mix,
        mu_rkv=jnp.stack([mu1[0], mu1[2], mu1[3]]).reshape(3, 1, D_MODEL),
        mu_wag=jnp.stack([mu1[1], mu1[4], mu1[5]]).reshape(3, 1, D_MODEL),
        w_rkv=jnp.stack([bf(w_r1), bf(w_k1), bf(w_v1)]),
        decay_w0=decay_w0, decay_w1=bf(decay_w1), decay_w2=bf(decay_w2),
        aaa_a0=aaa_a0, aaa_a1=bf(aaa_a1), aaa_a2=bf(aaa_a2), gate_g1=bf(gate_g1), gate_g2=bf(gate_g2),
        k_k=row(k_k), k_a=row(k_a), r_k=row(r_k), lnx_gain=row(lnx_gain), lnx_bias=row(lnx_bias),
        w_out1=bf(w_out1), norm1_ffn=norm1_ffn, ffn1=(w_gate1, w_up1, w_down1),
        norm_final=norm_final,
    )
    bsz, seq = x_prompt.shape[:2]
    y_p, k_p, v_p, ret_p, wkv_p, shift_p = _trunk(x_prompt.reshape(bsz * seq, D_MODEL), bsz, seq, None, W)
    nb, dseq = x_sample.shape[:2]
    past_len = page_table.shape[1] * cache_k.shape[1]
    past = dict(cache_k=cache_k, cache_v=cache_v, page_table=page_table, state_ret=state_ret,
                state_wkv=state_wkv, state_shift=state_shift, pos=past_len)
    y_s, k_s, v_s, ret_s, wkv_s, shift_s = _trunk(x_sample.reshape(nb * dseq, D_MODEL), nb, dseq, past, W)
    return (y_p.reshape(bsz, seq, D_MODEL), y_s.reshape(nb, dseq, D_MODEL),
            k_p.reshape(bsz, seq, HA, 2 * DKA), v_p.reshape(bsz, seq, HA, DVA),
            k_s.reshape(nb, dseq, HA, 2 * DKA), v_s.reshape(nb, dseq, HA, DVA),
            ret_p, ret_s, wkv_p, wkv_s, shift_p, shift_s)
```

```python
import functools
import math

import jax
import jax.numpy as jnp
from jax import lax
from jax.experimental import pallas as pl
from jax.experimental.pallas import tpu as pltpu

F32 = jnp.float32
BF16 = jnp.bfloat16

D_MODEL = 2048
EPS = 1e-6
DKA = 64
DVA = 128
HA = 8
LAMBDA_INIT = 0.8 - 0.6 * math.exp(-0.3 * 0)
DKB = 64
DVB = 128
HB = 8
ROPE_BASE = 10000.0
NC = 64
HC = D_MODEL // NC
GN_EPS = 64e-5
WKV_CHUNK = 64
PAGES_PER_STEP = 16
V7X_VMEM_BYTES = 64 * 1024 * 1024
VMEM_LIMIT = V7X_VMEM_BYTES * 3 // 4
INPROJ_VMEM_LIMIT = V7X_VMEM_BYTES * 7 // 8


def _cparams(sem):
    return pltpu.CompilerParams(dimension_semantics=sem, vmem_limit_bytes=VMEM_LIMIT)


def _rms(x, eps=EPS):
    return x * lax.rsqrt(jnp.mean(x * x, axis=-1, keepdims=True) + eps)


def _sigmoid(x):
    return 1.0 / (1.0 + jnp.exp(-x))


def _dot(a, b):
    return jnp.dot(a, b, preferred_element_type=F32)


def _dot_nt(a, b):
    return lax.dot_general(a, b, (((1,), (1,)), ((), ())), preferred_element_type=F32)


def _dot_tn(a, b):
    return lax.dot_general(a, b, (((0,), (0,)), ((), ())), preferred_element_type=F32)


def _split2(x):
    hi = x.astype(BF16)
    lo = (x - hi.astype(F32)).astype(BF16)
    return hi, lo


def _dot_f32(a, b, dot=_dot):
    ah, al = _split2(a)
    bh, bl = _split2(b)
    return dot(ah, bh) + (dot(ah, bl) + dot(al, bh))


def _inproj0_kernel(x_ref, g_ref, w_ref, ka_ref, va_ref, z_ref, xn_ref):
    j = pl.program_id(1)

    @pl.when(j == 0)
    def _():
        xn_ref[...] = (_rms(x_ref[...]) * g_ref[...]).astype(BF16)

    acc = _dot(xn_ref[...], w_ref[...])
    half = acc.shape[1] // 2

    @pl.when(j == 0)
    def _():
        z_ref[:, :half] = acc[:, :half]
        ka_ref[...] = acc[:, half:]

    @pl.when(j == 1)
    def _():
        va_ref[...] = acc[:, :half]
        z_ref[:, half:] = acc[:, half:]

    @pl.when(j == 2)
    def _():
        z_ref[...] = acc


def _inproj0(x, gain, w_bf16, tm):
    m = x.shape[0]
    tn = 2048
    nj = w_bf16.shape[1] // tn
    return pl.pallas_call(
        _inproj0_kernel,
        grid=(m // tm, nj),
        in_specs=[
            pl.BlockSpec((tm, D_MODEL), lambda i, j: (i, 0)),
            pl.BlockSpec((1, D_MODEL), lambda i, j: (0, 0)),
            pl.BlockSpec((D_MODEL, tn), lambda i, j: (0, j)),
        ],
        out_specs=[
            pl.BlockSpec((tm, tn // 2), lambda i, j: (i, 0)),
            pl.BlockSpec((tm, tn // 2), lambda i, j: (i, 0)),
            pl.BlockSpec((tm, tn), lambda i, j: (i, j // 2)),
        ],
        out_shape=[
            jax.ShapeDtypeStruct((m, tn // 2), F32),
            jax.ShapeDtypeStruct((m, tn // 2), F32),
            jax.ShapeDtypeStruct((m, 2 * tn), F32),
        ],
        scratch_shapes=[pltpu.VMEM((tm, D_MODEL), BF16)],
        compiler_params=pltpu.CompilerParams(dimension_semantics=("arbitrary", "arbitrary"),
                                             vmem_limit_bytes=INPROJ_VMEM_LIMIT),
        name="inproj0",
    )(x, gain.reshape(1, D_MODEL), w_bf16)


def _lambda_full(lq1, lk1, lq2, lk2):
    s1 = jnp.sum(lq1 * lk1, axis=-1, keepdims=True)
    s2 = jnp.sum(lq2 * lk2, axis=-1, keepdims=True)
    return jnp.exp(s1) - jnp.exp(s2) + LAMBDA_INIT


def _alibi_slope(h_vec):
    slope = jnp.zeros(h_vec.shape, F32)
    for h in range(HA):
        slope = jnp.where(h_vec == h, 2.0 ** (-(h + 1)), slope)
    return slope


def _subln(o, gain):
    return _rms(o) * gain * (1.0 - LAMBDA_INIT)


def _dattn_prompt_kernel(ii_ref, jj_ref, q_ref, k_ref, v_ref, lq1, lk1, lq2, lk2, gain_ref, o_ref,
                         m_ref, l_ref, a_ref, *, t, hps):
    hg = pl.program_id(1)
    s = pl.program_id(2)
    i = ii_ref[s]
    j = jj_ref[s]

    @pl.when(j == 0)
    def _():
        m_ref[...] = jnp.full(m_ref.shape, -jnp.inf, F32)
        l_ref[...] = jnp.zeros(l_ref.shape, F32)
        a_ref[...] = jnp.zeros(a_ref.shape, F32)

    def step(diagonal):
        col = lax.broadcasted_iota(jnp.int32, (1, t), 1)
        rel = ((j - i) * t + col - (t - 1)).astype(F32)
        if diagonal:
            keep = (lax.broadcasted_iota(jnp.int32, (t, t), 1) <= lax.broadcasted_iota(jnp.int32, (t, t), 0))
        for hh in range(hps):
            hl = slice(hh * DVA, (hh + 1) * DVA)
            bias = _alibi_slope(jnp.full((1, 1), hg * hps + hh, jnp.int32)) * rel
            q = (q_ref[:, hl] * (DKA ** -0.5)).astype(BF16)
            k = k_ref[:, hl].astype(BF16)
            v = v_ref[:, hl].astype(BF16)
            for half in range(2):
                idx = 2 * hh + half
                sl = slice(half * DKA, (half + 1) * DKA)
                sc = _dot_nt(q[:, sl], k[:, sl]) + bias
                if diagonal:
                    sc = jnp.where(keep, sc, -1e30)
                m_prev = m_ref[idx]
                m_new = jnp.maximum(m_prev, jnp.max(sc, axis=-1, keepdims=True))
                alpha = jnp.exp(m_prev - m_new)
                p = jnp.exp(sc - jnp.concatenate([m_new] * (t // DVA), axis=-1))
                l_ref[idx] = alpha * l_ref[idx] + jnp.sum(p, axis=-1, keepdims=True)
                a_ref[idx] = alpha * a_ref[idx] + _dot(p.astype(BF16), v)
                m_ref[idx] = m_new

    @pl.when(j < i)
    def _():
        step(False)

    @pl.when(j == i)
    def _():
        step(True)
        lam = _lambda_full(lq1[...], lk1[...], lq2[...], lk2[...])
        for hh in range(hps):
            o = a_ref[2 * hh] / l_ref[2 * hh] - lam * (a_ref[2 * hh + 1] / l_ref[2 * hh + 1])
            o_ref[:, hh * DVA:(hh + 1) * DVA] = _subln(o, gain_ref[...])


def _dattn_prompt(z, ka, va, lams, subln_gain, batch, seq, t):
    nt = seq // t
    hps = 4
    pairs = [(i, j) for i in range(nt) for j in range(i + 1)]
    ii = jnp.asarray([p[0] for p in pairs], jnp.int32)
    jj = jnp.asarray([p[1] for p in pairs], jnp.int32)
    vec = pl.BlockSpec((1, DKA), lambda b, h, s, ii, jj: (0, 0))
    grid_spec = pltpu.PrefetchScalarGridSpec(
        num_scalar_prefetch=2,
        grid=(batch, HA // hps, len(pairs)),
        in_specs=[
            pl.BlockSpec((t, hps * DVA), lambda b, h, s, ii, jj: (b * nt + ii[s], h)),
            pl.BlockSpec((t, hps * DVA), lambda b, h, s, ii, jj: (b * nt + jj[s], h)),
            pl.BlockSpec((t, hps * DVA), lambda b, h, s, ii, jj: (b * nt + jj[s], h)),
            vec, vec, vec, vec,
            pl.BlockSpec((1, DVA), lambda b, h, s, ii, jj: (0, 0)),
        ],
        out_specs=pl.BlockSpec((t, hps * DVA), lambda b, h, s, ii, jj: (b * nt + ii[s], h)),
        scratch_shapes=[pltpu.VMEM((2 * hps, t, DVA), F32)] * 3,
    )
    return pl.pallas_call(
        functools.partial(_dattn_prompt_kernel, t=t, hps=hps),
        grid_spec=grid_spec,
        out_shape=jax.ShapeDtypeStruct((batch * seq, HA * DVA), F32),
        compiler_params=_cparams(("arbitrary", "arbitrary", "arbitrary")),
        name="dattn_prompt",
    )(ii, jj, z, ka, va, *[l.reshape(1, DKA) for l in lams], subln_gain.reshape(1, DVA))


def _rows_by_head(row, nrow):
    head = lax.broadcasted_iota(jnp.int32, (nrow, 1), 0) * HA // nrow
    out = jnp.zeros((nrow, DVA), F32)
    for h in range(HA):
        out = jnp.where(head == h, row[:, h * DVA:(h + 1) * DVA], out)
    return out


def _dattn_sample_kernel(pt_ref, q_ref, kn_ref, vn_ref, *rest, n_pages, page, pps):
    kp_refs = rest[:pps]
    vp_refs = rest[pps:2 * pps]
    lq1, lk1, lq2, lk2, gain_ref, o_ref, qm_ref, m_ref, l_ref, a_ref = rest[2 * pps:]
    p = pl.program_id(1)
    nrow = 2 * HA
    rid = lax.broadcasted_iota(jnp.int32, (nrow, 1), 0)
    slope = _alibi_slope(rid // 2)
    past = n_pages * page

    @pl.when(p == 0)
    def _():
        half = lax.broadcasted_iota(jnp.int32, (nrow, DVA), 1) // DKA
        qm_ref[...] = jnp.where(half == rid % 2, _rows_by_head(q_ref[0], nrow), 0.0)
        m_ref[...] = jnp.full(m_ref.shape, -jnp.inf, F32)
        l_ref[...] = jnp.zeros(l_ref.shape, F32)
        a_ref[...] = jnp.zeros(a_ref.shape, F32)

    qm = qm_ref[...]
    qmb = qm.astype(BF16)
    n = lax.broadcasted_iota(jnp.int32, (1, page * HA), 1)
    own_head = n % HA == rid // 2
    scores = []
    for r in range(pps):
        rel = (past - ((p * pps + r) * page + n // HA)).astype(F32)
        kf = kp_refs[r][...].reshape(page * HA, DVA).astype(BF16)
        sc = _dot_nt(qmb, kf) * (DKA ** -0.5) - slope * rel
        scores.append(jnp.where(own_head, sc, -1e30))
    m_new = m_ref[...]
    for sc in scores:
        m_new = jnp.maximum(m_new, jnp.max(sc, axis=-1, keepdims=True))
    alpha = jnp.exp(m_ref[...] - m_new)
    l_new = alpha * l_ref[...]
    a_new = alpha * a_ref[...]
    for r, sc in enumerate(scores):
        pr = jnp.exp(sc - m_new)
        l_new = l_new + jnp.sum(pr, axis=-1, keepdims=True)
        a_new = a_new + _dot(pr.astype(BF16), vp_refs[r][...].reshape(page * HA, DVA).astype(BF16))
    l_ref[...] = l_new
    a_ref[...] = a_new
    m_ref[...] = m_new

    @pl.when(p == n_pages // pps - 1)
    def _():
        sc_n = jnp.sum(qm * _rows_by_head(kn_ref[0], nrow), axis=-1, keepdims=True) * (DKA ** -0.5)
        m_f = jnp.maximum(m_ref[...], sc_n)
        al = jnp.exp(m_ref[...] - m_f)
        pn = jnp.exp(sc_n - m_f)
        l_f = al * l_ref[...] + pn
        a_f = (al * a_ref[...] + pn * _rows_by_head(vn_ref[0], nrow)) / l_f
        lam = _lambda_full(lq1[...], lk1[...], lq2[...], lk2[...])
        outs = []
        for h in range(HA):
            o = a_f[2 * h:2 * h + 1] - lam * a_f[2 * h + 1:2 * h + 2]
            outs.append(_subln(o, gain_ref[...]))
        o_ref[0] = jnp.concatenate(outs, axis=-1)


def _dattn_sample(z, ka, va, cache_k, cache_v, page_table, lams, subln_gain):
    nb, n_pages = page_table.shape
    page = cache_k.shape[1]
    w = HA * DVA
    row3 = lambda a: a.reshape(nb, 1, a.shape[-1])
    vec = pl.BlockSpec((1, DKA), lambda b, p, pt: (0, 0))
    pps = max(d for d in range(1, PAGES_PER_STEP + 1) if n_pages % d == 0)
    page_specs = [
        pl.BlockSpec((None, page, HA, DVA), lambda b, p, pt, r=r: (pt[b * n_pages + p * pps + r], 0, 0, 0))
        for r in range(pps)]
    grid_spec = pltpu.PrefetchScalarGridSpec(
        num_scalar_prefetch=1,
        grid=(nb, n_pages // pps),
        in_specs=[
            pl.BlockSpec((1, 1, w), lambda b, p, pt: (b, 0, 0)),
            pl.BlockSpec((1, 1, w), lambda b, p, pt: (b, 0, 0)),
            pl.BlockSpec((1, 1, w), lambda b, p, pt: (b, 0, 0)),
            *page_specs, *page_specs,
            vec, vec, vec, vec,
            pl.BlockSpec((1, DVA), lambda b, p, pt: (0, 0)),
        ],
        out_specs=pl.BlockSpec((1, 1, w), lambda b, p, pt: (b, 0, 0)),
        scratch_shapes=[pltpu.VMEM((2 * HA, DVA), F32), pltpu.VMEM((2 * HA, 1), F32),
                        pltpu.VMEM((2 * HA, 1), F32), pltpu.VMEM((2 * HA, DVA), F32)],
    )
    out = pl.pallas_call(
        functools.partial(_dattn_sample_kernel, n_pages=n_pages, page=page, pps=pps),
        grid_spec=grid_spec,
        out_shape=jax.ShapeDtypeStruct((nb, 1, w), F32),
        compiler_params=_cparams(("arbitrary", "arbitrary")),
        name="dattn_sample",
    )(page_table.reshape(-1), row3(z), row3(ka), row3(va), *([cache_k] * pps), *([cache_v] * pps),
      *[l.reshape(1, DKA) for l in lams], subln_gain.reshape(1, DVA))
    return out.reshape(nb, w)


def _ret_log_gamma(h):
    return math.log(1.0 - 2.0 ** (-5.0 - h))


def _ret_theta(width):
    theta = 1.0 / (ROPE_BASE ** jnp.linspace(0.0, 1.0, DKB // 2, dtype=F32))
    return jnp.tile(jnp.repeat(theta, 2), width // DKB).reshape(1, width)


def _rotate_pairs(x, cos, sin):
    n = x.shape[-1]
    lane = lax.broadcasted_iota(jnp.int32, x.shape, x.ndim - 1)
    nxt = pltpu.roll(x, n - 1, axis=x.ndim - 1)
    prv = pltpu.roll(x, 1, axis=x.ndim - 1)
    return x * cos + jnp.where(lane % 2 == 0, -nxt, prv) * sin


def _ret_prompt_kernel(q_ref, k_ref, v_ref, g_ref, th_ref, o_ref, s_ref, dec_ref, cos_ref, sin_ref, *, c):
    ci = pl.program_id(1)
    ri = lax.broadcasted_iota(jnp.int32, (c, c), 0)
    cj = lax.broadcasted_iota(jnp.int32, (c, c), 1)
    idx = lax.broadcasted_iota(jnp.int32, (c, 1), 0)

    @pl.when(jnp.logical_and(pl.program_id(0) == 0, ci == 0))
    def _():
        diff = (ri - cj).astype(F32)
        for h in range(HB):
            dec_ref[h] = jnp.where(diff >= 0, jnp.exp(_ret_log_gamma(h) * jnp.maximum(diff, 0.0)), 0.0)
        within = idx.astype(F32) * th_ref[...]
        cos_ref[...] = jnp.cos(within)
        sin_ref[...] = jnp.sin(within)

    @pl.when(ci == 0)
    def _():
        s_ref[...] = jnp.zeros(s_ref.shape, F32)

    start = (ci * c).astype(F32) * th_ref[...]
    c0, s0 = jnp.cos(start), jnp.sin(start)
    cos = jnp.concatenate([cos_ref[...] * c0 - sin_ref[...] * s0] * (HB * DKB // 128), axis=-1)
    sin = jnp.concatenate([sin_ref[...] * c0 + cos_ref[...] * s0] * (HB * DKB // 128), axis=-1)
    qr = _rotate_pairs(q_ref[...], cos, sin)
    kr = _rotate_pairs(k_ref[...], cos, sin) * (DKB ** -0.5)
    idx_f = idx.astype(F32)
    outs = []
    for h in range(HB):
        lg = _ret_log_gamma(h)
        qh = qr[:, h * DKB:(h + 1) * DKB].astype(BF16)
        kh = kr[:, h * DKB:(h + 1) * DKB]
        vh = v_ref[:, h * DVB:(h + 1) * DVB].astype(BF16)
        s_old = s_ref[0, h]
        scores = _dot_nt(qh, kh.astype(BF16)) * dec_ref[h]
        o = _dot(scores.astype(BF16), vh)
        o = o + _dot(qh, s_old.astype(BF16)) * jnp.exp(lg * (idx_f + 1.0))
        kd = (kh * jnp.exp(lg * (c - 1.0 - idx_f))).astype(BF16)
        s_ref[0, h] = math.exp(lg * c) * s_old + _dot_tn(kd, vh)
        gate = g_ref[:, h * DVB:(h + 1) * DVB]
        outs.append(_rms(o) * (gate * _sigmoid(gate)))
    o_ref[...] = jnp.concatenate(outs, axis=-1)


def _ret_prompt(z, batch, seq, c):
    nc = seq // c
    wq = HB * DKB
    wv = HB * DVB
    return pl.pallas_call(
        functools.partial(_ret_prompt_kernel, c=c),
        grid=(batch, nc),
        in_specs=[
            pl.BlockSpec((c, wq), lambda b, i: (b * nc + i, 2)),
            pl.BlockSpec((c, wq), lambda b, i: (b * nc + i, 3)),
            pl.BlockSpec((c, wv), lambda b, i: (b * nc + i, 2)),
            pl.BlockSpec((c, wv), lambda b, i: (b * nc + i, 3)),
            pl.BlockSpec((1, 128), lambda b, i: (0, 0)),
        ],
        out_specs=[
            pl.BlockSpec((c, wv), lambda b, i: (b * nc + i, 0)),
            pl.BlockSpec((1, HB, DKB, DVB), lambda b, i: (b, 0, 0, 0)),
        ],
        out_shape=[
            jax.ShapeDtypeStruct((batch * seq, wv), F32),
            jax.ShapeDtypeStruct((batch, HB, DKB, DVB), F32),
        ],
        scratch_shapes=[pltpu.VMEM((HB, c, c), F32), pltpu.VMEM((c, 128), F32), pltpu.VMEM((c, 128), F32)],
        compiler_params=_cparams(("arbitrary", "arbitrary")),
        name="ret_prompt",
    )(z, z, z, z, _ret_theta(128))


def _ret_sample_kernel(qk_ref, v_ref, g_ref, th_ref, s_ref, o_ref, so_ref, *, pos):
    ang = float(pos) * th_ref[...]
    x = qk_ref[...]
    ncol = x.shape[1]
    d = lax.broadcasted_iota(jnp.int32, x.shape, 0)
    partner = jnp.where(d % 2 == 0, -pltpu.roll(x, DKB - 1, axis=0), pltpu.roll(x, 1, axis=0))
    qk = x * jnp.cos(ang) + partner * jnp.sin(ang)
    is_k = lax.broadcasted_iota(jnp.int32, (1, ncol), 1) % (2 * HB) >= HB
    qkb = (qk * jnp.where(is_k, DKB ** -0.5, 1.0)).astype(BF16)
    sel_row = lax.broadcasted_iota(jnp.int32, (ncol, HB * DVB), 0)
    sel_head = lax.broadcasted_iota(jnp.int32, (ncol, HB * DVB), 1) // DVB
    rows = []
    for s in range(v_ref.shape[0]):
        q_sel = jnp.where(sel_row == s * 2 * HB + sel_head, 1.0, 0.0)
        kv_sel = jnp.where(sel_row == s * 2 * HB + HB + sel_head, v_ref[s:s + 1, :], 0.0)
        res = _dot(qkb, jnp.concatenate([q_sel, kv_sel], axis=-1).astype(BF16))
        outs = []
        for h in range(HB):
            q_lanes = res[:, h * DVB:(h + 1) * DVB]
            outer = res[:, (HB + h) * DVB:(HB + h + 1) * DVB]
            s_new = math.exp(_ret_log_gamma(h)) * s_ref[s, h] + outer
            so_ref[s, h] = s_new
            o = jnp.sum(q_lanes * s_new, axis=0, keepdims=True)
            gate = g_ref[s:s + 1, h * DVB:(h + 1) * DVB]
            outs.append(_rms(o) * (gate * _sigmoid(gate)))
        rows.append(jnp.concatenate(outs, axis=-1))
    o_ref[...] = jnp.concatenate(rows, axis=0)


def _ret_sample(z, state, pos):
    nb = z.shape[0]
    wq = HB * DKB
    wv = HB * DVB
    sb = 8 if nb % 8 == 0 else nb
    qk_t = jnp.transpose(z[:, wv:wv + 2 * wq].reshape(nb, 2 * HB, DKB), (2, 0, 1)).reshape(DKB, nb * 2 * HB)
    theta_col = _ret_theta(DKB).reshape(DKB, 1)
    return pl.pallas_call(
        functools.partial(_ret_sample_kernel, pos=pos),
        grid=(nb // sb,),
        in_specs=[
            pl.BlockSpec((DKB, sb * 2 * HB), lambda b: (0, b)),
            pl.BlockSpec((sb, wv), lambda b: (b, 2)),
            pl.BlockSpec((sb, wv), lambda b: (b, 3)),
            pl.BlockSpec((DKB, 1), lambda b: (0, 0)),
            pl.BlockSpec((sb, HB, DKB, DVB), lambda b: (b, 0, 0, 0)),
        ],
        out_specs=[
            pl.BlockSpec((sb, wv), lambda b: (b, 0)),
            pl.BlockSpec((sb, HB, DKB, DVB), lambda b: (b, 0, 0, 0)),
        ],
        out_shape=[
            jax.ShapeDtypeStruct((nb, wv), F32),
            jax.ShapeDtypeStruct(state.shape, F32),
        ],
        compiler_params=_cparams(("arbitrary",)),
        name="ret_sample",
    )(qk_t, z, z, theta_col, state)


def _mm_resid_kernel(*refs, n_in):
    a_refs = refs[:n_in]
    w_refs = refs[n_in:2 * n_in]
    x_ref = refs[2 * n_in]
    o_ref = refs[2 * n_in + 1]
    acc = x_ref[...]
    for a_ref, w_ref in zip(a_refs, w_refs):
        acc = acc + _dot(a_ref[...].astype(BF16), w_ref[...])
    o_ref[...] = acc


def _mm_resid(a_list, w_bf16, resid, tm):
    m, n = resid.shape
    tn = n
    in_specs = []
    for a in a_list:
        in_specs.append(pl.BlockSpec((tm, a.shape[1]), lambda i, j: (i, 0)))
    row = 0
    for a in a_list:
        kb = a.shape[1]
        in_specs.append(pl.BlockSpec((kb, tn), lambda i, j, r=row // kb: (r, j)))
        row += kb
    in_specs.append(pl.BlockSpec((tm, tn), lambda i, j: (i, j)))
    return pl.pallas_call(
        functools.partial(_mm_resid_kernel, n_in=len(a_list)),
        grid=(m // tm, n // tn),
        in_specs=in_specs,
        out_specs=pl.BlockSpec((tm, tn), lambda i, j: (i, j)),
        out_shape=jax.ShapeDtypeStruct((m, n), F32),
        compiler_params=_cparams(("arbitrary", "arbitrary")),
        name="mm_resid",
    )(*a_list, *([w_bf16] * len(a_list)), resid)


def _ffn_kernel(*refs, final_norm, emit_bf16, aliased):
    x_ref, g_ref, wg_ref, wu_ref, wd_ref, fg_ref = refs[:6]
    refs = refs[6 + aliased:]
    if emit_bf16:
        o_ref, wgb_ref, wub_ref, wdb_ref, xn_ref, acc_ref = refs
    else:
        o_ref, xn_ref, acc_ref = refs
    j = pl.program_id(1)

    @pl.when(j == 0)
    def _():
        xn_ref[...] = (_rms(x_ref[...]) * g_ref[...]).astype(BF16)
        acc_ref[...] = jnp.zeros(acc_ref.shape, F32)

    wg, wu, wd = wg_ref[...].astype(BF16), wu_ref[...].astype(BF16), wd_ref[...].astype(BF16)
    if emit_bf16:
        wgb_ref[...], wub_ref[...], wdb_ref[...] = wg, wu, wd
    xn = xn_ref[...]
    gate = _dot(xn, wg)
    up = _dot(xn, wu)
    hid = (gate * _sigmoid(gate) * up).astype(BF16)
    acc_ref[...] += _dot(hid, wd)

    @pl.when(j == pl.num_programs(1) - 1)
    def _():
        y = x_ref[...] + acc_ref[...]
        if final_norm:
            y = _rms(y) * fg_ref[...]
        o_ref[...] = y


def _ffn_call(x, gain, wg, wu, wd, final_gain, final_norm, tm, tf, row_tiles, done=None):
    m = x.shape[0]
    dff = wg.shape[1]
    t0, t1 = row_tiles
    emit_bf16 = wg.dtype == F32
    w_specs = [pl.BlockSpec((D_MODEL, tf), lambda i, j: (0, j)),
               pl.BlockSpec((D_MODEL, tf), lambda i, j: (0, j)),
               pl.BlockSpec((tf, D_MODEL), lambda i, j: (j, 0))]
    in_specs = [pl.BlockSpec((tm, D_MODEL), lambda i, j: (i + t0, 0)),
                pl.BlockSpec((1, D_MODEL), lambda i, j: (0, 0)),
                *w_specs,
                pl.BlockSpec((1, D_MODEL), lambda i, j: (0, 0))]
    args = [x, gain.reshape(1, D_MODEL), wg, wu, wd, final_gain.reshape(1, D_MODEL)]
    out_specs = [pl.BlockSpec((tm, D_MODEL), lambda i, j: (i + t0, 0))]
    out_shape = [jax.ShapeDtypeStruct((m, D_MODEL), F32)]
    aliases = {}
    if done is not None:
        in_specs.append(pl.BlockSpec(memory_space=pl.ANY))
        args.append(done)
        aliases = {len(args) - 1: 0}
    if emit_bf16:
        out_specs += w_specs
        out_shape += [jax.ShapeDtypeStruct(w.shape, BF16) for w in (wg, wu, wd)]
    res = pl.pallas_call(
        functools.partial(_ffn_kernel, final_norm=final_norm, emit_bf16=emit_bf16, aliased=done is not None),
        grid=(t1 - t0, dff // tf),
        in_specs=in_specs,
        out_specs=out_specs,
        out_shape=out_shape,
        input_output_aliases=aliases,
        scratch_shapes=[pltpu.VMEM((tm, D_MODEL), BF16), pltpu.VMEM((tm, D_MODEL), F32)],
        compiler_params=_cparams(("arbitrary", "arbitrary")),
        name="ffn",
    )(*args)
    return res if emit_bf16 else res[0]


def _ffn(x, gain, weights, final_gain, final_norm, tm, tf=512):
    n_tiles = x.shape[0] // tm
    if weights[0].dtype != F32:
        return _ffn_call(x, gain, *weights, final_gain, final_norm, tm, tf, (0, n_tiles)), weights
    y, *weights_bf16 = _ffn_call(x, gain, *weights, final_gain, final_norm, tm, tf // 2, (0, 1))
    if n_tiles > 1:
        y = _ffn_call(x, gain, *weights_bf16, final_gain, final_norm, tm, tf, (1, n_tiles), done=y)
    return y, tuple(weights_bf16)


def _norm_and_shift(x_ref, g_ref, prev_ref, carry_ref, i, tiles_per_seq):
    h = _rms(x_ref[...]) * g_ref[...]
    if prev_ref is None:
        tm = h.shape[0]
        first = (i % tiles_per_seq) == 0
        prev_row = jnp.where(first, 0.0, carry_ref[0:1, :])
        rows = lax.broadcasted_iota(jnp.int32, (tm, 1), 0)
        shifted = jnp.where(rows == 0, prev_row, pltpu.roll(h, 1, axis=0))
        carry_ref[0:1, :] = h[tm - 1:tm, :]
    else:
        shifted = prev_ref[...]
    return h, shifted - h


def _rwkv_rkv_kernel(*refs, per_row_prev, tiles_per_seq):
    if per_row_prev:
        x_ref, g_ref, prev_ref, mu_ref, w_ref, o_ref, hl_ref, h_ref, xx_ref, carry_ref = refs
    else:
        x_ref, g_ref, mu_ref, w_ref, o_ref, hl_ref, h_ref, xx_ref, carry_ref = refs
        prev_ref = None
    i = pl.program_id(0)
    j = pl.program_id(1)

    @pl.when(j == 0)
    def _():
        h, xx = _norm_and_shift(x_ref, g_ref, prev_ref, carry_ref, i, tiles_per_seq)
        h_ref[...] = h
        xx_ref[...] = xx
        if per_row_prev:
            hl_ref[...] = h
        else:
            hl_ref[0] = h[h.shape[0] - 1:, :]

    xm = (h_ref[...] + xx_ref[...] * mu_ref[0]).astype(BF16)
    o_ref[...] = _dot(xm, w_ref[...])


def _rwkv_rkv(x, gain, prev, mu_rkv, w_rkv, tm, tiles_per_seq):
    m = x.shape[0]
    tn = D_MODEL
    per_seq = D_MODEL // tn
    per_row_prev = prev is not None
    in_specs = [pl.BlockSpec((tm, D_MODEL), lambda i, j: (i, 0)),
                pl.BlockSpec((1, D_MODEL), lambda i, j: (0, 0))]
    args = [x, gain.reshape(1, D_MODEL)]
    if per_row_prev:
        in_specs.append(pl.BlockSpec((tm, D_MODEL), lambda i, j: (i, 0)))
        args.append(prev)
        hl_spec = pl.BlockSpec((tm, D_MODEL), lambda i, j: (i, 0))
        hl_shape = jax.ShapeDtypeStruct((m, D_MODEL), F32)
    else:
        nseq = m // (tm * tiles_per_seq)
        hl_spec = pl.BlockSpec((1, 1, D_MODEL), lambda i, j: (i // tiles_per_seq, 0, 0))
        hl_shape = jax.ShapeDtypeStruct((nseq, 1, D_MODEL), F32)
    in_specs += [pl.BlockSpec((1, 1, D_MODEL), lambda i, j: (j // per_seq, 0, 0)),
                 pl.BlockSpec((None, D_MODEL, tn), lambda i, j: (j // per_seq, 0, j % per_seq))]
    args += [mu_rkv, w_rkv]
    return pl.pallas_call(
        functools.partial(_rwkv_rkv_kernel, per_row_prev=per_row_prev, tiles_per_seq=tiles_per_seq),
        grid=(m // tm, 3 * per_seq),
        in_specs=in_specs,
        out_specs=[pl.BlockSpec((tm, tn), lambda i, j: (i, j)), hl_spec],
        out_shape=[jax.ShapeDtypeStruct((m, 3 * D_MODEL), F32), hl_shape],
        scratch_shapes=[pltpu.VMEM((tm, D_MODEL), F32), pltpu.VMEM((tm, D_MODEL), F32),
                        pltpu.VMEM((8, D_MODEL), F32)],
        compiler_params=_cparams(("arbitrary", "arbitrary")),
        name="rwkv_rkv",
    )(*args)


def _rwkv_lora_kernel(*refs, per_row_prev, tiles_per_seq):
    if per_row_prev:
        (x_ref, g_ref, prev_ref, mu_ref, w0_ref, w1_ref, w2_ref, a0_ref, a1_ref, a2_ref, g1_ref, g2_ref,
         lw_ref, a_ref, go_ref, carry_ref) = refs
    else:
        (x_ref, g_ref, mu_ref, w0_ref, w1_ref, w2_ref, a0_ref, a1_ref, a2_ref, g1_ref, g2_ref,
         lw_ref, a_ref, go_ref, carry_ref) = refs
        prev_ref = None
    h, xx = _norm_and_shift(x_ref, g_ref, prev_ref, carry_ref, pl.program_id(0), tiles_per_seq)
    xw = (h + xx * mu_ref[0]).astype(BF16)
    xa = (h + xx * mu_ref[1]).astype(BF16)
    xg = (h + xx * mu_ref[2]).astype(BF16)
    zw = w0_ref[...] + _dot(jnp.tanh(_dot(xw, w1_ref[...])).astype(BF16), w2_ref[...])
    lw_ref[...] = -math.exp(-0.5) * _sigmoid(zw)
    a_ref[...] = _sigmoid(a0_ref[...] + _dot(_dot(xa, a1_ref[...]).astype(BF16), a2_ref[...]))
    go_ref[...] = _dot(_sigmoid(_dot(xg, g1_ref[...])).astype(BF16), g2_ref[...])


def _rwkv_lora(x, gain, prev, mu_wag, w0, w1, w2, a0, a1, a2, g1, g2, tm, tiles_per_seq):
    m = x.shape[0]
    per_row_prev = prev is not None
    full = lambda a: pl.BlockSpec(a.shape, lambda i: (0,) * a.ndim)
    row = pl.BlockSpec((tm, D_MODEL), lambda i: (i, 0))
    args = [x, gain.reshape(1, D_MODEL)]
    in_specs = [row, full(args[1])]
    if per_row_prev:
        in_specs.append(row)
        args.append(prev)
    rest = [mu_wag, w0.reshape(1, D_MODEL), w1, w2, a0.reshape(1, D_MODEL), a1, a2, g1, g2]
    in_specs += [full(a) for a in rest]
    args += rest
    return pl.pallas_call(
        functools.partial(_rwkv_lora_kernel, per_row_prev=per_row_prev, tiles_per_seq=tiles_per_seq),
        grid=(m // tm,),
        in_specs=in_specs,
        out_specs=[row, row, row],
        out_shape=[jax.ShapeDtypeStruct((m, D_MODEL), F32)] * 3,
        scratch_shapes=[pltpu.VMEM((8, D_MODEL), F32)],
        compiler_params=_cparams(("arbitrary",)),
        name="rwkv_lora",
    )(*args)


def _wkv_keys(k, a, kk_gain, ka_gain, head_sum):
    kk = k * kk_gain
    kk = kk / jnp.maximum(jnp.sqrt(head_sum(kk * kk)), 1e-12)
    return kk, kk * a, k * (1.0 + (a - 1.0) * ka_gain)


def _head_norm(o, head_sum):
    mean = head_sum(o) * (1.0 / NC)
    var = head_sum(jnp.square(o - mean)) * (1.0 / NC)
    return (o - mean) * lax.rsqrt(var + GN_EPS)


def _wkv_finish(on, r, k2, v, g, rk, ln_g, ln_b, head_sum):
    bonus = head_sum(r * k2 * rk) * v
    return (on * ln_g + ln_b + bonus) * g


def _pair_sum(x):
    lane = lax.broadcasted_iota(jnp.int32, x.shape, 1)
    s0 = jnp.sum(x[:, :NC], axis=-1, keepdims=True)
    s1 = jnp.sum(x[:, NC:], axis=-1, keepdims=True)
    return jnp.where(lane < NC, s0, s1)


def _bmm(a, b):
    return jnp.einsum("cij,cjk->cik", a, b, preferred_element_type=F32)


def _bmm_nt(a, b):
    return jnp.einsum("cik,cjk->cij", a, b, preferred_element_type=F32)


def _wkv_prompt_kernel(r_ref, k_ref, v_ref, lw_ref, a_ref, kkg_ref, kag_ref, rk_ref, g_ref, lng_ref, lnb_ref,
                       y_ref, so_ref, st_ref, gh_ref, ghb_ref, e0t_ref, ft_ref, wend_ref, bonus_ref, *, tc, nt):
    c = WKV_CHUNK
    nch = tc // c
    s = pl.program_id(0)

    @pl.when(s == 0)
    def _():
        for ref in (st_ref, gh_ref, ghb_ref, e0t_ref, ft_ref, wend_ref, bonus_ref):
            ref[...] = jnp.zeros(ref.shape, ref.dtype)

    prev = jnp.maximum(s - 1, 0)
    slot = prev % 2
    first = prev % nt == 0
    ss = [jnp.where(first, 0.0, st_ref[0]), jnp.where(first, 0.0, st_ref[1])]
    o_chunks = [[], []]
    lane_sum = lambda t: jnp.sum(t, axis=-1, keepdims=True)
    n_points = 2 + int(math.log2(c))
    done = [0]

    def advance_stage2(point):
        upto = nch * (point + 1) // n_points
        for ch in range(done[0], upto):
            w_end = wend_ref[slot, ch]
            for hh in range(2):
                n = hh * nch + ch
                other = slice((1 - hh) * NC, (2 - hh) * NC)
                sb = ss[hh].astype(BF16)
                o = _dot_nt(ghb_ref[slot, n], sb) + gh_ref[slot, n][:, other]
                o_chunks[hh].append(_head_norm(o, lane_sum))
                ss[hh] = ss[hh] * w_end + _dot(sb, e0t_ref[slot, n]) + ft_ref[slot, n]
        done[0] = upto

    ri = lax.broadcasted_iota(jnp.int32, (c, c), 0)
    cj = lax.broadcasted_iota(jnp.int32, (c, c), 1)
    tri = (ri >= cj).astype(BF16)
    gi = lax.broadcasted_iota(jnp.int32, (2 * c, 2 * c), 0)
    gj = lax.broadcasted_iota(jnp.int32, (2 * c, 2 * c), 1) % c
    gmask = (gj < jnp.where(gi < c, gi, gi - c + 1))[None]

    r = r_ref[...]
    v = v_ref[...]
    kk, bv, k2 = _wkv_keys(k_ref[...], a_ref[...], kkg_ref[...], kag_ref[...], _pair_sum)
    to3 = lambda x: x.reshape(nch, c, 2 * NC)
    lw3 = to3(lw_ref[...])
    p0 = lw3.astype(BF16)
    r1 = lw3 - p0.astype(F32)
    p1 = r1.astype(BF16)
    p2 = (r1 - p1.astype(F32)).astype(BF16)
    trib = jnp.broadcast_to(tri[None], (nch, c, c))
    cum = _bmm(trib, p0) + (_bmm(trib, p1) + _bmm(trib, p2))
    cum_last = cum[:, c - 1:c, :]
    w_rest = jnp.exp(cum_last - cum)
    w_inv = jnp.exp(-cum)
    am = -to3(kk) * jnp.exp(cum - lw3)
    rm = to3(r) * jnp.exp(cum)
    bm = (to3(bv) * w_inv).astype(BF16)
    km = (to3(k2) * w_inv).astype(BF16)
    bt = (to3(bv) * w_rest).astype(BF16)
    kt = (to3(k2) * w_rest).astype(BF16)
    v3 = to3(v)
    advance_stage2(0)

    both = lambda x: jnp.concatenate([x, x], axis=0)
    lane = lax.broadcasted_iota(jnp.int32, (2 * nch, 1, 2 * NC), 2)
    head = lax.broadcasted_iota(jnp.int32, (2 * nch, 1, 2 * NC), 0) // nch
    own = (lane // NC) == head
    v_sw = both(pltpu.roll(v, NC, axis=1).reshape(nch, c, 2 * NC))
    x1 = both(jnp.concatenate([am, rm], axis=1).astype(BF16))
    x2 = jnp.where(own, both(jnp.concatenate([bm, km], axis=1)), 0.0)
    g4 = jnp.where(gmask, _bmm_nt(x1, x2), 0.0)
    lv = _bmm(g4[:, :, c:].astype(BF16), v_sw.astype(BF16))
    advance_stage2(1)
    x = jnp.where(own, both(am), lv[:, :c])
    lp = g4[:, :c, :c].astype(BF16)
    steps = int(math.log2(c))
    for i in range(steps - 1):
        prod = _bmm(lp, jnp.concatenate([x.astype(BF16), lp], axis=-1))
        x = x + prod[:, :, :2 * NC]
        lp = prod[:, :, 2 * NC:].astype(BF16)
        advance_stage2(2 + i)
    x = x + _bmm(lp, x.astype(BF16))
    advance_stage2(n_points - 1)
    gh_new = _bmm(g4[:, c:, :c].astype(BF16), x.astype(BF16)) + jnp.where(own, both(rm), lv[:, c:])
    btk = jnp.where(own, both(jnp.concatenate([bt, kt], axis=1)), 0.0)
    pqv = jnp.concatenate([x, jnp.where(own, 0.0, v_sw)], axis=1).astype(BF16)
    st_ref[0] = ss[0]
    st_ref[1] = ss[1]
    so_ref[0, 0] = ss[0][:, :NC]
    so_ref[0, 1] = ss[1][:, NC:]
    on = jnp.concatenate([jnp.concatenate(oc, axis=0) for oc in o_chunks], axis=-1)
    y_ref[...] = (on * lng_ref[...] + lnb_ref[...] + bonus_ref[slot]) * g_ref[...]

    nxt = s % 2
    gh_ref[nxt] = gh_new
    ghb_ref[nxt] = gh_new.astype(BF16)
    for n in range(2 * nch):
        eft = _dot_tn(pqv[n], btk[n])
        e0t_ref[nxt, n] = eft.astype(BF16)
        ft_ref[nxt, n] = eft[NC:] if n < nch else eft[:NC]
    wend_ref[nxt] = jnp.exp(cum_last)
    bonus_ref[nxt] = _pair_sum(r * k2 * rk_ref[...]) * v


def _wkv_prompt(rkv, lw, a, g, kk_gain, ka_gain, rk, ln_g, ln_b, batch, seq, tc):
    nt = seq // tc
    nhp = HC // 2
    nblk = batch * nhp * nt
    nch = tc // WKV_CHUNK

    def rows_cols(blk):
        b = blk // (nhp * nt)
        return b * nt + blk % nt, (blk // nt) % nhp, b

    cur = lambda s: jnp.minimum(s, nblk - 1)
    prv = lambda s: jnp.maximum(s - 1, 0)
    tok = lambda off, which: pl.BlockSpec(
        (tc, 2 * NC), lambda s: (rows_cols(which(s))[0], off + rows_cols(which(s))[1]))
    par = lambda which: pl.BlockSpec((1, 2 * NC), lambda s: (0, rows_cols(which(s))[1]))
    return pl.pallas_call(
        functools.partial(_wkv_prompt_kernel, tc=tc, nt=nt),
        grid=(nblk + 1,),
        in_specs=[tok(0, cur), tok(nhp, cur), tok(2 * nhp, cur), tok(0, cur), tok(0, cur),
                  par(cur), par(cur), par(cur), tok(0, prv), par(prv), par(prv)],
        out_specs=[
            tok(0, prv),
            pl.BlockSpec((1, 2, NC, NC), lambda s: (rows_cols(prv(s))[2], rows_cols(prv(s))[1], 0, 0)),
        ],
        out_shape=[
            jax.ShapeDtypeStruct((batch * seq, D_MODEL), F32),
            jax.ShapeDtypeStruct((batch, HC, NC, NC), F32),
        ],
        scratch_shapes=[
            pltpu.VMEM((2, NC, 2 * NC), F32),
            pltpu.VMEM((2, 2 * nch, WKV_CHUNK, 2 * NC), F32),
            pltpu.VMEM((2, 2 * nch, WKV_CHUNK, 2 * NC), BF16),
            pltpu.VMEM((2, 2 * nch, 2 * NC, 2 * NC), BF16),
            pltpu.VMEM((2, 2 * nch, NC, 2 * NC), F32),
            pltpu.VMEM((2, nch, 1, 2 * NC), F32),
            pltpu.VMEM((2, tc, 2 * NC), F32),
        ],
        compiler_params=_cparams(("arbitrary",)),
        name="wkv_prompt",
    )(rkv, rkv, rkv, lw, a, kk_gain, ka_gain, rk, g, ln_g, ln_b)


def _wkv_sample_kernel(r_ref, k_ref, v_ref, lw_ref, a_ref, g_ref, kkg_ref, kag_ref, rk_ref, lng_ref, lnb_ref,
                       s_ref, y_ref, so_ref, o_ref):
    ys = []
    col_sum = lambda x: jnp.sum(x, axis=0, keepdims=True)
    for hh in range(2):
        sl = slice(hh * NC, (hh + 1) * NC)
        t = lambda ref: ref[...].T[sl, :]
        r, k, v, a, g = t(r_ref), t(k_ref), t(v_ref), t(a_ref), t(g_ref)
        w = jnp.exp(t(lw_ref))
        kk, bv, k2 = _wkv_keys(k, a, kkg_ref[sl, :], kag_ref[sl, :], col_sum)
        for i in range(NC):
            s_old = s_ref[hh, i]
            sk = jnp.sum(s_old * kk, axis=0, keepdims=True)
            s_new = s_old * w - sk * bv + v[i:i + 1, :] * k2
            so_ref[hh, i] = s_new
            o_ref[i:i + 1, :] = jnp.sum(s_new * r, axis=0, keepdims=True)
        ys.append(_wkv_finish(_head_norm(o_ref[...], col_sum), r, k2, v, g,rk_ref[sl, :], lng_ref[sl, :], lnb_ref[sl, :], col_sum))
    y_ref[...] = jnp.concatenate(ys, axis=0).T


def _wkv_sample(rkv, lw, a, g, kk_gain, ka_gain, rk, ln_g, ln_b, state_t):
    nb = lw.shape[0]
    nhp = HC // 2
    tok = lambda off: pl.BlockSpec((nb, 2 * NC), lambda hp: (0, off + hp))
    par = pl.BlockSpec((2 * NC, 1), lambda hp: (hp, 0))
    col = lambda p: p.reshape(D_MODEL, 1)
    st_spec = pl.BlockSpec((2, NC, NC, nb), lambda hp: (hp, 0, 0, 0))
    return pl.pallas_call(
        _wkv_sample_kernel,
        grid=(nhp,),
        in_specs=[tok(0), tok(nhp), tok(2 * nhp), tok(0), tok(0), tok(0), par, par, par, par, par, st_spec],
        out_specs=[pl.BlockSpec((nb, 2 * NC), lambda hp: (0, hp)), st_spec],
        out_shape=[
            jax.ShapeDtypeStruct((nb, D_MODEL), F32),
            jax.ShapeDtypeStruct(state_t.shape, F32),
        ],
        scratch_shapes=[pltpu.VMEM((NC, nb), F32)],
        compiler_params=_cparams(("arbitrary",)),
        name="wkv_sample",
    )(rkv, rkv, rkv, lw, a, g, col(kk_gain), col(ka_gain), col(rk), col(ln_g), col(ln_b), state_t)


def _row_tile(m, want):
    return min(m, want)


def _trunk(x, batch, seq, past, W):
    m = x.shape[0]
    prompt = past is None
    tm = _row_tile(m, 512)
    lams = (W["lambda_q1"], W["lambda_k1"], W["lambda_q2"], W["lambda_k2"])

    ka, va, z = _inproj0(x, W["norm0_mix"], W["w_in0"], tm)
    if prompt:
        o_a = _dattn_prompt(z, ka, va, lams, W["subln_gain"], batch, seq, min(seq, 512))
        o_b, ret_new = _ret_prompt(z, batch, seq, min(seq, 256))
    else:
        o_a = _dattn_sample(z, ka, va, past["cache_k"], past["cache_v"], past["page_table"], lams,
                            W["subln_gain"])
        o_b, ret_new = _ret_sample(z, past["state_ret"], past["pos"])
    x = _mm_resid([o_a, o_b], W["w_out0"], x, _row_tile(m, 512))
    x, W["ffn0"] = _ffn(x, W["norm0_ffn"], W["ffn0"], W["norm_final"], False, tm)

    tiles_per_seq = max(seq // tm, 1)
    prev = None if prompt else past["state_shift"]
    rkv, hlast = _rwkv_rkv(x, W["norm1_mix"], prev, W["mu_rkv"], W["w_rkv"], tm, tiles_per_seq)
    tl = _row_tile(m, 256)
    lw, a, g = _rwkv_lora(x, W["norm1_mix"], prev, W["mu_wag"], W["decay_w0"], W["decay_w1"], W["decay_w2"],
                          W["aaa_a0"], W["aaa_a1"], W["aaa_a2"], W["gate_g1"], W["gate_g2"],
                          tl, max(seq // tl, 1))
    head_params = (W["k_k"], W["k_a"], W["r_k"], W["lnx_gain"], W["lnx_bias"])
    if prompt:
        y, wkv_new = _wkv_prompt(rkv, lw, a, g, *head_params, batch, seq, min(seq, 512))
        shift_new = hlast.reshape(batch, D_MODEL)
    else:
        state_t = jnp.transpose(past["state_wkv"], (1, 2, 3, 0))
        y, wkv_t = _wkv_sample(rkv, lw, a, g, *head_params, state_t)
        wkv_new = jnp.transpose(wkv_t, (3, 0, 1, 2))
        shift_new = hlast
    x = _mm_resid([y], W["w_out1"], x, _row_tile(m, 512))
    x, W["ffn1"] = _ffn(x, W["norm1_ffn"], W["ffn1"], W["norm_final"], True, tm)
    return x, ka, va, ret_new, wkv_new, shift_new


def kernel(x_prompt, x_sample, cache_k, cache_v, page_table, state_ret, state_wkv, state_shift, norm0_mix, w_in0, lambda_q1, lambda_k1, lambda_q2, lambda_k2, subln_gain, w_out0, norm0_ffn, w_gate0, w_up0, w_down0, norm1_mix, mu1, w_r1, w_k1, w_v1, decay_w0, decay_w1, decay_w2, aaa_a0, aaa_a1, aaa_a2, gate_g1, gate_g2, k_k, k_a, r_k, lnx_gain, lnx_bias, w_out1, norm1_ffn, w_gate1, w_up1, w_down1, norm_final):
    bf = lambda w: w.astype(BF16)
    row = lambda p: p.reshape(1, D_MODEL)
    W = dict(
        norm0_mix=norm0_mix, w_in0=bf(w_in0), lambda_q1=lambda_q1, lambda_k1=lambda_k1, lambda_q2=lambda_q2,
        lambda_k2=lambda_k2, subln_gain=subln_gain, w_out0=bf(w_out0), norm0_ffn=norm0_ffn,
        ffn0=(w_gate0, w_up0, w_down0),
        norm1_mix=norm1_mix,
        mu_rkv=jnp.stack([mu1[0], mu1[2], mu1[3]]).reshape(3, 1, D_MODEL),
        mu_wag=jnp.stack([mu1[1], mu1[4], mu1[5]]).reshape(3, 1, D_MODEL),
        w_rkv=jnp.stack([bf(w_r1), bf(w_k1), bf(w_v1)]),
        decay_w0=decay_w0, decay_w1=bf(decay_w1), decay_w2=bf(decay_w2),
        aaa_a0=aaa_a0, aaa_a1=bf(aaa_a1), aaa_a2=bf(aaa_a2), gate_g1=bf(gate_g1), gate_g2=bf(gate_g2),
        k_k=row(k_k), k_a=row(k_a), r_k=row(r_k), lnx_gain=row(lnx_gain), lnx_bias=row(lnx_bias),
        w_out1=bf(w_out1), norm1_ffn=norm1_ffn, ffn1=(w_gate1, w_up1, w_down1),
        norm_final=norm_final,
    )
    bsz, seq = x_prompt.shape[:2]
    y_p, k_p, v_p, ret_p, wkv_p, shift_p = _trunk(x_prompt.reshape(bsz * seq, D_MODEL), bsz, seq, None, W)
    nb, dseq = x_sample.shape[:2]
    past_len = page_table.shape[1] * cache_k.shape[1]
    past = dict(cache_k=cache_k, cache_v=cache_v, page_table=page_table, state_ret=state_ret,
                state_wkv=state_wkv, state_shift=state_shift, pos=past_len)
    y_s, k_s, v_s, ret_s, wkv_s, shift_s = _trunk(x_sample.reshape(nb * dseq, D_MODEL), nb, dseq, past, W)
    return (y_p.reshape(bsz, seq, D_MODEL), y_s.reshape(nb, dseq, D_MODEL),
            k_p.reshape(bsz, seq, HA, 2 * DKA), v_p.reshape(bsz, seq, HA, DVA),
            k_s.reshape(nb, dseq, HA, 2 * DKA), v_s.reshape(nb, dseq, HA, DVA),
            ret_p, ret_s, wkv_p, wkv_s, shift_p, shift_s)
```
